```python
import math
import jax
import jax.numpy as jnp
from jax import lax
import numpy as np

D_MODEL = 1024
BATCH = 8
SEQ = 4096
DEPTH = 4

HEAD_DIM = 64
CONV_CH = 256
CONV_WIDTH = 31
N_Q_HEADS = 12
N_KV_HEADS = 4
Q_PER_KV = N_Q_HEADS // N_KV_HEADS
NSA_WIDTH = N_Q_HEADS * HEAD_DIM
MIX_WIDTH = CONV_CH + NSA_WIDTH
KV_WIDTH = N_KV_HEADS * HEAD_DIM
CMP_LEN = 32
CMP_STRIDE = 16
CMP_HIDDEN = 256
SEL_BLOCK = 64
SEL_TOPK = 8
WINDOW = 512
Q_BLOCK = 64
N_BRANCH = 3
FORCED_BONUS = 1000.0
D_FF = -(-8 * D_MODEL // (3 * 256)) * 256
IN_COLS = 2 * CONV_CH + NSA_WIDTH + 6 * KV_WIDTH + N_BRANCH * N_Q_HEADS
EPS = 1e-6

kernel_name = "hymba_conformer_nsa_alibi_trunk"


def rms_norm(x, g):
    xf = x.astype(jnp.float32)
    y = xf * lax.rsqrt(jnp.mean(xf * xf, axis=-1, keepdims=True) + EPS)
    return (y * g.astype(jnp.float32)).astype(x.dtype)


def layer_norm(x, g, b):
    xf = x.astype(jnp.float32)
    mu = jnp.mean(xf, axis=-1, keepdims=True)
    var = jnp.mean(jnp.square(xf - mu), axis=-1, keepdims=True)
    y = (xf - mu) * lax.rsqrt(var + EPS)
    return (y * g.astype(jnp.float32) + b.astype(jnp.float32)).astype(x.dtype)


def alibi_slopes(n):
    def pow2_slopes(m):
        start = 2.0 ** (-8.0 / m)
        return [start ** (i + 1) for i in range(m)]
    if math.log2(n).is_integer():
        s = pow2_slopes(n)
    else:
        c = 2 ** math.floor(math.log2(n))
        s = pow2_slopes(c) + pow2_slopes(2 * c)[0::2][: n - c]
    return np.asarray(s, dtype=np.float32)


def masked_softmax(s, mask):
    s = jnp.where(mask, s.astype(jnp.float32), -jnp.inf)
    m = jnp.max(s, axis=-1, keepdims=True)
    m = jnp.where(jnp.isfinite(m), m, 0.0)
    p = jnp.exp(s - m)
    return p / jnp.maximum(jnp.sum(p, axis=-1, keepdims=True), 1e-30)


def conv_mixer(a, w_dw, b_dw, ln_g, ln_b):
    u, v = jnp.split(a, 2, axis=-1)
    y = u * jax.nn.sigmoid(v)
    y = lax.conv_general_dilated(
        y, w_dw[:, None, :], window_strides=(1,),
        padding=[(CONV_WIDTH - 1, 0)],
        dimension_numbers=("NWC", "WIO", "NWC"),
        feature_group_count=CONV_CH) + b_dw
    y = layer_norm(y, ln_g, ln_b)
    return jax.nn.silu(y)


def compress(kr, pe, w1, w2):
    b, t = kr.shape[:2]
    chunks = kr.reshape(b, t // CMP_STRIDE, CMP_STRIDE, N_KV_HEADS, HEAD_DIM)
    lo = jnp.einsum("bclgd,ldh->bcgh", chunks, w1[:CMP_STRIDE])
    hi = jnp.einsum("bclgd,ldh->bcgh", chunks, w1[CMP_STRIDE:])
    h = lo[:, :-1] + hi[:, 1:] + jnp.einsum("ld,ldh->h", pe, w1)
    return jnp.einsum("bngh,hd->bngd", jax.nn.silu(h), w2)


def nsa_attention(q, kc, vc, ks, vs, kw, vw, gates):
    b, t = q.shape[:2]
    n_cmp = kc.shape[1]
    n_sel = t // SEL_BLOCK
    k_top = min(SEL_TOPK, n_sel)
    scale = HEAD_DIM ** -0.5
    slopes = jnp.asarray(alibi_slopes(N_Q_HEADS)).reshape(N_KV_HEADS, Q_PER_KV)[None, :, :, None, None]
    cmp_idx = jnp.arange(n_cmp)
    cmp_start = cmp_idx * CMP_STRIDE
    cmp_center = cmp_start.astype(jnp.float32) + 0.5 * (CMP_LEN - 1)
    cmp_end = cmp_start + CMP_LEN - 1
    sel_start = jnp.arange(n_sel) * SEL_BLOCK
    overlap = ((cmp_start[:, None] < sel_start[None, :] + SEL_BLOCK)
               & (cmp_start[:, None] + CMP_LEN > sel_start[None, :])).astype(jnp.float32)
    ks_blk = ks.reshape(b, n_sel, SEL_BLOCK, N_KV_HEADS, HEAD_DIM).transpose(0, 3, 1, 2, 4)
    vs_blk = vs.reshape(b, n_sel, SEL_BLOCK, N_KV_HEADS, HEAD_DIM).transpose(0, 3, 1, 2, 4)
    pad = ((0, 0), (WINDOW, 0), (0, 0), (0, 0))
    kw_pad = jnp.pad(kw, pad)
    vw_pad = jnp.pad(vw, pad)
    b_idx = jnp.arange(b)[:, None, None, None]
    g_idx = jnp.arange(N_KV_HEADS)[None, :, None, None]
    offs = jnp.arange(SEL_BLOCK)
    win_offs = jnp.arange(WINDOW + Q_BLOCK) - WINDOW
    sel_j = jnp.arange(n_sel)

    def block(i):
        t0 = i * Q_BLOCK
        tq = t0 + jnp.arange(Q_BLOCK)
        tqf = tq.astype(jnp.float32)
        qb = lax.dynamic_slice_in_dim(q, t0, Q_BLOCK, axis=1) * scale
        gb = lax.dynamic_slice_in_dim(gates, t0, Q_BLOCK, axis=1)

        s_c = jnp.einsum("bqgrd,bngd->bgrqn", qb, kc) - slopes * (tqf[:, None] - cmp_center[None, :])
        p_c = masked_softmax(s_c, cmp_end[None, :] <= tq[:, None])
        o_c = jnp.einsum("bgrqn,bngd->bqgrd", p_c, vc)

        imp = jnp.einsum("bgrqn,nj->bgqj", p_c, overlap)
        cur = tq // SEL_BLOCK
        forced = ((sel_j[None, :] == 0) | (sel_j[None, :] == cur[:, None])
                  | (sel_j[None, :] == cur[:, None] - 1))
        imp = jnp.where(sel_j[None, :] <= cur[:, None],
                        imp + FORCED_BONUS * forced.astype(jnp.float32), -jnp.inf)
        top_s, top_i = lax.top_k(imp, k_top)

        kg = ks_blk[b_idx, g_idx, top_i]
        vg = vs_blk[b_idx, g_idx, top_i]
        pos = top_i[..., None] * SEL_BLOCK + offs
        mask_s = jnp.isfinite(top_s)[..., None] & (pos <= tq[None, None, :, None, None])
        dist_s = (tqf[None, None, :, None, None] - pos.astype(jnp.float32))[:, :, None]
        s_s = jnp.einsum("bqgrd,bgqksd->bgrqks", qb, kg) - slopes[..., None] * dist_s
        s_s = s_s.reshape(b, N_KV_HEADS, Q_PER_KV, Q_BLOCK, k_top * SEL_BLOCK)
        p_s = masked_softmax(s_s, mask_s.reshape(b, N_KV_HEADS, 1, Q_BLOCK, k_top * SEL_BLOCK))
        o_s = jnp.einsum("bgrqs,bgqsd->bqgrd", p_s,
                         vg.reshape(b, N_KV_HEADS, Q_BLOCK, k_top * SEL_BLOCK, HEAD_DIM))

        kwb = lax.dynamic_slice_in_dim(kw_pad, t0, WINDOW + Q_BLOCK, axis=1)
        vwb = lax.dynamic_slice_in_dim(vw_pad, t0, WINDOW + Q_BLOCK, axis=1)
        pos_w = t0 + win_offs
        dist_w = tq[:, None] - pos_w[None, :]
        mask_w = (dist_w >= 0) & (dist_w < WINDOW) & (pos_w[None, :] >= 0)
        s_w = jnp.einsum("bqgrd,bkgd->bgrqk", qb, kwb) - slopes * dist_w.astype(jnp.float32)
        p_w = masked_softmax(s_w, mask_w)
        o_w = jnp.einsum("bgrqk,bkgd->bqgrd", p_w, vwb)

        out = gb[..., 0:1] * o_c + gb[..., 1:2] * o_s + gb[..., 2:3] * o_w
        return out.astype(q.dtype)

    outs = lax.map(block, jnp.arange(t // Q_BLOCK))
    return jnp.moveaxis(outs, 0, 1).reshape(b, t, NSA_WIDTH)


def hybrid_layer(x, attn_norm, w_in, conv_w, conv_b, conv_ln_g, conv_ln_b,
                 cmp_k_pe, cmp_k_w1, cmp_k_w2, cmp_v_pe, cmp_v_w1, cmp_v_w2,
                 w_out, ffn_norm, w_gate_up, w_down):
    b, t, _ = x.shape
    h = rms_norm(x, attn_norm)
    z = h @ w_in
    splits = np.cumsum([2 * CONV_CH, NSA_WIDTH] + [KV_WIDTH] * 6).tolist()
    a_conv, q, kc_r, vc_r, ks, vs, kw, vw, g = jnp.split(z, splits, axis=-1)
    kv_shape = (b, t, N_KV_HEADS, HEAD_DIM)
    conv_out = conv_mixer(a_conv, conv_w, conv_b, conv_ln_g, conv_ln_b)
    kc = compress(kc_r.reshape(kv_shape), cmp_k_pe, cmp_k_w1, cmp_k_w2)
    vc = compress(vc_r.reshape(kv_shape), cmp_v_pe, cmp_v_w1, cmp_v_w2)
    gates = jax.nn.sigmoid(g).reshape(b, t, N_KV_HEADS, Q_PER_KV, N_BRANCH)
    nsa_out = nsa_attention(q.reshape(b, t, N_KV_HEADS, Q_PER_KV, HEAD_DIM), kc, vc,
                            ks.reshape(kv_shape), vs.reshape(kv_shape),
                            kw.reshape(kv_shape), vw.reshape(kv_shape), gates)
    mix = jnp.concatenate([conv_out, nsa_out.astype(conv_out.dtype)], axis=-1)
    x = x + mix @ w_out
    h = rms_norm(x, ffn_norm)
    gu = h @ w_gate_up
    gate, up = jnp.split(gu, 2, axis=-1)
    return x + (jax.nn.silu(gate) * up) @ w_down


def setup_inputs(seed: int = 0) -> dict:
    key = jax.random.key(seed)
    ks = jax.random.split(key, 20)
    f32 = jnp.float32

    def nrm(k, shape, scale):
        return jax.random.normal(k, shape, f32) * scale

    return {
        "x": nrm(ks[0], (BATCH, SEQ, D_MODEL), 1.0),
        "attn_norm": 1.0 + nrm(ks[1], (DEPTH, D_MODEL), 0.05),
        "w_in": nrm(ks[2], (DEPTH, D_MODEL, IN_COLS), D_MODEL ** -0.5),
        "conv_w": nrm(ks[3], (DEPTH, CONV_WIDTH, CONV_CH), CONV_WIDTH ** -0.5),
        "conv_b": nrm(ks[4], (DEPTH, CONV_CH), 0.01),
        "conv_ln_g": 1.0 + nrm(ks[5], (DEPTH, CONV_CH), 0.05),
        "conv_ln_b": nrm(ks[6], (DEPTH, CONV_CH), 0.01),
        "cmp_k_pe": nrm(ks[7], (DEPTH, CMP_LEN, HEAD_DIM), 0.1),
        "cmp_k_w1": nrm(ks[8], (DEPTH, CMP_LEN, HEAD_DIM, CMP_HIDDEN), (CMP_LEN * HEAD_DIM) ** -0.5),
        "cmp_k_w2": nrm(ks[9], (DEPTH, CMP_HIDDEN, HEAD_DIM), CMP_HIDDEN ** -0.5),
        "cmp_v_pe": nrm(ks[10], (DEPTH, CMP_LEN, HEAD_DIM), 0.1),
        "cmp_v_w1": nrm(ks[11], (DEPTH, CMP_LEN, HEAD_DIM, CMP_HIDDEN), (CMP_LEN * HEAD_DIM) ** -0.5),
        "cmp_v_w2": nrm(ks[12], (DEPTH, CMP_HIDDEN, HEAD_DIM), CMP_HIDDEN ** -0.5),
        "w_out": nrm(ks[13], (DEPTH, MIX_WIDTH, D_MODEL), MIX_WIDTH ** -0.5),
        "ffn_norm": 1.0 + nrm(ks[14], (DEPTH, D_MODEL), 0.05),
        "w_gate_up": nrm(ks[15], (DEPTH, D_MODEL, 2 * D_FF), D_MODEL ** -0.5),
        "w_down": nrm(ks[16], (DEPTH, D_FF, D_MODEL), D_FF ** -0.5),
        "final_norm": 1.0 + nrm(ks[17], (D_MODEL,), 0.05),
    }


def reference(x, attn_norm, w_in, conv_w, conv_b, conv_ln_g, conv_ln_b,
              cmp_k_pe, cmp_k_w1, cmp_k_w2, cmp_v_pe, cmp_v_w1, cmp_v_w2,
              w_out, ffn_norm, w_gate_up, w_down, final_norm):
    for l in range(DEPTH):
        x = hybrid_layer(x, attn_norm[l], w_in[l], conv_w[l], conv_b[l], conv_ln_g[l], conv_ln_b[l],
                         cmp_k_pe[l], cmp_k_w1[l], cmp_k_w2[l], cmp_v_pe[l], cmp_v_w1[l], cmp_v_w2[l],
                         w_out[l], ffn_norm[l], w_gate_up[l], w_down[l])
    return rms_norm(x, final_norm)
```

```python
import functools
import math

import jax
import jax.numpy as jnp
import numpy as np
from jax import lax
from jax.experimental import pallas as pl
from jax.experimental.pallas import tpu as pltpu

F32 = jnp.float32
BF16 = jnp.bfloat16

D_MODEL = 1024
HEAD_DIM = 64
CONV_CH = 256
CONV_WIDTH = 31
N_Q_HEADS = 12
N_KV_HEADS = 4
Q_PER_KV = N_Q_HEADS // N_KV_HEADS
NSA_WIDTH = N_Q_HEADS * HEAD_DIM
KV_WIDTH = N_KV_HEADS * HEAD_DIM
CMP_LEN = 32
CMP_STRIDE = 16
CMP_HIDDEN = 256
SEL_BLOCK = 64
SEL_TOPK = 8
WINDOW = 512
N_BRANCH = 3
FORCED_BONUS = 1000.0
D_FF = 2816
EPS = 1e-6

LANES = 128
GATE_PAD = LANES
FF_CHUNK = 256
N_FF_CHUNKS = D_FF // FF_CHUNK
TM_PROJ = 512
TC_CONV = 512
CONV_HALO = 32
TQ = 128
MASK_NEG = -(2.0 ** 100)
M_INIT = -1e30
VMEM_LIMIT = 56 * 1024 * 1024


def _alibi_slopes(n):
    def pow2_slopes(m):
        start = 2.0 ** (-8.0 / m)
        return [start ** (i + 1) for i in range(m)]
    if math.log2(n).is_integer():
        s = pow2_slopes(n)
    else:
        c = 2 ** math.floor(math.log2(n))
        s = pow2_slopes(c) + pow2_slopes(2 * c)[0::2][: n - c]
    return np.asarray(s, dtype=np.float32)


def _sigmoid(v):
    return 1.0 / (1.0 + jnp.exp(-v))


def _dot(a, b):
    return jnp.dot(a, b, preferred_element_type=F32)


def _dot_nt(a, b):
    return lax.dot_general(a, b, (((1,), (1,)), ((), ())), preferred_element_type=F32)


def _const_spec(shape):
    nd = len(shape)
    return pl.BlockSpec(shape, lambda *_: (0,) * nd, pipeline_mode=pl.Buffered(1))


C_A = 0
C_Q = C_A + 2 * CONV_CH
C_KC = C_Q + NSA_WIDTH
C_VC = C_KC + KV_WIDTH
C_KS = C_VC + KV_WIDTH
C_VS = C_KS + KV_WIDTH
C_KW = C_VS + KV_WIDTH
C_VW = C_KW + KV_WIDTH
C_G = C_VW + KV_WIDTH
IN_COLS_PAD = C_G + GATE_PAD


def _inproj_kernel(x_ref, g_ref, w_ref, a_ref, q_ref, kcr_ref, vcr_ref, ksa_ref, vs_ref, kw_ref,
                   vw_ref, gate_ref):
    tm = x_ref.shape[1]
    x = x_ref[0]
    ms = jnp.mean(x * x, axis=-1, keepdims=True)
    h = (x * lax.rsqrt(ms + EPS) * g_ref[...]).astype(BF16)

    a_ref[0] = _dot(h, w_ref[:, C_A:C_Q])
    zq = _dot(h, w_ref[:, C_Q:C_KC]) * (HEAD_DIM ** -0.5)
    zero_hi = jnp.zeros((tm, HEAD_DIM), BF16)
    for hd in range(N_Q_HEADS):
        qh = zq[:, hd * HEAD_DIM:(hd + 1) * HEAD_DIM].astype(BF16)
        q_ref[0, hd] = jnp.concatenate([qh, zero_hi], axis=1)
    zkc = _dot(h, w_ref[:, C_KC:C_VC])
    zvc = _dot(h, w_ref[:, C_VC:C_KS])
    for half in range(KV_WIDTH // LANES):
        kcr_ref[0, half] = zkc[:, half * LANES:(half + 1) * LANES]
        vcr_ref[0, half] = zvc[:, half * LANES:(half + 1) * LANES]

    t = pl.program_id(1) * tm + lax.broadcasted_iota(jnp.int32, (tm, HEAD_DIM), 0)
    blk = lax.broadcasted_iota(jnp.int32, (tm, HEAD_DIM), 1)
    onehot = jnp.where(t // SEL_BLOCK == blk, 1.0, 0.0).astype(BF16)
    zks = _dot(h, w_ref[:, C_KS:C_VS])
    zvs = _dot(h, w_ref[:, C_VS:C_KW])
    zkw = _dot(h, w_ref[:, C_KW:C_VW])
    zvw = _dot(h, w_ref[:, C_VW:C_G])
    for g in range(N_KV_HEADS):
        sl = slice(g * HEAD_DIM, (g + 1) * HEAD_DIM)
        ksa_ref[0, g] = jnp.concatenate([zks[:, sl].astype(BF16), onehot], axis=1)
        vs_ref[0, g] = zvs[:, sl].astype(BF16)
        kw_ref[0, g] = zkw[:, sl].astype(BF16)
        vw_ref[0, g] = zvw[:, sl].astype(BF16)
    gate_ref[0] = _sigmoid(_dot(h, w_ref[:, C_G:IN_COLS_PAD]))


def _inproj(x, norm_g, w_pad):
    b, t, _ = x.shape
    tm = min(TM_PROJ, t)
    grid = (b, t // tm)
    tok = lambda bi, i: (bi, i, 0)
    head = lambda bi, i: (bi, 0, i, 0)
    out_shape = (
        jax.ShapeDtypeStruct((b, t, 2 * CONV_CH), F32),
        jax.ShapeDtypeStruct((b, N_Q_HEADS, t, 2 * HEAD_DIM), BF16),
        jax.ShapeDtypeStruct((b, KV_WIDTH // LANES, t, LANES), F32),
        jax.ShapeDtypeStruct((b, KV_WIDTH // LANES, t, LANES), F32),
        jax.ShapeDtypeStruct((b, N_KV_HEADS, t, 2 * HEAD_DIM), BF16),
        jax.ShapeDtypeStruct((b, N_KV_HEADS, t, HEAD_DIM), BF16),
        jax.ShapeDtypeStruct((b, N_KV_HEADS, t, HEAD_DIM), BF16),
        jax.ShapeDtypeStruct((b, N_KV_HEADS, t, HEAD_DIM), BF16),
        jax.ShapeDtypeStruct((b, t, GATE_PAD), F32),
    )
    out_specs = (
        pl.BlockSpec((1, tm, 2 * CONV_CH), tok),
        pl.BlockSpec((1, N_Q_HEADS, tm, 2 * HEAD_DIM), head),
        pl.BlockSpec((1, KV_WIDTH // LANES, tm, LANES), head),
        pl.BlockSpec((1, KV_WIDTH // LANES, tm, LANES), head),
        pl.BlockSpec((1, N_KV_HEADS, tm, 2 * HEAD_DIM), head),
        pl.BlockSpec((1, N_KV_HEADS, tm, HEAD_DIM), head),
        pl.BlockSpec((1, N_KV_HEADS, tm, HEAD_DIM), head),
        pl.BlockSpec((1, N_KV_HEADS, tm, HEAD_DIM), head),
        pl.BlockSpec((1, tm, GATE_PAD), tok),
    )
    return pl.pallas_call(
        _inproj_kernel,
        grid=grid,
        in_specs=[
            pl.BlockSpec((1, tm, D_MODEL), tok),
            _const_spec((1, D_MODEL)),
            _const_spec((D_MODEL, IN_COLS_PAD)),
        ],
        out_specs=out_specs,
        out_shape=out_shape,
        compiler_params=pltpu.CompilerParams(
            dimension_semantics=("arbitrary", "arbitrary"), vmem_limit_bytes=VMEM_LIMIT),
        name="inproj",
    )(x, norm_g, w_pad)


def _conv_kernel(a_ref, halo_ref, w_ref, b_ref, lg_ref, lb_ref, o_ref, y_ref):
    tc = a_ref.shape[1]
    am = a_ref[0]
    ah = halo_ref[0]
    ym = am[:, :CONV_CH] * _sigmoid(am[:, CONV_CH:])
    yh = ah[:, :CONV_CH] * _sigmoid(ah[:, CONV_CH:])
    yh = jnp.where(pl.program_id(1) == 0, 0.0, yh)
    y_ref[0:CONV_HALO, :] = yh
    y_ref[CONV_HALO:CONV_HALO + tc, :] = ym
    first = CONV_HALO - (CONV_WIDTH - 1)
    acc = jnp.zeros((tc, CONV_CH), F32)
    for k in range(CONV_WIDTH):
        acc = acc + w_ref[k:k + 1, :] * y_ref[pl.ds(first + k, tc), :]
    acc = acc + b_ref[...]
    mu = jnp.mean(acc, axis=-1, keepdims=True)
    d = acc - mu
    var = jnp.mean(d * d, axis=-1, keepdims=True)
    yn = d * lax.rsqrt(var + EPS) * lg_ref[...] + lb_ref[...]
    o_ref[0] = (yn * _sigmoid(yn)).astype(o_ref.dtype)


def _conv_mixer(a, w_dw, b_dw, ln_g, ln_b):
    b, t, _ = a.shape
    tc = min(TC_CONV, t)
    per = tc // CONV_HALO
    return pl.pallas_call(
        _conv_kernel,
        grid=(b, t // tc),
        in_specs=[
            pl.BlockSpec((1, tc, 2 * CONV_CH), lambda bi, i: (bi, i, 0)),
            pl.BlockSpec((1, CONV_HALO, 2 * CONV_CH),
                         lambda bi, i: (bi, jnp.maximum(i * per - 1, 0), 0)),
            _const_spec((CONV_HALO, CONV_CH)),
            _const_spec((1, CONV_CH)),
            _const_spec((1, CONV_CH)),
            _const_spec((1, CONV_CH)),
        ],
        out_specs=pl.BlockSpec((1, tc, CONV_CH), lambda bi, i: (bi, i, 0)),
        out_shape=jax.ShapeDtypeStruct((b, t, CONV_CH), BF16),
        scratch_shapes=[pltpu.VMEM((CONV_HALO + tc, CONV_CH), F32)],
        compiler_params=pltpu.CompilerParams(
            dimension_semantics=("arbitrary", "arbitrary"), vmem_limit_bytes=VMEM_LIMIT),
        name="conv_mixer",
    )(a, a, w_dw, b_dw, ln_g, ln_b)


def _compress_kernel(kr_ref, vr_ref, kpe_ref, kw1_ref, kw2_ref, vpe_ref, vw1_ref, vw2_ref,
                     kc_ref, vc_ref):
    nch = kr_ref.shape[2] // CMP_STRIDE
    row = lax.broadcasted_iota(jnp.int32, (nch, HEAD_DIM), 0)
    for r_ref, pe_ref, w1_ref, w2_ref, o_ref in (
            (kr_ref, kpe_ref, kw1_ref, kw2_ref, kc_ref),
            (vr_ref, vpe_ref, vw1_ref, vw2_ref, vc_ref)):
        pb = _dot(pe_ref[...], w1_ref[...])
        bias = pb[0:1, :CMP_HIDDEN] + pb[8:9, CMP_HIDDEN:]
        rows = [[r_ref[0, half, pl.ds(l, nch, stride=CMP_STRIDE), :] for l in range(CMP_STRIDE)]
                for half in range(KV_WIDTH // LANES)]
        for g in range(N_KV_HEADS):
            half, odd = divmod(g, LANES // HEAD_DIM)
            sl = slice(odd * HEAD_DIM, (odd + 1) * HEAD_DIM)
            xg = jnp.concatenate([rw[:, sl] for rw in rows[half]], axis=1).astype(BF16)
            lohi = _dot(xg, w1_ref[...])
            lo = lohi[:, :CMP_HIDDEN]
            hi = lohi[:, CMP_HIDDEN:]
            hi_next = jnp.concatenate([hi[1:], jnp.zeros((1, CMP_HIDDEN), F32)], axis=0)
            hid = lo + hi_next + bias
            act = (hid * _sigmoid(hid)).astype(BF16)
            out = _dot(act, w2_ref[...])
            out = jnp.where(row < nch - 1, out, 0.0)
            o_ref[0, g] = out.astype(o_ref.dtype)


def _compress(kc_r, vc_r, kpe, kw1, kw2, vpe, vw1, vw2):
    b, _, t, _ = kc_r.shape
    nch = t // CMP_STRIDE
    blk_len = CMP_STRIDE * HEAD_DIM
    out_sds = jax.ShapeDtypeStruct((b, N_KV_HEADS, nch, HEAD_DIM), BF16)
    out_spec = pl.BlockSpec((1, N_KV_HEADS, nch, HEAD_DIM), lambda bi: (bi, 0, 0, 0))
    raw_spec = pl.BlockSpec((1, KV_WIDTH // LANES, t, LANES), lambda bi: (bi, 0, 0, 0))
    w_specs = [_const_spec((16, blk_len)), _const_spec((blk_len, 2 * CMP_HIDDEN)),
               _const_spec((CMP_HIDDEN, HEAD_DIM))]
    return pl.pallas_call(
        _compress_kernel,
        grid=(b,),
        in_specs=[raw_spec, raw_spec] + w_specs + w_specs,
        out_specs=(out_spec, out_spec),
        out_shape=(out_sds, out_sds),
        compiler_params=pltpu.CompilerParams(
            dimension_semantics=("arbitrary",), vmem_limit_bytes=VMEM_LIMIT),
        name="compress",
    )(kc_r, vc_r, kpe, kw1, kw2, vpe, vw1, vw2)


def _nsa_kernel(slope_ref, q_ref, kc_ref, vc_ref, ksa_ref, vs_ref, kw_ref, vw_ref, gate_ref, ov_ref,
                o_ref, qa_ref, m_ref, l_ref, acc_ref):
    g = pl.program_id(1)
    i = pl.program_id(2)
    t0 = i * TQ
    ncp = kc_ref.shape[2]
    slopes = [slope_ref[g * Q_PER_KV + r] for r in range(Q_PER_KV)]
    rows = [slice(r * TQ, (r + 1) * TQ) for r in range(Q_PER_KV)]

    q_full = q_ref[0].reshape(Q_PER_KV * TQ, 2 * HEAD_DIM)
    q3 = q_full[:, :HEAD_DIM]

    s_c = _dot_nt(q3, kc_ref[0, 0])
    n_idx = lax.broadcasted_iota(jnp.int32, (TQ, ncp), 1)
    t_idx = t0 + lax.broadcasted_iota(jnp.int32, (TQ, ncp), 0)
    c_valid = n_idx * CMP_STRIDE + (CMP_LEN - 1) <= t_idx
    n_row = lax.broadcasted_iota(jnp.int32, (1, ncp), 1)
    c_pos = (n_row * CMP_STRIDE - t0).astype(F32) + 0.5 * (CMP_LEN - 1)
    vc = vc_ref[0, 0]
    o_c = []
    imp = jnp.zeros((TQ, LANES), F32)
    for r in range(Q_PER_KV):
        sr = jnp.where(c_valid, s_c[rows[r]] + slopes[r] * c_pos, -jnp.inf)
        m = jnp.max(sr, axis=1, keepdims=True)
        m = jnp.where(m == -jnp.inf, 0.0, m)
        p = jnp.exp(sr - m)
        p = p / jnp.maximum(jnp.sum(p, axis=1, keepdims=True), 1e-30)
        pb = p.astype(BF16)
        o_c.append(_dot(pb, vc))
        imp = imp + _dot(pb, ov_ref[...])

    j_idx = lax.broadcasted_iota(jnp.int32, (TQ, LANES), 1) - HEAD_DIM
    cur = (t0 + lax.broadcasted_iota(jnp.int32, (TQ, LANES), 0)) // SEL_BLOCK
    forced = (j_idx == 0) | (j_idx == cur) | (j_idx == cur - 1)
    cand = jnp.where((j_idx >= 0) & (j_idx <= cur),
                     imp + jnp.where(forced, FORCED_BONUS, 0.0), -jnp.inf)
    sel = jnp.zeros((TQ, LANES), jnp.bool_)
    for _ in range(SEL_TOPK):
        mx = jnp.max(cand, axis=1, keepdims=True)
        first = jnp.min(jnp.where(cand == mx, j_idx, LANES), axis=1, keepdims=True)
        hit = j_idx == first
        sel = sel | (hit & (mx > -jnp.inf))
        cand = jnp.where(hit, -jnp.inf, cand)
    lane = lax.broadcasted_iota(jnp.int32, (TQ, LANES), 1)
    sel_bias = jnp.where(sel | (lane < HEAD_DIM), 0.0, MASK_NEG).astype(BF16)
    for r in range(Q_PER_KV):
        qa_ref[rows[r], :] = q_full[rows[r]] + sel_bias

    sub_t = t0 + lax.broadcasted_iota(jnp.int32, (TQ, TQ), 0)
    lane_k = lax.broadcasted_iota(jnp.int32, (TQ, TQ), 1)
    lane_row = lax.broadcasted_iota(jnp.int32, (1, TQ), 1)

    def flash_update(s_all, valid, v, k0):
        k_pos = (k0 + lane_row - t0).astype(F32)
        for r in range(Q_PER_KV):
            sr = jnp.where(valid, s_all[rows[r]] + slopes[r] * k_pos, MASK_NEG)
            m_old = m_ref[rows[r], :]
            m_new = jnp.maximum(m_old, jnp.max(sr, axis=1, keepdims=True))
            alpha = jnp.exp(m_old - m_new)
            p = jnp.exp(sr - m_new)
            l_ref[rows[r], :] = alpha * l_ref[rows[r], :] + jnp.sum(p, axis=1, keepdims=True)
            acc_ref[rows[r], :] = alpha * acc_ref[rows[r], :] + _dot(p.astype(BF16), v)
            m_ref[rows[r], :] = m_new

    def flash_reset():
        m_ref[...] = jnp.full(m_ref.shape, M_INIT, F32)
        l_ref[...] = jnp.zeros(l_ref.shape, F32)
        acc_ref[...] = jnp.zeros(acc_ref.shape, F32)

    flash_reset()

    def sel_body(kt, carry):
        k0 = pl.multiple_of(kt * TQ, TQ)
        s_all = _dot_nt(qa_ref[...], ksa_ref[0, 0, pl.ds(k0, TQ), :])
        flash_update(s_all, k0 + lane_k <= sub_t, vs_ref[0, 0, pl.ds(k0, TQ), :], k0)
        return carry

    lax.fori_loop(0, i + 1, sel_body, 0)
    o_s = [acc_ref[rows[r], :] / l_ref[rows[r], :] for r in range(Q_PER_KV)]

    flash_reset()

    def win_body(kt, carry):
        k0 = pl.multiple_of(kt * TQ, TQ)
        s_all = _dot_nt(q3, kw_ref[0, 0, pl.ds(k0, TQ), :])
        dist = sub_t - (k0 + lane_k)
        flash_update(s_all, (dist >= 0) & (dist < WINDOW), vw_ref[0, 0, pl.ds(k0, TQ), :], k0)
        return carry

    lax.fori_loop(jnp.maximum(i - WINDOW // TQ, 0), i + 1, win_body, 0)

    gates = gate_ref[0]
    outs = []
    for r in range(Q_PER_KV):
        o_w = acc_ref[rows[r], :] / l_ref[rows[r], :]
        col = g * (Q_PER_KV * N_BRANCH) + r * N_BRANCH
        gc, gs, gw = (jnp.sum(jnp.where(lane == col + br, gates, 0.0), axis=1, keepdims=True)
                      for br in range(N_BRANCH))
        outs.append(gc * o_c[r] + gs * o_s[r] + gw * o_w)
    o_ref[0, 0] = jnp.concatenate(outs, axis=1).astype(o_ref.dtype)


def _nsa(q, kc, vc, ksa, vs, kw, vw, gates, overlap, slopes):
    b, _, t, _ = q.shape
    ncp = kc.shape[2]
    grp = lambda bi, g, i: (bi, g, 0, 0)
    return pl.pallas_call(
        _nsa_kernel,
        grid=(b, N_KV_HEADS, t // TQ),
        in_specs=[
            pl.BlockSpec(memory_space=pltpu.SMEM),
            pl.BlockSpec((1, Q_PER_KV, TQ, 2 * HEAD_DIM), lambda bi, g, i: (bi, g, i, 0)),
            pl.BlockSpec((1, 1, ncp, HEAD_DIM), grp),
            pl.BlockSpec((1, 1, ncp, HEAD_DIM), grp),
            pl.BlockSpec((1, 1, t, 2 * HEAD_DIM), grp),
            pl.BlockSpec((1, 1, t, HEAD_DIM), grp),
            pl.BlockSpec((1, 1, t, HEAD_DIM), grp),
            pl.BlockSpec((1, 1, t, HEAD_DIM), grp),
            pl.BlockSpec((1, TQ, GATE_PAD), lambda bi, g, i: (bi, i, 0)),
            _const_spec((ncp, LANES)),
        ],
        out_specs=pl.BlockSpec((1, 1, TQ, Q_PER_KV * HEAD_DIM), lambda bi, g, i: (bi, g, i, 0)),
        out_shape=jax.ShapeDtypeStruct((b, N_KV_HEADS, t, Q_PER_KV * HEAD_DIM), BF16),
        scratch_shapes=[
            pltpu.VMEM((Q_PER_KV * TQ, 2 * HEAD_DIM), BF16),
            pltpu.VMEM((Q_PER_KV * TQ, 1), F32),
            pltpu.VMEM((Q_PER_KV * TQ, 1), F32),
            pltpu.VMEM((Q_PER_KV * TQ, HEAD_DIM), F32),
        ],
        compiler_params=pltpu.CompilerParams(
            dimension_semantics=("arbitrary", "arbitrary", "arbitrary"),
            vmem_limit_bytes=VMEM_LIMIT),
        name="nsa_attention",
    )(slopes, q, kc, vc, ksa, vs, kw, vw, gates, overlap)


def _outffn_kernel(x_ref, c_ref, n_ref, woc_ref, won_ref, fg_ref, wg_ref, wu_ref, wd_ref, fin_ref,
                   o_ref, acc_ref, h_ref, *, final):
    x1 = x_ref[0] + _dot(c_ref[0], woc_ref[...])
    for g in range(N_KV_HEADS):
        x1 = x1 + _dot(n_ref[0, g], won_ref[g])
    ms = jnp.mean(x1 * x1, axis=-1, keepdims=True)
    h_ref[...] = (x1 * lax.rsqrt(ms + EPS) * fg_ref[...]).astype(BF16)
    acc_ref[...] = x1

    def body(c, carry):
        h = h_ref[...]
        gate = _dot(h, wg_ref[c])
        up = _dot(h, wu_ref[c])
        act = (gate * _sigmoid(gate) * up).astype(BF16)
        acc_ref[...] += _dot(act, wd_ref[c])
        return carry

    lax.fori_loop(0, N_FF_CHUNKS, body, 0)
    y = acc_ref[...]
    if final:
        ms = jnp.mean(y * y, axis=-1, keepdims=True)
        y = y * lax.rsqrt(ms + EPS) * fin_ref[...]
    o_ref[0] = y


def _outffn(x, conv_out, nsa_out, wo_c, wo_n, ffn_g, wg, wu, wd, final_g, final):
    b, t, _ = x.shape
    tm = min(TM_PROJ, t)
    tok = lambda bi, i: (bi, i, 0)
    return pl.pallas_call(
        functools.partial(_outffn_kernel, final=final),
        grid=(b, t // tm),
        in_specs=[
            pl.BlockSpec((1, tm, D_MODEL), tok),
            pl.BlockSpec((1, tm, CONV_CH), tok),
            pl.BlockSpec((1, N_KV_HEADS, tm, Q_PER_KV * HEAD_DIM), lambda bi, i: (bi, 0, i, 0)),
            _const_spec((CONV_CH, D_MODEL)),
            _const_spec((N_KV_HEADS, Q_PER_KV * HEAD_DIM, D_MODEL)),
            _const_spec((1, D_MODEL)),
            _const_spec((N_FF_CHUNKS, D_MODEL, FF_CHUNK)),
            _const_spec((N_FF_CHUNKS, D_MODEL, FF_CHUNK)),
            _const_spec((N_FF_CHUNKS, FF_CHUNK, D_MODEL)),
            _const_spec((1, D_MODEL)),
        ],
        out_specs=pl.BlockSpec((1, tm, D_MODEL), tok),
        out_shape=jax.ShapeDtypeStruct((b, t, D_MODEL), F32),
        scratch_shapes=[pltpu.VMEM((tm, D_MODEL), F32), pltpu.VMEM((tm, D_MODEL), BF16)],
        compiler_params=pltpu.CompilerParams(
            dimension_semantics=("arbitrary", "arbitrary"), vmem_limit_bytes=VMEM_LIMIT),
        name="outproj_ffn",
    )(x, conv_out, nsa_out, wo_c, wo_n, ffn_g, wg, wu, wd, final_g)


def _overlap_matrix(ncp):
    i = np.arange(ncp)[:, None] * CMP_STRIDE
    j = np.arange(LANES - HEAD_DIM)[None, :] * SEL_BLOCK
    ov = (i < j + SEL_BLOCK) & (i + CMP_LEN > j)
    out = np.zeros((ncp, LANES), np.float32)
    out[:, HEAD_DIM:] = ov
    return jnp.asarray(out, BF16)


def _pe_rows(pe):
    lo = pe[:CMP_STRIDE].reshape(1, -1)
    hi = pe[CMP_STRIDE:].reshape(1, -1)
    z = jnp.zeros((7, lo.shape[1]), pe.dtype)
    return jnp.concatenate([lo, z, hi, z], axis=0).astype(BF16)


def _w1_cat(w1):
    lo = w1[:CMP_STRIDE].reshape(CMP_STRIDE * HEAD_DIM, CMP_HIDDEN)
    hi = w1[CMP_STRIDE:].reshape(CMP_STRIDE * HEAD_DIM, CMP_HIDDEN)
    return jnp.concatenate([lo, hi], axis=1).astype(BF16)


def kernel(x, attn_norm, w_in, conv_w, conv_b, conv_ln_g, conv_ln_b, cmp_k_pe, cmp_k_w1, cmp_k_w2,
           cmp_v_pe, cmp_v_w1, cmp_v_w2, w_out, ffn_norm, w_gate_up, w_down, final_norm):
    depth = w_in.shape[0]
    t = x.shape[1]
    assert t % TQ == 0 and t // SEL_BLOCK <= LANES - HEAD_DIM and t >= WINDOW
    overlap = _overlap_matrix(t // CMP_STRIDE)
    slopes = jnp.asarray(_alibi_slopes(N_Q_HEADS))
    final_g = final_norm.reshape(1, D_MODEL)
    for l in range(depth):
        w_pad = jnp.pad(w_in[l], ((0, 0), (0, IN_COLS_PAD - w_in.shape[2]))).astype(BF16)
        a, q, kc_r, vc_r, ksa, vs, kw, vw, gates = _inproj(x, attn_norm[l].reshape(1, D_MODEL), w_pad)
        conv_out = _conv_mixer(
            a, jnp.pad(conv_w[l], ((0, CONV_HALO - CONV_WIDTH), (0, 0))),
            conv_b[l].reshape(1, CONV_CH), conv_ln_g[l].reshape(1, CONV_CH),
            conv_ln_b[l].reshape(1, CONV_CH))
        kc, vc = _compress(kc_r, vc_r,
                           _pe_rows(cmp_k_pe[l]), _w1_cat(cmp_k_w1[l]), cmp_k_w2[l].astype(BF16),
                           _pe_rows(cmp_v_pe[l]), _w1_cat(cmp_v_w1[l]), cmp_v_w2[l].astype(BF16))
        nsa_out = _nsa(q, kc, vc, ksa, vs, kw, vw, gates, overlap, slopes)
        wo = w_out[l].astype(BF16)
        wo_n = wo[CONV_CH:].reshape(N_KV_HEADS, Q_PER_KV * HEAD_DIM, D_MODEL)
        wgu = w_gate_up[l].astype(BF16)
        wg = wgu[:, :D_FF].reshape(D_MODEL, N_FF_CHUNKS, FF_CHUNK).transpose(1, 0, 2)
        wu = wgu[:, D_FF:].reshape(D_MODEL, N_FF_CHUNKS, FF_CHUNK).transpose(1, 0, 2)
        wd = w_down[l].astype(BF16).reshape(N_FF_CHUNKS, FF_CHUNK, D_MODEL)
        x = _outffn(x, conv_out, nsa_out, wo[:CONV_CH], wo_n, ffn_norm[l].reshape(1, D_MODEL),
                    wg, wu, wd, final_g, final=(l == depth - 1))
    return x
```

```python
import functools
import math

import jax
import jax.numpy as jnp
import numpy as np
from jax import lax
from jax.experimental import pallas as pl
from jax.experimental.pallas import tpu as pltpu

F32 = jnp.float32
BF16 = jnp.bfloat16

D_MODEL = 1024
HEAD_DIM = 64
CONV_CH = 256
CONV_WIDTH = 31
N_Q_HEADS = 12
N_KV_HEADS = 4
Q_PER_KV = N_Q_HEADS // N_KV_HEADS
NSA_WIDTH = N_Q_HEADS * HEAD_DIM
KV_WIDTH = N_KV_HEADS * HEAD_DIM
CMP_LEN = 32
CMP_STRIDE = 16
CMP_HIDDEN = 256
SEL_BLOCK = 64
SEL_TOPK = 8
WINDOW = 512
N_BRANCH = 3
FORCED_BONUS = 1000.0
D_FF = 2816
EPS = 1e-6
LOG2E = math.log2(math.e)

LANES = 128
GATE_PAD = LANES
FF_CHUNK = 256
N_FF_CHUNKS = D_FF // FF_CHUNK
TM_PROJ = 512
TC_CONV = 512
CONV_HALO = 32
TQ = 256
WIN_KEYS = WINDOW + TQ
MASK_NEG = -(2.0 ** 100)
M_INIT = -1e30
VMEM_LIMIT = 56 * 1024 * 1024


def _alibi_slopes(n):
    def pow2_slopes(m):
        start = 2.0 ** (-8.0 / m)
        return [start ** (i + 1) for i in range(m)]
    if math.log2(n).is_integer():
        s = pow2_slopes(n)
    else:
        c = 2 ** math.floor(math.log2(n))
        s = pow2_slopes(c) + pow2_slopes(2 * c)[0::2][: n - c]
    return np.asarray(s, dtype=np.float32)


def _sigmoid(v):
    return 1.0 / (1.0 + jnp.exp(-v))


def _dot(a, b):
    return jnp.dot(a, b, preferred_element_type=F32)


def _dot_nt(a, b):
    return lax.dot_general(a, b, (((1,), (1,)), ((), ())), preferred_element_type=F32)


def _const_spec(shape):
    nd = len(shape)
    return pl.BlockSpec(shape, lambda *_: (0,) * nd, pipeline_mode=pl.Buffered(1))


C_A = 0
C_Q = C_A + 2 * CONV_CH
C_KC = C_Q + NSA_WIDTH
C_VC = C_KC + KV_WIDTH
C_KS = C_VC + KV_WIDTH
C_VS = C_KS + KV_WIDTH
C_KW = C_VS + KV_WIDTH
C_VW = C_KW + KV_WIDTH
C_G = C_VW + KV_WIDTH
IN_COLS_PAD = C_G + GATE_PAD


def _inproj_kernel(x_ref, g_ref, w_ref, a_ref, q_ref, kcr_ref, vcr_ref, ksa_ref, vsa_ref, kw_ref,
                   vwa_ref, gate_ref):
    tm = x_ref.shape[1]
    x = x_ref[0]
    ms = jnp.mean(x * x, axis=-1, keepdims=True)
    h = (x * lax.rsqrt(ms + EPS) * g_ref[...]).astype(BF16)

    a_ref[0] = _dot(h, w_ref[:, C_A:C_Q])
    zq = _dot(h, w_ref[:, C_Q:C_KC]) * (HEAD_DIM ** -0.5 * LOG2E)
    zero_hi = jnp.zeros((tm, HEAD_DIM), BF16)
    for hd in range(N_Q_HEADS):
        qh = zq[:, hd * HEAD_DIM:(hd + 1) * HEAD_DIM].astype(BF16)
        q_ref[0, hd] = jnp.concatenate([qh, zero_hi], axis=1)
    zkc = _dot(h, w_ref[:, C_KC:C_VC])
    zvc = _dot(h, w_ref[:, C_VC:C_KS])
    for half in range(KV_WIDTH // LANES):
        kcr_ref[0, half] = zkc[:, half * LANES:(half + 1) * LANES]
        vcr_ref[0, half] = zvc[:, half * LANES:(half + 1) * LANES]

    t = pl.program_id(1) * tm + lax.broadcasted_iota(jnp.int32, (tm, HEAD_DIM), 0)
    blk = lax.broadcasted_iota(jnp.int32, (tm, HEAD_DIM), 1)
    onehot = jnp.where(t // SEL_BLOCK == blk, 1.0, 0.0).astype(BF16)
    ones_col = jnp.where(blk == 0, 1.0, 0.0).astype(BF16)
    zks = _dot(h, w_ref[:, C_KS:C_VS])
    zvs = _dot(h, w_ref[:, C_VS:C_KW])
    zkw = _dot(h, w_ref[:, C_KW:C_VW])
    zvw = _dot(h, w_ref[:, C_VW:C_G])
    for g in range(N_KV_HEADS):
        sl = slice(g * HEAD_DIM, (g + 1) * HEAD_DIM)
        ksa_ref[0, g] = jnp.concatenate([zks[:, sl].astype(BF16), onehot], axis=1)
        vsa_ref[0, g] = jnp.concatenate([zvs[:, sl].astype(BF16), ones_col], axis=1)
        kw_ref[0, g] = zkw[:, sl].astype(BF16)
        vwa_ref[0, g] = jnp.concatenate([zvw[:, sl].astype(BF16), ones_col], axis=1)
    gate_ref[0] = _sigmoid(_dot(h, w_ref[:, C_G:IN_COLS_PAD]))


def _inproj(x, norm_g, w_pad):
    b, t, _ = x.shape
    tm = min(TM_PROJ, t)
    grid = (b, t // tm)
    tok = lambda bi, i: (bi, i, 0)
    head = lambda bi, i: (bi, 0, i, 0)
    out_shape = (
        jax.ShapeDtypeStruct((b, t, 2 * CONV_CH), F32),
        jax.ShapeDtypeStruct((b, N_Q_HEADS, t, 2 * HEAD_DIM), BF16),
        jax.ShapeDtypeStruct((b, KV_WIDTH // LANES, t, LANES), F32),
        jax.ShapeDtypeStruct((b, KV_WIDTH // LANES, t, LANES), F32),
        jax.ShapeDtypeStruct((b, N_KV_HEADS, t, 2 * HEAD_DIM), BF16),
        jax.ShapeDtypeStruct((b, N_KV_HEADS, t, 2 * HEAD_DIM), BF16),
        jax.ShapeDtypeStruct((b, N_KV_HEADS, t, HEAD_DIM), BF16),
        jax.ShapeDtypeStruct((b, N_KV_HEADS, t, 2 * HEAD_DIM), BF16),
        jax.ShapeDtypeStruct((b, t, GATE_PAD), F32),
    )
    out_specs = (
        pl.BlockSpec((1, tm, 2 * CONV_CH), tok),
        pl.BlockSpec((1, N_Q_HEADS, tm, 2 * HEAD_DIM), head),
        pl.BlockSpec((1, KV_WIDTH // LANES, tm, LANES), head),
        pl.BlockSpec((1, KV_WIDTH // LANES, tm, LANES), head),
        pl.BlockSpec((1, N_KV_HEADS, tm, 2 * HEAD_DIM), head),
        pl.BlockSpec((1, N_KV_HEADS, tm, 2 * HEAD_DIM), head),
        pl.BlockSpec((1, N_KV_HEADS, tm, HEAD_DIM), head),
        pl.BlockSpec((1, N_KV_HEADS, tm, 2 * HEAD_DIM), head),
        pl.BlockSpec((1, tm, GATE_PAD), tok),
    )
    return pl.pallas_call(
        _inproj_kernel,
        grid=grid,
        in_specs=[
            pl.BlockSpec((1, tm, D_MODEL), tok),
            _const_spec((1, D_MODEL)),
            _const_spec((D_MODEL, IN_COLS_PAD)),
        ],
        out_specs=out_specs,
        out_shape=out_shape,
        compiler_params=pltpu.CompilerParams(
            dimension_semantics=("arbitrary", "arbitrary"), vmem_limit_bytes=VMEM_LIMIT),
        name="inproj",
    )(x, norm_g, w_pad)


def _conv_kernel(a_ref, halo_ref, w_ref, b_ref, lg_ref, lb_ref, o_ref, y_ref):
    tc = a_ref.shape[1]
    am = a_ref[0]
    ah = halo_ref[0]
    ym = am[:, :CONV_CH] * _sigmoid(am[:, CONV_CH:])
    yh = ah[:, :CONV_CH] * _sigmoid(ah[:, CONV_CH:])
    yh = jnp.where(pl.program_id(1) == 0, 0.0, yh)
    y_ref[0:CONV_HALO, :] = yh
    y_ref[CONV_HALO:CONV_HALO + tc, :] = ym
    first = CONV_HALO - (CONV_WIDTH - 1)
    acc = jnp.zeros((tc, CONV_CH), F32)
    for k in range(CONV_WIDTH):
        acc = acc + w_ref[k:k + 1, :] * y_ref[pl.ds(first + k, tc), :]
    acc = acc + b_ref[...]
    mu = jnp.mean(acc, axis=-1, keepdims=True)
    d = acc - mu
    var = jnp.mean(d * d, axis=-1, keepdims=True)
    yn = d * lax.rsqrt(var + EPS) * lg_ref[...] + lb_ref[...]
    o_ref[0] = (yn * _sigmoid(yn)).astype(o_ref.dtype)


def _conv_mixer(a, w_dw, b_dw, ln_g, ln_b):
    b, t, _ = a.shape
    tc = min(TC_CONV, t)
    per = tc // CONV_HALO
    return pl.pallas_call(
        _conv_kernel,
        grid=(b, t // tc),
        in_specs=[
            pl.BlockSpec((1, tc, 2 * CONV_CH), lambda bi, i: (bi, i, 0)),
            pl.BlockSpec((1, CONV_HALO, 2 * CONV_CH),
                         lambda bi, i: (bi, jnp.maximum(i * per - 1, 0), 0)),
            _const_spec((CONV_HALO, CONV_CH)),
            _const_spec((1, CONV_CH)),
            _const_spec((1, CONV_CH)),
            _const_spec((1, CONV_CH)),
        ],
        out_specs=pl.BlockSpec((1, tc, CONV_CH), lambda bi, i: (bi, i, 0)),
        out_shape=jax.ShapeDtypeStruct((b, t, CONV_CH), BF16),
        scratch_shapes=[pltpu.VMEM((CONV_HALO + tc, CONV_CH), F32)],
        compiler_params=pltpu.CompilerParams(
            dimension_semantics=("arbitrary", "arbitrary"), vmem_limit_bytes=VMEM_LIMIT),
        name="conv_mixer",
    )(a, a, w_dw, b_dw, ln_g, ln_b)


def _compress_kernel(kr_ref, vr_ref, kpe_ref, kw1_ref, kw2_ref, vpe_ref, vw1_ref, vw2_ref,
                     kc_ref, vc_ref):
    nch = kr_ref.shape[2] // CMP_STRIDE
    row = lax.broadcasted_iota(jnp.int32, (nch, HEAD_DIM), 0)
    for r_ref, pe_ref, w1_ref, w2_ref, o_ref in (
            (kr_ref, kpe_ref, kw1_ref, kw2_ref, kc_ref),
            (vr_ref, vpe_ref, vw1_ref, vw2_ref, vc_ref)):
        pb = _dot(pe_ref[...], w1_ref[...])
        bias = pb[0:1, :CMP_HIDDEN] + pb[8:9, CMP_HIDDEN:]
        rows = [[r_ref[0, half, pl.ds(l, nch, stride=CMP_STRIDE), :] for l in range(CMP_STRIDE)]
                for half in range(KV_WIDTH // LANES)]
        for g in range(N_KV_HEADS):
            half, odd = divmod(g, LANES // HEAD_DIM)
            sl = slice(odd * HEAD_DIM, (odd + 1) * HEAD_DIM)
            xg = jnp.concatenate([rw[:, sl] for rw in rows[half]], axis=1).astype(BF16)
            lohi = _dot(xg, w1_ref[...])
            lo = lohi[:, :CMP_HIDDEN]
            hi = lohi[:, CMP_HIDDEN:]
            hi_next = jnp.concatenate([hi[1:], jnp.zeros((1, CMP_HIDDEN), F32)], axis=0)
            hid = lo + hi_next + bias
            act = (hid * _sigmoid(hid)).astype(BF16)
            out = _dot(act, w2_ref[...])
            out = jnp.where(row < nch - 1, out, 0.0)
            o_ref[0, g] = out.astype(o_ref.dtype)


def _compress(kc_r, vc_r, kpe, kw1, kw2, vpe, vw1, vw2):
    b, _, t, _ = kc_r.shape
    nch = t // CMP_STRIDE
    blk_len = CMP_STRIDE * HEAD_DIM
    out_sds = jax.ShapeDtypeStruct((b, N_KV_HEADS, nch, HEAD_DIM), BF16)
    out_spec = pl.BlockSpec((1, N_KV_HEADS, nch, HEAD_DIM), lambda bi: (bi, 0, 0, 0))
    raw_spec = pl.BlockSpec((1, KV_WIDTH // LANES, t, LANES), lambda bi: (bi, 0, 0, 0))
    w_specs = [_const_spec((16, blk_len)), _const_spec((blk_len, 2 * CMP_HIDDEN)),
               _const_spec((CMP_HIDDEN, HEAD_DIM))]
    return pl.pallas_call(
        _compress_kernel,
        grid=(b,),
        in_specs=[raw_spec, raw_spec] + w_specs + w_specs,
        out_specs=(out_spec, out_spec),
        out_shape=(out_sds, out_sds),
        compiler_params=pltpu.CompilerParams(
            dimension_semantics=("arbitrary",), vmem_limit_bytes=VMEM_LIMIT),
        name="compress",
    )(kc_r, vc_r, kpe, kw1, kw2, vpe, vw1, vw2)


def _nsa_kernel(slope_ref, q_ref, kc_ref, vc_ref, ksa_ref, vsa_ref, kw_ref, vwa_ref, gate_ref, ov_ref,
                o_ref, qa_ref, s_ref, mx_ref, p_ref, acc_ref, pw_ref):
    g = pl.program_id(1)
    i = pl.program_id(2)
    t0 = i * TQ
    ncp = kc_ref.shape[2]
    slopes = [slope_ref[g * Q_PER_KV + r] for r in range(Q_PER_KV)]
    rows = [slice(r * TQ, (r + 1) * TQ) for r in range(Q_PER_KV)]
    halves = [slice(hf * LANES, (hf + 1) * LANES) for hf in range(TQ // LANES)]

    q_full = q_ref[0].reshape(Q_PER_KV * TQ, 2 * HEAD_DIM)
    q3 = q_full[:, :HEAD_DIM]

    s_c = _dot_nt(q3, kc_ref[0, 0])
    n_idx = lax.broadcasted_iota(jnp.int32, (TQ, ncp), 1)
    t_idx = t0 + lax.broadcasted_iota(jnp.int32, (TQ, ncp), 0)
    c_valid = n_idx * CMP_STRIDE + (CMP_LEN - 1) <= t_idx
    n_row = lax.broadcasted_iota(jnp.int32, (1, ncp), 1)
    c_pos = (n_row * CMP_STRIDE - t0).astype(F32) + 0.5 * (CMP_LEN - 1)
    vc = vc_ref[0, 0]
    o_c = []
    imp = jnp.zeros((TQ, LANES), F32)
    for r in range(Q_PER_KV):
        sr = jnp.where(c_valid, s_c[rows[r]] + slopes[r] * c_pos, -jnp.inf)
        m = jnp.max(sr, axis=1, keepdims=True)
        m = jnp.where(m == -jnp.inf, 0.0, m)
        p = jnp.exp2(sr - m)
        p = p / jnp.maximum(jnp.sum(p, axis=1, keepdims=True), 1e-30)
        pb = p.astype(BF16)
        o_c.append(_dot(pb, vc))
        imp = imp + _dot(pb, ov_ref[...])

    imp_t = imp.T[HEAD_DIM:, :]
    j_t = lax.broadcasted_iota(jnp.int32, (LANES - HEAD_DIM, TQ), 0)
    cur_t = (t0 + lax.broadcasted_iota(jnp.int32, (LANES - HEAD_DIM, TQ), 1)) // SEL_BLOCK
    forced = (j_t == 0) | (j_t == cur_t) | (j_t == cur_t - 1)
    cand = jnp.where(j_t <= cur_t, imp_t + jnp.where(forced, FORCED_BONUS, 0.0), -jnp.inf)
    sel = jnp.zeros(cand.shape, jnp.bool_)
    for _ in range(SEL_TOPK):
        mx = jnp.max(cand, axis=0, keepdims=True)
        first = jnp.min(jnp.where(cand == mx, j_t, LANES), axis=0, keepdims=True)
        hit = j_t == first
        sel = sel | (hit & (mx > -jnp.inf))
        cand = jnp.where(hit, -jnp.inf, cand)
    bias_t = jnp.concatenate([jnp.zeros(cand.shape, F32), jnp.where(sel, 0.0, MASK_NEG)], axis=0)
    sel_bias = bias_t.T.astype(BF16)
    for r in range(Q_PER_KV):
        qa_ref[rows[r], :] = q_full[rows[r]] + sel_bias

    lane_row = lax.broadcasted_iota(jnp.int32, (1, TQ), 1)
    mx_ref[...] = jnp.full(mx_ref.shape, M_INIT, F32)

    def score_chunk(c, diagonal):
        k0 = pl.multiple_of(c * TQ, TQ)
        s_all = _dot_nt(qa_ref[...], ksa_ref[0, 0, pl.ds(k0, TQ), :])
        k_pos = (k0 + lane_row - t0).astype(F32)
        if diagonal:
            causal = (lax.broadcasted_iota(jnp.int32, (TQ, TQ), 1)
                      <= lax.broadcasted_iota(jnp.int32, (TQ, TQ), 0))
        for r in range(Q_PER_KV):
            sr = s_all[rows[r]] + slopes[r] * k_pos
            if diagonal:
                sr = jnp.where(causal, sr, MASK_NEG)
            s_ref[c, rows[r], :] = sr
            part = sr[:, halves[0]]
            for hf in halves[1:]:
                part = jnp.maximum(part, sr[:, hf])
            mx_ref[rows[r], :] = jnp.maximum(mx_ref[rows[r], :], part)

    def pass1(c, carry):
        score_chunk(c, False)
        return carry

    lax.fori_loop(0, i, pass1, 0)
    score_chunk(i, True)

    for r in range(Q_PER_KV):
        m = jnp.max(mx_ref[rows[r], :], axis=1, keepdims=True)
        mx_ref[rows[r], :] = jnp.broadcast_to(m, (TQ, LANES))
    acc_ref[...] = jnp.zeros(acc_ref.shape, F32)

    def pass2(c, carry):
        k0 = pl.multiple_of(c * TQ, TQ)
        for r in range(Q_PER_KV):
            mb = mx_ref[rows[r], :]
            for hf in halves:
                p_ref[rows[r], hf] = jnp.exp2(s_ref[c, rows[r], hf] - mb).astype(BF16)
        acc_ref[...] += _dot(p_ref[...], vsa_ref[0, 0, pl.ds(k0, TQ), :])
        return carry

    lax.fori_loop(0, i + 1, pass2, 0)
    o_s = []
    for r in range(Q_PER_KV):
        a = acc_ref[rows[r], :]
        o_s.append(a[:, :HEAD_DIM] / a[:, HEAD_DIM:HEAD_DIM + 1])

    w0 = pl.multiple_of(jnp.maximum(t0 - WINDOW, 0), TQ)
    s_w = _dot_nt(q3, kw_ref[0, 0, pl.ds(w0, WIN_KEYS), :])
    wk_row = w0 - t0 + lax.broadcasted_iota(jnp.int32, (1, WIN_KEYS), 1)
    dist = (lax.broadcasted_iota(jnp.int32, (TQ, WIN_KEYS), 0)
            - (w0 - t0) - lax.broadcasted_iota(jnp.int32, (TQ, WIN_KEYS), 1))
    w_valid = (dist >= 0) & (dist < WINDOW)
    wk_pos = wk_row.astype(F32)
    for r in range(Q_PER_KV):
        sr = jnp.where(w_valid, s_w[rows[r]] + slopes[r] * wk_pos, MASK_NEG)
        m = jnp.max(sr, axis=1, keepdims=True)
        pw_ref[rows[r], :] = jnp.exp2(sr - m).astype(BF16)
    acc_w = _dot(pw_ref[...], vwa_ref[0, 0, pl.ds(w0, WIN_KEYS), :])

    gates = gate_ref[0]
    lane = lax.broadcasted_iota(jnp.int32, (TQ, LANES), 1)
    outs = []
    for r in range(Q_PER_KV):
        a = acc_w[rows[r]]
        o_w = a[:, :HEAD_DIM] / a[:, HEAD_DIM:HEAD_DIM + 1]
        col = g * (Q_PER_KV * N_BRANCH) + r * N_BRANCH
        gc, gs, gw = (jnp.sum(jnp.where(lane == col + br, gates, 0.0), axis=1, keepdims=True)
                      for br in range(N_BRANCH))
        outs.append(gc * o_c[r] + gs * o_s[r] + gw * o_w)
    o_ref[0, 0] = jnp.concatenate(outs, axis=1).astype(o_ref.dtype)


def _nsa(q, kc, vc, ksa, vsa, kw, vwa, gates, overlap, slopes):
    b, _, t, _ = q.shape
    ncp = kc.shape[2]
    grp = lambda bi, g, i: (bi, g, 0, 0)
    return pl.pallas_call(
        _nsa_kernel,
        grid=(b, N_KV_HEADS, t // TQ),
        in_specs=[
            pl.BlockSpec(memory_space=pltpu.SMEM),
            pl.BlockSpec((1, Q_PER_KV, TQ, 2 * HEAD_DIM), lambda bi, g, i: (bi, g, i, 0)),
            pl.BlockSpec((1, 1, ncp, HEAD_DIM), grp),
            pl.BlockSpec((1, 1, ncp, HEAD_DIM), grp),
            pl.BlockSpec((1, 1, t, 2 * HEAD_DIM), grp),
            pl.BlockSpec((1, 1, t, 2 * HEAD_DIM), grp),
            pl.BlockSpec((1, 1, t, HEAD_DIM), grp),
            pl.BlockSpec((1, 1, t, 2 * HEAD_DIM), grp),
            pl.BlockSpec((1, TQ, GATE_PAD), lambda bi, g, i: (bi, i, 0)),
            _const_spec((ncp, LANES)),
        ],
        out_specs=pl.BlockSpec((1, 1, TQ, Q_PER_KV * HEAD_DIM), lambda bi, g, i: (bi, g, i, 0)),
        out_shape=jax.ShapeDtypeStruct((b, N_KV_HEADS, t, Q_PER_KV * HEAD_DIM), BF16),
        scratch_shapes=[
            pltpu.VMEM((Q_PER_KV * TQ, 2 * HEAD_DIM), BF16),
            pltpu.VMEM((t // TQ, Q_PER_KV * TQ, TQ), F32),
            pltpu.VMEM((Q_PER_KV * TQ, LANES), F32),
            pltpu.VMEM((Q_PER_KV * TQ, TQ), BF16),
            pltpu.VMEM((Q_PER_KV * TQ, LANES), F32),
            pltpu.VMEM((Q_PER_KV * TQ, WIN_KEYS), BF16),
        ],
        compiler_params=pltpu.CompilerParams(
            dimension_semantics=("arbitrary", "arbitrary", "arbitrary"),
            vmem_limit_bytes=VMEM_LIMIT),
        name="nsa_attention",
    )(slopes, q, kc, vc, ksa, vsa, kw, vwa, gates, overlap)


def _outffn_kernel(x_ref, c_ref, n_ref, woc_ref, won_ref, fg_ref, wg_ref, wu_ref, wd_ref, fin_ref,
                   o_ref, acc_ref, h_ref, *, final):
    x1 = x_ref[0] + _dot(c_ref[0], woc_ref[...])
    for g in range(N_KV_HEADS):
        x1 = x1 + _dot(n_ref[0, g], won_ref[g])
    ms = jnp.mean(x1 * x1, axis=-1, keepdims=True)
    h_ref[...] = (x1 * lax.rsqrt(ms + EPS) * fg_ref[...]).astype(BF16)
    acc_ref[...] = x1

    def body(c, carry):
        h = h_ref[...]
        gate = _dot(h, wg_ref[c])
        up = _dot(h, wu_ref[c])
        act = (gate * _sigmoid(gate) * up).astype(BF16)
        acc_ref[...] += _dot(act, wd_ref[c])
        return carry

    lax.fori_loop(0, N_FF_CHUNKS, body, 0)
    y = acc_ref[...]
    if final:
        ms = jnp.mean(y * y, axis=-1, keepdims=True)
        y = y * lax.rsqrt(ms + EPS) * fin_ref[...]
    o_ref[0] = y


def _outffn(x, conv_out, nsa_out, wo_c, wo_n, ffn_g, wg, wu, wd, final_g, final):
    b, t, _ = x.shape
    tm = min(TM_PROJ, t)
    tok = lambda bi, i: (bi, i, 0)
    return pl.pallas_call(
        functools.partial(_outffn_kernel, final=final),
        grid=(b, t // tm),
        in_specs=[
            pl.BlockSpec((1, tm, D_MODEL), tok),
            pl.BlockSpec((1, tm, CONV_CH), tok),
            pl.BlockSpec((1, N_KV_HEADS, tm, Q_PER_KV * HEAD_DIM), lambda bi, i: (bi, 0, i, 0)),
            _const_spec((CONV_CH, D_MODEL)),
            _const_spec((N_KV_HEADS, Q_PER_KV * HEAD_DIM, D_MODEL)),
            _const_spec((1, D_MODEL)),
            _const_spec((N_FF_CHUNKS, D_MODEL, FF_CHUNK)),
            _const_spec((N_FF_CHUNKS, D_MODEL, FF_CHUNK)),
            _const_spec((N_FF_CHUNKS, FF_CHUNK, D_MODEL)),
            _const_spec((1, D_MODEL)),
        ],
        out_specs=pl.BlockSpec((1, tm, D_MODEL), tok),
        out_shape=jax.ShapeDtypeStruct((b, t, D_MODEL), F32),
        scratch_shapes=[pltpu.VMEM((tm, D_MODEL), F32), pltpu.VMEM((tm, D_MODEL), BF16)],
        compiler_params=pltpu.CompilerParams(
            dimension_semantics=("arbitrary", "arbitrary"), vmem_limit_bytes=VMEM_LIMIT),
        name="outproj_ffn",
    )(x, conv_out, nsa_out, wo_c, wo_n, ffn_g, wg, wu, wd, final_g)


def _overlap_matrix(ncp):
    i = np.arange(ncp)[:, None] * CMP_STRIDE
    j = np.arange(LANES - HEAD_DIM)[None, :] * SEL_BLOCK
    ov = (i < j + SEL_BLOCK) & (i + CMP_LEN > j)
    out = np.zeros((ncp, LANES), np.float32)
    out[:, HEAD_DIM:] = ov
    return jnp.asarray(out, BF16)


def _pe_rows(pe):
    lo = pe[:CMP_STRIDE].reshape(1, -1)
    hi = pe[CMP_STRIDE:].reshape(1, -1)
    z = jnp.zeros((7, lo.shape[1]), pe.dtype)
    return jnp.concatenate([lo, z, hi, z], axis=0).astype(BF16)


def _w1_cat(w1):
    lo = w1[:CMP_STRIDE].reshape(CMP_STRIDE * HEAD_DIM, CMP_HIDDEN)
    hi = w1[CMP_STRIDE:].reshape(CMP_STRIDE * HEAD_DIM, CMP_HIDDEN)
    return jnp.concatenate([lo, hi], axis=1).astype(BF16)


def kernel(x, attn_norm, w_in, conv_w, conv_b, conv_ln_g, conv_ln_b, cmp_k_pe, cmp_k_w1, cmp_k_w2,
           cmp_v_pe, cmp_v_w1, cmp_v_w2, w_out, ffn_norm, w_gate_up, w_down, final_norm):
    depth = w_in.shape[0]
    t = x.shape[1]
    assert t % TQ == 0 and t // SEL_BLOCK <= LANES - HEAD_DIM and t >= WIN_KEYS
    overlap = _overlap_matrix(t // CMP_STRIDE)
    slopes = jnp.asarray(_alibi_slopes(N_Q_HEADS) * np.float32(LOG2E))
    final_g = final_norm.reshape(1, D_MODEL)
    for l in range(depth):
        w_pad = jnp.pad(w_in[l], ((0, 0), (0, IN_COLS_PAD - w_in.shape[2]))).astype(BF16)
        a, q, kc_r, vc_r, ksa, vsa, kw, vwa, gates = _inproj(
            x, attn_norm[l].reshape(1, D_MODEL), w_pad)
        conv_out = _conv_mixer(
            a, jnp.pad(conv_w[l], ((0, CONV_HALO - CONV_WIDTH), (0, 0))),
            conv_b[l].reshape(1, CONV_CH), conv_ln_g[l].reshape(1, CONV_CH),
            conv_ln_b[l].reshape(1, CONV_CH))
        kc, vc = _compress(kc_r, vc_r,
                           _pe_rows(cmp_k_pe[l]), _w1_cat(cmp_k_w1[l]), cmp_k_w2[l].astype(BF16),
                           _pe_rows(cmp_v_pe[l]), _w1_cat(cmp_v_w1[l]), cmp_v_w2[l].astype(BF16))
        nsa_out = _nsa(q, kc, vc, ksa, vsa, kw, vwa, gates, overlap, slopes)
        wo = w_out[l].astype(BF16)
        wo_n = wo[CONV_CH:].reshape(N_KV_HEADS, Q_PER_KV * HEAD_DIM, D_MODEL)
        wgu = w_gate_up[l].astype(BF16)
        wg = wgu[:, :D_FF].reshape(D_MODEL, N_FF_CHUNKS, FF_CHUNK).transpose(1, 0, 2)
        wu = wgu[:, D_FF:].reshape(D_MODEL, N_FF_CHUNKS, FF_CHUNK).transpose(1, 0, 2)
        wd = w_down[l].astype(BF16).reshape(N_FF_CHUNKS, FF_CHUNK, D_MODEL)
        x = _outffn(x, conv_out, nsa_out, wo[:CONV_CH], wo_n, ffn_norm[l].reshape(1, D_MODEL),
                    wg, wu, wd, final_g, final=(l == depth - 1))
    return x
```

```python
import functools
import math

import jax
import jax.numpy as jnp
import numpy as np
from jax import lax
from jax.experimental import pallas as pl
from jax.experimental.pallas import tpu as pltpu

F32 = jnp.float32
BF16 = jnp.bfloat16

D_MODEL = 1024
HEAD_DIM = 64
CONV_CH = 256
CONV_WIDTH = 31
N_Q_HEADS = 12
N_KV_HEADS = 4
Q_PER_KV = N_Q_HEADS // N_KV_HEADS
NSA_WIDTH = N_Q_HEADS * HEAD_DIM
KV_WIDTH = N_KV_HEADS * HEAD_DIM
CMP_LEN = 32
CMP_STRIDE = 16
CMP_HIDDEN = 256
SEL_BLOCK = 64
SEL_TOPK = 8
WINDOW = 512
N_BRANCH = 3
FORCED_BONUS = 1000.0
D_FF = 2816
EPS = 1e-6
LOG2E = math.log2(math.e)

LANES = 128
GATE_PAD = LANES
FF_CHUNK = 256
N_FF_CHUNKS = D_FF // FF_CHUNK
TM_PROJ = 512
TC_CONV = 512
CONV_HALO = 32
TQ = 256
WIN_KEYS = WINDOW + TQ
MASK_NEG = -(2.0 ** 100)
M_INIT = -1e30
VMEM_LIMIT = 56 * 1024 * 1024


def _alibi_slopes(n):
    def pow2_slopes(m):
        start = 2.0 ** (-8.0 / m)
        return [start ** (i + 1) for i in range(m)]
    if math.log2(n).is_integer():
        s = pow2_slopes(n)
    else:
        c = 2 ** math.floor(math.log2(n))
        s = pow2_slopes(c) + pow2_slopes(2 * c)[0::2][: n - c]
    return np.asarray(s, dtype=np.float32)


def _sigmoid(v):
    return 1.0 / (1.0 + jnp.exp(-v))


def _dot(a, b):
    return jnp.dot(a, b, preferred_element_type=F32)


def _dot_nt(a, b):
    return lax.dot_general(a, b, (((1,), (1,)), ((), ())), preferred_element_type=F32)


def _const_spec(shape):
    nd = len(shape)
    return pl.BlockSpec(shape, lambda *_: (0,) * nd, pipeline_mode=pl.Buffered(1))


C_A = 0
C_Q = C_A + 2 * CONV_CH
C_KC = C_Q + NSA_WIDTH
C_VC = C_KC + KV_WIDTH
C_KS = C_VC + KV_WIDTH
C_VS = C_KS + KV_WIDTH
C_KW = C_VS + KV_WIDTH
C_VW = C_KW + KV_WIDTH
C_G = C_VW + KV_WIDTH
IN_COLS_PAD = C_G + GATE_PAD


def _inproj_kernel(x_ref, g_ref, w_ref, a_ref, q_ref, kcr_ref, vcr_ref, ksa_ref, vsa_ref, kw_ref,
                   vwa_ref, gate_ref):
    tm = x_ref.shape[1]
    x = x_ref[0]
    ms = jnp.mean(x * x, axis=-1, keepdims=True)
    h = (x * lax.rsqrt(ms + EPS) * g_ref[...]).astype(BF16)

    a_ref[0] = _dot(h, w_ref[:, C_A:C_Q])
    zq = _dot(h, w_ref[:, C_Q:C_KC]) * (HEAD_DIM ** -0.5 * LOG2E)
    zero_hi = jnp.zeros((tm, HEAD_DIM), BF16)
    for hd in range(N_Q_HEADS):
        qh = zq[:, hd * HEAD_DIM:(hd + 1) * HEAD_DIM].astype(BF16)
        q_ref[0, hd] = jnp.concatenate([qh, zero_hi], axis=1)
    zkc = _dot(h, w_ref[:, C_KC:C_VC])
    zvc = _dot(h, w_ref[:, C_VC:C_KS])
    for half in range(KV_WIDTH // LANES):
        kcr_ref[0, half] = zkc[:, half * LANES:(half + 1) * LANES]
        vcr_ref[0, half] = zvc[:, half * LANES:(half + 1) * LANES]

    t = pl.program_id(1) * tm + lax.broadcasted_iota(jnp.int32, (tm, HEAD_DIM), 0)
    blk = lax.broadcasted_iota(jnp.int32, (tm, HEAD_DIM), 1)
    onehot = jnp.where(t // SEL_BLOCK == blk, 1.0, 0.0).astype(BF16)
    ones_col = jnp.where(blk == 0, 1.0, 0.0).astype(BF16)
    zks = _dot(h, w_ref[:, C_KS:C_VS])
    zvs = _dot(h, w_ref[:, C_VS:C_KW])
    zkw = _dot(h, w_ref[:, C_KW:C_VW])
    zvw = _dot(h, w_ref[:, C_VW:C_G])
    for g in range(N_KV_HEADS):
        sl = slice(g * HEAD_DIM, (g + 1) * HEAD_DIM)
        ksa_ref[0, g] = jnp.concatenate([zks[:, sl].astype(BF16), onehot], axis=1)
        vsa_ref[0, g] = jnp.concatenate([zvs[:, sl].astype(BF16), ones_col], axis=1)
        kw_ref[0, g] = zkw[:, sl].astype(BF16)
        vwa_ref[0, g] = jnp.concatenate([zvw[:, sl].astype(BF16), ones_col], axis=1)
    gate_ref[0] = _sigmoid(_dot(h, w_ref[:, C_G:IN_COLS_PAD]))


def _inproj(x, norm_g, w_pad):
    b, t, _ = x.shape
    tm = min(TM_PROJ, t)
    grid = (b, t // tm)
    tok = lambda bi, i: (bi, i, 0)
    head = lambda bi, i: (bi, 0, i, 0)
    out_shape = (
        jax.ShapeDtypeStruct((b, t, 2 * CONV_CH), F32),
        jax.ShapeDtypeStruct((b, N_Q_HEADS, t, 2 * HEAD_DIM), BF16),
        jax.ShapeDtypeStruct((b, KV_WIDTH // LANES, t, LANES), F32),
        jax.ShapeDtypeStruct((b, KV_WIDTH // LANES, t, LANES), F32),
        jax.ShapeDtypeStruct((b, N_KV_HEADS, t, 2 * HEAD_DIM), BF16),
        jax.ShapeDtypeStruct((b, N_KV_HEADS, t, 2 * HEAD_DIM), BF16),
        jax.ShapeDtypeStruct((b, N_KV_HEADS, t, HEAD_DIM), BF16),
        jax.ShapeDtypeStruct((b, N_KV_HEADS, t, 2 * HEAD_DIM), BF16),
        jax.ShapeDtypeStruct((b, t, GATE_PAD), F32),
    )
    out_specs = (
        pl.BlockSpec((1, tm, 2 * CONV_CH), tok),
        pl.BlockSpec((1, N_Q_HEADS, tm, 2 * HEAD_DIM), head),
        pl.BlockSpec((1, KV_WIDTH // LANES, tm, LANES), head),
        pl.BlockSpec((1, KV_WIDTH // LANES, tm, LANES), head),
        pl.BlockSpec((1, N_KV_HEADS, tm, 2 * HEAD_DIM), head),
        pl.BlockSpec((1, N_KV_HEADS, tm, 2 * HEAD_DIM), head),
        pl.BlockSpec((1, N_KV_HEADS, tm, HEAD_DIM), head),
        pl.BlockSpec((1, N_KV_HEADS, tm, 2 * HEAD_DIM), head),
        pl.BlockSpec((1, tm, GATE_PAD), tok),
    )
    return pl.pallas_call(
        _inproj_kernel,
        grid=grid,
        in_specs=[
            pl.BlockSpec((1, tm, D_MODEL), tok),
            _const_spec((1, D_MODEL)),
            _const_spec((D_MODEL, IN_COLS_PAD)),
        ],
        out_specs=out_specs,
        out_shape=out_shape,
        compiler_params=pltpu.CompilerParams(
            dimension_semantics=("arbitrary", "arbitrary"), vmem_limit_bytes=VMEM_LIMIT),
        name="inproj",
    )(x, norm_g, w_pad)


def _conv_kernel(a_ref, halo_ref, w_ref, b_ref, lg_ref, lb_ref, o_ref, y_ref):
    tc = a_ref.shape[1]
    am = a_ref[0]
    ah = halo_ref[0]
    ym = am[:, :CONV_CH] * _sigmoid(am[:, CONV_CH:])
    yh = ah[:, :CONV_CH] * _sigmoid(ah[:, CONV_CH:])
    yh = jnp.where(pl.program_id(1) == 0, 0.0, yh)
    y_ref[0:CONV_HALO, :] = yh
    y_ref[CONV_HALO:CONV_HALO + tc, :] = ym
    first = CONV_HALO - (CONV_WIDTH - 1)
    acc = jnp.zeros((tc, CONV_CH), F32)
    for k in range(CONV_WIDTH):
        acc = acc + w_ref[k:k + 1, :] * y_ref[pl.ds(first + k, tc), :]
    acc = acc + b_ref[...]
    mu = jnp.mean(acc, axis=-1, keepdims=True)
    d = acc - mu
    var = jnp.mean(d * d, axis=-1, keepdims=True)
    yn = d * lax.rsqrt(var + EPS) * lg_ref[...] + lb_ref[...]
    o_ref[0] = (yn * _sigmoid(yn)).astype(o_ref.dtype)


def _conv_mixer(a, w_dw, b_dw, ln_g, ln_b):
    b, t, _ = a.shape
    tc = min(TC_CONV, t)
    per = tc // CONV_HALO
    return pl.pallas_call(
        _conv_kernel,
        grid=(b, t // tc),
        in_specs=[
            pl.BlockSpec((1, tc, 2 * CONV_CH), lambda bi, i: (bi, i, 0)),
            pl.BlockSpec((1, CONV_HALO, 2 * CONV_CH),
                         lambda bi, i: (bi, jnp.maximum(i * per - 1, 0), 0)),
            _const_spec((CONV_HALO, CONV_CH)),
            _const_spec((1, CONV_CH)),
            _const_spec((1, CONV_CH)),
            _const_spec((1, CONV_CH)),
        ],
        out_specs=pl.BlockSpec((1, tc, CONV_CH), lambda bi, i: (bi, i, 0)),
        out_shape=jax.ShapeDtypeStruct((b, t, CONV_CH), BF16),
        scratch_shapes=[pltpu.VMEM((CONV_HALO + tc, CONV_CH), F32)],
        compiler_params=pltpu.CompilerParams(
            dimension_semantics=("arbitrary", "arbitrary"), vmem_limit_bytes=VMEM_LIMIT),
        name="conv_mixer",
    )(a, a, w_dw, b_dw, ln_g, ln_b)


def _compress_kernel(kr_ref, vr_ref, kpe_ref, kw1_ref, kw2_ref, vpe_ref, vw1_ref, vw2_ref,
                     kc_ref, vc_ref):
    nch = kr_ref.shape[2] // CMP_STRIDE
    row = lax.broadcasted_iota(jnp.int32, (nch, HEAD_DIM), 0)
    c_start = row * CMP_STRIDE
    s_start = lax.broadcasted_iota(jnp.int32, (nch, HEAD_DIM), 1) * SEL_BLOCK
    overlap = jnp.where((c_start < s_start + SEL_BLOCK) & (c_start + CMP_LEN > s_start),
                        1.0, 0.0).astype(BF16)
    for r_ref, pe_ref, w1_ref, w2_ref, o_ref, tail in (
            (kr_ref, kpe_ref, kw1_ref, kw2_ref, kc_ref, None),
            (vr_ref, vpe_ref, vw1_ref, vw2_ref, vc_ref, overlap)):
        pb = _dot(pe_ref[...], w1_ref[...])
        bias = pb[0:1, :CMP_HIDDEN] + pb[8:9, CMP_HIDDEN:]
        rows = [[r_ref[0, half, pl.ds(l, nch, stride=CMP_STRIDE), :] for l in range(CMP_STRIDE)]
                for half in range(KV_WIDTH // LANES)]
        for g in range(N_KV_HEADS):
            half, odd = divmod(g, LANES // HEAD_DIM)
            sl = slice(odd * HEAD_DIM, (odd + 1) * HEAD_DIM)
            xg = jnp.concatenate([rw[:, sl] for rw in rows[half]], axis=1).astype(BF16)
            lohi = _dot(xg, w1_ref[...])
            lo = lohi[:, :CMP_HIDDEN]
            hi = lohi[:, CMP_HIDDEN:]
            hi_next = jnp.concatenate([hi[1:], jnp.zeros((1, CMP_HIDDEN), F32)], axis=0)
            hid = lo + hi_next + bias
            act = (hid * _sigmoid(hid)).astype(BF16)
            out = _dot(act, w2_ref[...])
            out = jnp.where(row < nch - 1, out, 0.0)
            out = out.astype(o_ref.dtype)
            o_ref[0, g] = out if tail is None else jnp.concatenate([out, tail], axis=1)


def _compress(kc_r, vc_r, kpe, kw1, kw2, vpe, vw1, vw2):
    b, _, t, _ = kc_r.shape
    nch = t // CMP_STRIDE
    blk_len = CMP_STRIDE * HEAD_DIM
    out_sds = [jax.ShapeDtypeStruct((b, N_KV_HEADS, nch, w), BF16) for w in (HEAD_DIM, 2 * HEAD_DIM)]
    out_spec = [pl.BlockSpec((1, N_KV_HEADS, nch, w), lambda bi: (bi, 0, 0, 0))
                for w in (HEAD_DIM, 2 * HEAD_DIM)]
    raw_spec = pl.BlockSpec((1, KV_WIDTH // LANES, t, LANES), lambda bi: (bi, 0, 0, 0))
    w_specs = [_const_spec((16, blk_len)), _const_spec((blk_len, 2 * CMP_HIDDEN)),
               _const_spec((CMP_HIDDEN, HEAD_DIM))]
    return pl.pallas_call(
        _compress_kernel,
        grid=(b,),
        in_specs=[raw_spec, raw_spec] + w_specs + w_specs,
        out_specs=tuple(out_spec),
        out_shape=tuple(out_sds),
        compiler_params=pltpu.CompilerParams(
            dimension_semantics=("arbitrary",), vmem_limit_bytes=VMEM_LIMIT),
        name="compress",
    )(kc_r, vc_r, kpe, kw1, kw2, vpe, vw1, vw2)


def _nsa_kernel(slope_ref, q_ref, kc_ref, vca_ref, ksa_ref, vsa_ref, kw_ref, vwa_ref, gate_ref,
                o_ref, qa_ref, se_ref, so_ref, mx_ref, pa_ref, pb_ref, acc_ref, pw_ref, list_ref):
    g = pl.program_id(1)
    i = pl.program_id(2)
    t0 = i * TQ
    ncp = kc_ref.shape[2]
    slopes = [slope_ref[g * Q_PER_KV + r] for r in range(Q_PER_KV)]
    rows = [slice(r * TQ, (r + 1) * TQ) for r in range(Q_PER_KV)]
    halves = [slice(hf * LANES, (hf + 1) * LANES) for hf in range(TQ // LANES)]

    q_full = q_ref[0].reshape(Q_PER_KV * TQ, 2 * HEAD_DIM)
    q3 = q_full[:, :HEAD_DIM]

    s_c = _dot_nt(q3, kc_ref[0, 0])
    n_idx = lax.broadcasted_iota(jnp.int32, (TQ, ncp), 1)
    t_idx = t0 + lax.broadcasted_iota(jnp.int32, (TQ, ncp), 0)
    c_valid = n_idx * CMP_STRIDE + (CMP_LEN - 1) <= t_idx
    n_row = lax.broadcasted_iota(jnp.int32, (1, ncp), 1)
    c_pos = (n_row * CMP_STRIDE - t0).astype(F32) + 0.5 * (CMP_LEN - 1)
    vca = vca_ref[0, 0]
    o_c = []
    imp = jnp.zeros((TQ, LANES), F32)
    for r in range(Q_PER_KV):
        sr = jnp.where(c_valid, s_c[rows[r]] + slopes[r] * c_pos, -jnp.inf)
        m = jnp.max(sr, axis=1, keepdims=True)
        m = jnp.where(m == -jnp.inf, 0.0, m)
        p = jnp.exp2(sr - m)
        inv = 1.0 / jnp.maximum(jnp.sum(p, axis=1, keepdims=True), 1e-30)
        pv = _dot(p.astype(BF16), vca) * inv
        o_c.append(pv[:, :HEAD_DIM])
        imp = imp + pv

    w0 = pl.multiple_of(jnp.maximum(t0 - WINDOW, 0), TQ)
    s_w = _dot_nt(q3, kw_ref[0, 0, pl.ds(w0, WIN_KEYS), :])
    wk_row = w0 - t0 + lax.broadcasted_iota(jnp.int32, (1, WIN_KEYS), 1)
    dist = (lax.broadcasted_iota(jnp.int32, (TQ, WIN_KEYS), 0)
            - (w0 - t0) - lax.broadcasted_iota(jnp.int32, (TQ, WIN_KEYS), 1))
    w_valid = (dist >= 0) & (dist < WINDOW)
    wk_pos = wk_row.astype(F32)
    for r in range(Q_PER_KV):
        sr = jnp.where(w_valid, s_w[rows[r]] + slopes[r] * wk_pos, MASK_NEG)
        m = jnp.max(sr, axis=1, keepdims=True)
        pw_ref[rows[r], :] = jnp.exp2(sr - m).astype(BF16)
    acc_w = _dot(pw_ref[...], vwa_ref[0, 0, pl.ds(w0, WIN_KEYS), :])
    o_w = []
    for r in range(Q_PER_KV):
        a = acc_w[rows[r]]
        o_w.append(a[:, :HEAD_DIM] / a[:, HEAD_DIM:HEAD_DIM + 1])

    imp_t = imp.T[HEAD_DIM:, :]
    j_t = lax.broadcasted_iota(jnp.int32, (LANES - HEAD_DIM, TQ), 0)
    cur_t = (t0 + lax.broadcasted_iota(jnp.int32, (LANES - HEAD_DIM, TQ), 1)) // SEL_BLOCK
    forced = (j_t == 0) | (j_t == cur_t) | (j_t == cur_t - 1)
    cand = jnp.where(j_t <= cur_t, imp_t + jnp.where(forced, FORCED_BONUS, 0.0), -jnp.inf)
    sel = jnp.zeros(cand.shape, jnp.bool_)
    for _ in range(SEL_TOPK):
        mx = jnp.max(cand, axis=0, keepdims=True)
        first = jnp.min(jnp.where(cand == mx, j_t, LANES), axis=0, keepdims=True)
        hit = j_t == first
        sel = sel | (hit & (mx > -jnp.inf))
        cand = jnp.where(hit, -jnp.inf, cand)
    bias_t = jnp.concatenate([jnp.zeros(cand.shape, F32), jnp.where(sel, 0.0, MASK_NEG)], axis=0)
    sel_bias = bias_t.T
    sel_bias_bf = sel_bias.astype(BF16)
    for r in range(Q_PER_KV):
        qa_ref[rows[r], :] = q_full[rows[r]] + sel_bias_bf

    blk_any = jnp.broadcast_to(jnp.max(sel_bias, axis=0, keepdims=True), (8, LANES))
    per_chunk = TQ // SEL_BLOCK
    n_chunks = list_ref.shape[0] - 2
    chunk_any = blk_any
    for d in range(1, per_chunk):
        chunk_any = jnp.maximum(chunk_any, pltpu.roll(blk_any, LANES - d, 1))
    cnt = jnp.int32(0)
    idle = jnp.int32(0)
    for c in range(n_chunks):
        below = c < i
        hit = (chunk_any[0, HEAD_DIM + per_chunk * c] > 0.5 * MASK_NEG) & below
        list_ref[cnt] = jnp.int32(c)
        cnt = cnt + hit.astype(jnp.int32)
        idle = jnp.where(below & jnp.logical_not(hit), c, idle)
    list_ref[cnt] = jnp.where(i + 1 < n_chunks, i + 1, idle)
    list_ref[cnt + 1 - (cnt & 1)] = i
    n_pairs = (cnt + 2) // 2
    k_last = n_pairs - 1

    lane_row = lax.broadcasted_iota(jnp.int32, (1, TQ), 1)
    causal = (lax.broadcasted_iota(jnp.int32, (TQ, TQ), 1)
              <= lax.broadcasted_iota(jnp.int32, (TQ, TQ), 0))
    mx_ref[...] = jnp.full(mx_ref.shape, M_INIT, F32)

    def raw_scores(pos, dst_ref, k, diagonal):
        k0 = pl.multiple_of(list_ref[pos] * TQ, TQ)
        s_all = _dot_nt(qa_ref[...], ksa_ref[0, 0, pl.ds(k0, TQ), :])
        for r in range(Q_PER_KV):
            sr = s_all[rows[r]]
            dst_ref[k, rows[r], :] = jnp.where(causal, sr, MASK_NEG) if diagonal else sr

    def alibi_row(pos, r):
        return slopes[r] * (list_ref[pos] * TQ - t0 + lane_row).astype(F32)

    def fold_max(pos, src_ref, k):
        for r in range(Q_PER_KV):
            sr = src_ref[k, rows[r], :] + alibi_row(pos, r)
            part = sr[:, halves[0]]
            for hf in halves[1:]:
                part = jnp.maximum(part, sr[:, hf])
            mx_ref[rows[r], :] = jnp.maximum(mx_ref[rows[r], :], part)

    raw_scores(0, se_ref, 0, False)

    def pass1(k, carry):
        raw_scores(2 * k + 1, so_ref, k, False)
        fold_max(2 * k, se_ref, k)
        raw_scores(2 * k + 2, se_ref, k + 1, False)
        fold_max(2 * k + 1, so_ref, k)
        return carry

    lax.fori_loop(0, k_last, pass1, 0)
    raw_scores(2 * k_last + 1, so_ref, k_last, True)
    fold_max(2 * k_last, se_ref, k_last)
    fold_max(2 * k_last + 1, so_ref, k_last)

    for r in range(Q_PER_KV):
        m = jnp.max(mx_ref[rows[r], :], axis=1, keepdims=True)
        mx_ref[rows[r], :] = jnp.broadcast_to(m, (TQ, LANES))
    acc_ref[...] = jnp.zeros(acc_ref.shape, F32)

    def probs(pos, src_ref, k, dst_ref):
        for r in range(Q_PER_KV):
            mb = mx_ref[rows[r], :]
            bias = alibi_row(pos, r)
            for hf in halves:
                dst_ref[rows[r], hf] = jnp.exp2(src_ref[k, rows[r], hf] + bias[:, hf] - mb).astype(BF16)

    def add_pv(pos, src_ref):
        k0 = pl.multiple_of(list_ref[pos] * TQ, TQ)
        acc_ref[...] += _dot(src_ref[...], vsa_ref[0, 0, pl.ds(k0, TQ), :])

    probs(0, se_ref, 0, pa_ref)

    def pass2(k, carry):
        add_pv(2 * k, pa_ref)
        probs(2 * k + 1, so_ref, k, pb_ref)
        add_pv(2 * k + 1, pb_ref)
        probs(2 * k + 2, se_ref, k + 1, pa_ref)
        return carry

    lax.fori_loop(0, k_last, pass2, 0)
    add_pv(2 * k_last, pa_ref)
    probs(2 * k_last + 1, so_ref, k_last, pb_ref)
    add_pv(2 * k_last + 1, pb_ref)
    o_s = []
    for r in range(Q_PER_KV):
        a = acc_ref[rows[r], :]
        o_s.append(a[:, :HEAD_DIM] / a[:, HEAD_DIM:HEAD_DIM + 1])

    gates = gate_ref[0]
    lane = lax.broadcasted_iota(jnp.int32, (TQ, LANES), 1)
    outs = []
    for r in range(Q_PER_KV):
        col = g * (Q_PER_KV * N_BRANCH) + r * N_BRANCH
        gc, gs, gw = (jnp.sum(jnp.where(lane == col + br, gates, 0.0), axis=1, keepdims=True)
                      for br in range(N_BRANCH))
        outs.append(gc * o_c[r] + gs * o_s[r] + gw * o_w[r])
    o_ref[0, 0] = jnp.concatenate(outs, axis=1).astype(o_ref.dtype)


def _nsa(q, kc, vca, ksa, vsa, kw, vwa, gates, slopes):
    b, _, t, _ = q.shape
    ncp = kc.shape[2]
    grp = lambda bi, g, i: (bi, g, 0, 0)
    return pl.pallas_call(
        _nsa_kernel,
        grid=(b, N_KV_HEADS, t // TQ),
        in_specs=[
            pl.BlockSpec(memory_space=pltpu.SMEM),
            pl.BlockSpec((1, Q_PER_KV, TQ, 2 * HEAD_DIM), lambda bi, g, i: (bi, g, i, 0)),
            pl.BlockSpec((1, 1, ncp, HEAD_DIM), grp),
            pl.BlockSpec((1, 1, ncp, 2 * HEAD_DIM), grp),
            pl.BlockSpec((1, 1, t, 2 * HEAD_DIM), grp),
            pl.BlockSpec((1, 1, t, 2 * HEAD_DIM), grp),
            pl.BlockSpec((1, 1, t, HEAD_DIM), grp),
            pl.BlockSpec((1, 1, t, 2 * HEAD_DIM), grp),
            pl.BlockSpec((1, TQ, GATE_PAD), lambda bi, g, i: (bi, i, 0)),
        ],
        out_specs=pl.BlockSpec((1, 1, TQ, Q_PER_KV * HEAD_DIM), lambda bi, g, i: (bi, g, i, 0)),
        out_shape=jax.ShapeDtypeStruct((b, N_KV_HEADS, t, Q_PER_KV * HEAD_DIM), BF16),
        scratch_shapes=[
            pltpu.VMEM((Q_PER_KV * TQ, 2 * HEAD_DIM), BF16),
            pltpu.VMEM((t // (2 * TQ), Q_PER_KV * TQ, TQ), F32),
            pltpu.VMEM((t // (2 * TQ), Q_PER_KV * TQ, TQ), F32),
            pltpu.VMEM((Q_PER_KV * TQ, LANES), F32),
            pltpu.VMEM((Q_PER_KV * TQ, TQ), BF16),
            pltpu.VMEM((Q_PER_KV * TQ, TQ), BF16),
            pltpu.VMEM((Q_PER_KV * TQ, LANES), F32),
            pltpu.VMEM((Q_PER_KV * TQ, WIN_KEYS), BF16),
            pltpu.SMEM((t // TQ + 2,), jnp.int32),
        ],
        compiler_params=pltpu.CompilerParams(
            dimension_semantics=("arbitrary", "arbitrary", "arbitrary"),
            vmem_limit_bytes=VMEM_LIMIT),
        name="nsa_attention",
    )(slopes, q, kc, vca, ksa, vsa, kw, vwa, gates)


def _outffn_kernel(x_ref, c_ref, n_ref, woc_ref, won_ref, fg_ref, wg_ref, wu_ref, wd_ref, fin_ref,
                   o_ref, acc_ref, h_ref, *, final):
    x1 = x_ref[0] + _dot(c_ref[0], woc_ref[...])
    for g in range(N_KV_HEADS):
        x1 = x1 + _dot(n_ref[0, g], won_ref[g])
    ms = jnp.mean(x1 * x1, axis=-1, keepdims=True)
    h_ref[...] = (x1 * lax.rsqrt(ms + EPS) * fg_ref[...]).astype(BF16)
    acc_ref[...] = x1

    def body(c, carry):
        h = h_ref[...]
        gate = _dot(h, wg_ref[c])
        up = _dot(h, wu_ref[c])
        act = (gate * _sigmoid(gate) * up).astype(BF16)
        acc_ref[...] += _dot(act, wd_ref[c])
        return carry

    lax.fori_loop(0, N_FF_CHUNKS, body, 0)
    y = acc_ref[...]
    if final:
        ms = jnp.mean(y * y, axis=-1, keepdims=True)
        y = y * lax.rsqrt(ms + EPS) * fin_ref[...]
    o_ref[0] = y


def _outffn(x, conv_out, nsa_out, wo_c, wo_n, ffn_g, wg, wu, wd, final_g, final):
    b, t, _ = x.shape
    tm = min(TM_PROJ, t)
    tok = lambda bi, i: (bi, i, 0)
    return pl.pallas_call(
        functools.partial(_outffn_kernel, final=final),
        grid=(b, t // tm),
        in_specs=[
            pl.BlockSpec((1, tm, D_MODEL), tok),
            pl.BlockSpec((1, tm, CONV_CH), tok),
            pl.BlockSpec((1, N_KV_HEADS, tm, Q_PER_KV * HEAD_DIM), lambda bi, i: (bi, 0, i, 0)),
            _const_spec((CONV_CH, D_MODEL)),
            _const_spec((N_KV_HEADS, Q_PER_KV * HEAD_DIM, D_MODEL)),
            _const_spec((1, D_MODEL)),
            _const_spec((N_FF_CHUNKS, D_MODEL, FF_CHUNK)),
            _const_spec((N_FF_CHUNKS, D_MODEL, FF_CHUNK)),
            _const_spec((N_FF_CHUNKS, FF_CHUNK, D_MODEL)),
            _const_spec((1, D_MODEL)),
        ],
        out_specs=pl.BlockSpec((1, tm, D_MODEL), tok),
        out_shape=jax.ShapeDtypeStruct((b, t, D_MODEL), F32),
        scratch_shapes=[pltpu.VMEM((tm, D_MODEL), F32), pltpu.VMEM((tm, D_MODEL), BF16)],
        compiler_params=pltpu.CompilerParams(
            dimension_semantics=("arbitrary", "arbitrary"), vmem_limit_bytes=VMEM_LIMIT),
        name="outproj_ffn",
    )(x, conv_out, nsa_out, wo_c, wo_n, ffn_g, wg, wu, wd, final_g)


def _pe_rows(pe):
    lo = pe[:CMP_STRIDE].reshape(1, -1)
    hi = pe[CMP_STRIDE:].reshape(1, -1)
    z = jnp.zeros((7, lo.shape[1]), pe.dtype)
    return jnp.concatenate([lo, z, hi, z], axis=0).astype(BF16)


def _w1_cat(w1):
    lo = w1[:CMP_STRIDE].reshape(CMP_STRIDE * HEAD_DIM, CMP_HIDDEN)
    hi = w1[CMP_STRIDE:].reshape(CMP_STRIDE * HEAD_DIM, CMP_HIDDEN)
    return jnp.concatenate([lo, hi], axis=1).astype(BF16)


def kernel(x, attn_norm, w_in, conv_w, conv_b, conv_ln_g, conv_ln_b, cmp_k_pe, cmp_k_w1, cmp_k_w2,
           cmp_v_pe, cmp_v_w1, cmp_v_w2, w_out, ffn_norm, w_gate_up, w_down, final_norm):
    depth = w_in.shape[0]
    t = x.shape[1]
    assert t % (2 * TQ) == 0 and t // SEL_BLOCK <= LANES - HEAD_DIM and t >= WIN_KEYS
    slopes =jnp.asarray(_alibi_slopes(N_Q_HEADS) * np.float32(LOG2E))
    final_g = final_norm.reshape(1, D_MODEL)
    for l in range(depth):
        w_pad = jnp.pad(w_in[l], ((0, 0), (0, IN_COLS_PAD - w_in.shape[2]))).astype(BF16)
        a, q, kc_r, vc_r, ksa, vsa, kw, vwa, gates = _inproj(
            x, attn_norm[l].reshape(1, D_MODEL), w_pad)
        conv_out = _conv_mixer(
            a, jnp.pad(conv_w[l], ((0, CONV_HALO - CONV_WIDTH), (0, 0))),
            conv_b[l].reshape(1, CONV_CH), conv_ln_g[l].reshape(1, CONV_CH),
            conv_ln_b[l].reshape(1, CONV_CH))
        kc, vca = _compress(kc_r, vc_r,
                           _pe_rows(cmp_k_pe[l]), _w1_cat(cmp_k_w1[l]), cmp_k_w2[l].astype(BF16),
                           _pe_rows(cmp_v_pe[l]), _w1_cat(cmp_v_w1[l]), cmp_v_w2[l].astype(BF16))
        nsa_out = _nsa(q, kc, vca, ksa, vsa, kw, vwa, gates, slopes)
        wo = w_out[l].astype(BF16)
        wo_n = wo[CONV_CH:].reshape(N_KV_HEADS, Q_PER_KV * HEAD_DIM, D_MODEL)
        wgu = w_gate_up[l].astype(BF16)
        wg = wgu[:, :D_FF].reshape(D_MODEL, N_FF_CHUNKS, FF_CHUNK).transpose(1, 0, 2)
        wu = wgu[:, D_FF:].reshape(D_MODEL, N_FF_CHUNKS, FF_CHUNK).transpose(1, 0, 2)
        wd = w_down[l].astype(BF16).reshape(N_FF_CHUNKS, FF_CHUNK, D_MODEL)
        x = _outffn(x, conv_out, nsa_out, wo[:CONV_CH], wo_n, ffn_norm[l].reshape(1, D_MODEL),
                    wg, wu, wd, final_g, final=(l == depth - 1))
    return x
```

```python
import functools
import math

import jax
import jax.numpy as jnp
import numpy as np
from jax import lax
from jax.experimental import pallas as pl
from jax.experimental.pallas import tpu as pltpu

F32 = jnp.float32
BF16 = jnp.bfloat16

D_MODEL = 1024
HEAD_DIM = 64
CONV_CH = 256
CONV_WIDTH = 31
N_Q_HEADS = 12
N_KV_HEADS = 4
Q_PER_KV = N_Q_HEADS // N_KV_HEADS
NSA_WIDTH = N_Q_HEADS * HEAD_DIM
KV_WIDTH = N_KV_HEADS * HEAD_DIM
CMP_LEN = 32
CMP_STRIDE = 16
CMP_HIDDEN = 256
SEL_BLOCK = 64
SEL_TOPK = 8
WINDOW = 512
N_BRANCH = 3
FORCED_BONUS = 1000.0
D_FF = 2816
EPS = 1e-6
LOG2E = math.log2(math.e)

LANES = 128
SUBLANES = 8
GATE_PAD = LANES
FF_CHUNK = 256
N_FF_CHUNKS = D_FF // FF_CHUNK
TM_PROJ = 512
TC_CONV = 512
CONV_HALO = 32
TQ = 256
WIN_KEYS = WINDOW + TQ
MASK_NEG = -(2.0 ** 100)
M_INIT = -1e30
VMEM_LIMIT = 56 * 1024 * 1024


def _alibi_slopes(n):
    def pow2_slopes(m):
        start = 2.0 ** (-8.0 / m)
        return [start ** (i + 1) for i in range(m)]
    if math.log2(n).is_integer():
        s = pow2_slopes(n)
    else:
        c = 2 ** math.floor(math.log2(n))
        s = pow2_slopes(c) + pow2_slopes(2 * c)[0::2][: n - c]
    return np.asarray(s, dtype=np.float32)


def _sigmoid(v):
    return 1.0 / (1.0 + jnp.exp(-v))


def _dot(a, b):
    return jnp.dot(a, b, preferred_element_type=F32)


def _dot_nt(a, b):
    return lax.dot_general(a, b, (((1,), (1,)), ((), ())), preferred_element_type=F32)


def _const_spec(shape):
    nd = len(shape)
    return pl.BlockSpec(shape, lambda *_: (0,) * nd, pipeline_mode=pl.Buffered(1))


C_A = 0
C_Q = C_A + 2 * CONV_CH
C_KC = C_Q + NSA_WIDTH
C_VC = C_KC + KV_WIDTH
C_KS = C_VC + KV_WIDTH
C_VS = C_KS + KV_WIDTH
C_KW = C_VS + KV_WIDTH
C_VW = C_KW + KV_WIDTH
C_G = C_VW + KV_WIDTH
IN_COLS_PAD = C_G + GATE_PAD


def _inproj_kernel(x_ref, g_ref, w_ref, a_ref, q_ref, kcr_ref, vcr_ref, ksa_ref, vsa_ref, kw_ref,
                   vwa_ref, gate_ref):
    tm = x_ref.shape[1]
    x = x_ref[0]
    ms = jnp.mean(x * x, axis=-1, keepdims=True)
    h = (x * lax.rsqrt(ms + EPS) * g_ref[...]).astype(BF16)

    a_ref[0] = _dot(h, w_ref[:, C_A:C_Q])
    zq = _dot(h, w_ref[:, C_Q:C_KC]) * (HEAD_DIM ** -0.5 * LOG2E)
    zero_hi = jnp.zeros((tm, HEAD_DIM), BF16)
    for hd in range(N_Q_HEADS):
        qh = zq[:, hd * HEAD_DIM:(hd + 1) * HEAD_DIM].astype(BF16)
        q_ref[0, hd] = jnp.concatenate([qh, zero_hi], axis=1)
    zkc = _dot(h, w_ref[:, C_KC:C_VC])
    zvc = _dot(h, w_ref[:, C_VC:C_KS])
    for half in range(KV_WIDTH // LANES):
        kcr_ref[0, half] = zkc[:, half * LANES:(half + 1) * LANES]
        vcr_ref[0, half] = zvc[:, half * LANES:(half + 1) * LANES]

    t = pl.program_id(1) * tm + lax.broadcasted_iota(jnp.int32, (tm, HEAD_DIM), 0)
    blk = lax.broadcasted_iota(jnp.int32, (tm, HEAD_DIM), 1)
    onehot = jnp.where(t // SEL_BLOCK == blk, 1.0, 0.0).astype(BF16)
    ones_col = jnp.where(blk == 0, 1.0, 0.0).astype(BF16)
    zks = _dot(h, w_ref[:, C_KS:C_VS])
    zvs = _dot(h, w_ref[:, C_VS:C_KW])
    zkw = _dot(h, w_ref[:, C_KW:C_VW])
    zvw = _dot(h, w_ref[:, C_VW:C_G])
    for g in range(N_KV_HEADS):
        sl = slice(g * HEAD_DIM, (g + 1) * HEAD_DIM)
        ksa_ref[0, g] = jnp.concatenate([zks[:, sl].astype(BF16), onehot], axis=1)
        vsa_ref[0, g] = jnp.concatenate([zvs[:, sl].astype(BF16), ones_col], axis=1)
        kw_ref[0, g] = zkw[:, sl].astype(BF16)
        vwa_ref[0, g] = jnp.concatenate([zvw[:, sl].astype(BF16), ones_col], axis=1)
    gate_ref[0] = _sigmoid(_dot(h, w_ref[:, C_G:IN_COLS_PAD]))


def _inproj(x, norm_g, w_pad):
    b, t, _ = x.shape
    tm = min(TM_PROJ, t)
    grid = (b, t // tm)
    tok = lambda bi, i: (bi, i, 0)
    head = lambda bi, i: (bi, 0, i, 0)
    out_shape = (
        jax.ShapeDtypeStruct((b, t, 2 * CONV_CH), F32),
        jax.ShapeDtypeStruct((b, N_Q_HEADS, t, 2 * HEAD_DIM), BF16),
        jax.ShapeDtypeStruct((b, KV_WIDTH // LANES, t, LANES), F32),
        jax.ShapeDtypeStruct((b, KV_WIDTH // LANES, t, LANES), F32),
        jax.ShapeDtypeStruct((b, N_KV_HEADS, t, 2 * HEAD_DIM), BF16),
        jax.ShapeDtypeStruct((b, N_KV_HEADS, t, 2 * HEAD_DIM), BF16),
        jax.ShapeDtypeStruct((b, N_KV_HEADS, t, HEAD_DIM), BF16),
        jax.ShapeDtypeStruct((b, N_KV_HEADS, t, 2 * HEAD_DIM), BF16),
        jax.ShapeDtypeStruct((b, t, GATE_PAD), F32),
    )
    out_specs = (
        pl.BlockSpec((1, tm, 2 * CONV_CH), tok),
        pl.BlockSpec((1, N_Q_HEADS, tm, 2 * HEAD_DIM), head),
        pl.BlockSpec((1, KV_WIDTH // LANES, tm, LANES), head),
        pl.BlockSpec((1, KV_WIDTH // LANES, tm, LANES), head),
        pl.BlockSpec((1, N_KV_HEADS, tm, 2 * HEAD_DIM), head),
        pl.BlockSpec((1, N_KV_HEADS, tm, 2 * HEAD_DIM), head),
        pl.BlockSpec((1, N_KV_HEADS, tm, HEAD_DIM), head),
        pl.BlockSpec((1, N_KV_HEADS, tm, 2 * HEAD_DIM), head),
        pl.BlockSpec((1, tm, GATE_PAD), tok),
    )
    return pl.pallas_call(
        _inproj_kernel,
        grid=grid,
        in_specs=[
            pl.BlockSpec((1, tm, D_MODEL), tok),
            _const_spec((1, D_MODEL)),
            _const_spec((D_MODEL, IN_COLS_PAD)),
        ],
        out_specs=out_specs,
        out_shape=out_shape,
        compiler_params=pltpu.CompilerParams(
            dimension_semantics=("arbitrary", "arbitrary"), vmem_limit_bytes=VMEM_LIMIT),
        name="inproj",
    )(x, norm_g, w_pad)


def _conv_kernel(a_ref, halo_ref, w_ref, b_ref, lg_ref, lb_ref, o_ref, y_ref):
    tc = a_ref.shape[1]
    am = a_ref[0]
    ah = halo_ref[0]
    ym = am[:, :CONV_CH] * _sigmoid(am[:, CONV_CH:])
    yh = ah[:, :CONV_CH] * _sigmoid(ah[:, CONV_CH:])
    yh = jnp.where(pl.program_id(1) == 0, 0.0, yh)
    y_ref[0:CONV_HALO, :] = yh
    y_ref[CONV_HALO:CONV_HALO + tc, :] = ym
    first = CONV_HALO - (CONV_WIDTH - 1)
    acc = b_ref[...]
    for rho in range(SUBLANES):
        n_rows = tc if rho == 0 else tc + SUBLANES
        z = None
        for base in range(0, CONV_HALO + 1, SUBLANES):
            k = base + rho - first
            if 0 <= k < CONV_WIDTH:
                term = w_ref[k:k + 1, :] * y_ref[pl.ds(base, n_rows), :]
                z = term if z is None else z + term
        acc = acc + (z if rho == 0 else z[rho:rho + tc, :])
    mu = jnp.mean(acc, axis=-1, keepdims=True)
    d = acc - mu
    var = jnp.mean(d * d, axis=-1, keepdims=True)
    yn = d * lax.rsqrt(var + EPS) * lg_ref[...] + lb_ref[...]
    o_ref[0] = (yn * _sigmoid(yn)).astype(o_ref.dtype)


def _conv_mixer(a, w_dw, b_dw, ln_g, ln_b):
    b, t, _ = a.shape
    tc = min(TC_CONV, t)
    per = tc // CONV_HALO
    return pl.pallas_call(
        _conv_kernel,
        grid=(b, t // tc),
        in_specs=[
            pl.BlockSpec((1, tc, 2 * CONV_CH), lambda bi, i: (bi, i, 0)),
            pl.BlockSpec((1, CONV_HALO, 2 * CONV_CH),
                         lambda bi, i: (bi, jnp.maximum(i * per - 1, 0), 0)),
            _const_spec((CONV_HALO, CONV_CH)),
            _const_spec((1, CONV_CH)),
            _const_spec((1, CONV_CH)),
            _const_spec((1, CONV_CH)),
        ],
        out_specs=pl.BlockSpec((1, tc, CONV_CH), lambda bi, i: (bi, i, 0)),
        out_shape=jax.ShapeDtypeStruct((b, t, CONV_CH), BF16),
        scratch_shapes=[pltpu.VMEM((CONV_HALO + tc, CONV_CH), F32)],
        compiler_params=pltpu.CompilerParams(
            dimension_semantics=("arbitrary", "arbitrary"), vmem_limit_bytes=VMEM_LIMIT),
        name="conv_mixer",
    )(a, a, w_dw, b_dw, ln_g, ln_b)


def _compress_kernel(kr_ref, vr_ref, kpe_ref, kw1_ref, kw2_ref, vpe_ref, vw1_ref, vw2_ref,
                     kc_ref, vc_ref):
    nch = kr_ref.shape[2] // CMP_STRIDE
    row = lax.broadcasted_iota(jnp.int32, (nch, HEAD_DIM), 0)
    c_start = row * CMP_STRIDE
    s_start = lax.broadcasted_iota(jnp.int32, (nch, HEAD_DIM), 1) * SEL_BLOCK
    overlap = jnp.where((c_start < s_start + SEL_BLOCK) & (c_start + CMP_LEN > s_start),
                        1.0, 0.0).astype(BF16)
    for r_ref, pe_ref, w1_ref, w2_ref, o_ref, tail in (
            (kr_ref, kpe_ref, kw1_ref, kw2_ref, kc_ref, None),
            (vr_ref, vpe_ref, vw1_ref, vw2_ref, vc_ref, overlap)):
        pb = _dot(pe_ref[...], w1_ref[...])
        bias = pb[0:1, :CMP_HIDDEN] + pb[8:9, CMP_HIDDEN:]
        rows = [[r_ref[0, half, pl.ds(l, nch, stride=CMP_STRIDE), :] for l in range(CMP_STRIDE)]
                for half in range(KV_WIDTH // LANES)]
        for g in range(N_KV_HEADS):
            half, odd = divmod(g, LANES // HEAD_DIM)
            sl = slice(odd * HEAD_DIM, (odd + 1) * HEAD_DIM)
            xg = jnp.concatenate([rw[:, sl] for rw in rows[half]], axis=1).astype(BF16)
            lohi = _dot(xg, w1_ref[...])
            lo = lohi[:, :CMP_HIDDEN]
            hi = lohi[:, CMP_HIDDEN:]
            hi_next = jnp.concatenate([hi[1:], jnp.zeros((1, CMP_HIDDEN), F32)], axis=0)
            hid = lo + hi_next + bias
            act = (hid * _sigmoid(hid)).astype(BF16)
            out = _dot(act, w2_ref[...])
            out = jnp.where(row < nch - 1, out, 0.0)
            out = out.astype(o_ref.dtype)
            o_ref[0, g] = out if tail is None else jnp.concatenate([out, tail], axis=1)


def _compress(kc_r, vc_r, kpe, kw1, kw2, vpe, vw1, vw2):
    b, _, t, _ = kc_r.shape
    nch = t // CMP_STRIDE
    blk_len = CMP_STRIDE * HEAD_DIM
    out_sds = [jax.ShapeDtypeStruct((b, N_KV_HEADS, nch, w), BF16) for w in (HEAD_DIM, 2 * HEAD_DIM)]
    out_spec = [pl.BlockSpec((1, N_KV_HEADS, nch, w), lambda bi: (bi, 0, 0, 0))
                for w in (HEAD_DIM, 2 * HEAD_DIM)]
    raw_spec = pl.BlockSpec((1, KV_WIDTH // LANES, t, LANES), lambda bi: (bi, 0, 0, 0))
    w_specs = [_const_spec((16, blk_len)), _const_spec((blk_len, 2 * CMP_HIDDEN)),
               _const_spec((CMP_HIDDEN, HEAD_DIM))]
    return pl.pallas_call(
        _compress_kernel,
        grid=(b,),
        in_specs=[raw_spec, raw_spec] + w_specs + w_specs,
        out_specs=tuple(out_spec),
        out_shape=tuple(out_sds),
        compiler_params=pltpu.CompilerParams(
            dimension_semantics=("arbitrary",), vmem_limit_bytes=VMEM_LIMIT),
        name="compress",
    )(kc_r, vc_r, kpe, kw1, kw2, vpe, vw1, vw2)


def _nsa_kernel(slope_ref, q_ref, kc_ref, vca_ref, ksa_ref, vsa_ref, kw_ref, vwa_ref, gate_ref,
                o_ref, qa_ref, se_ref, so_ref, mx_ref, pa_ref, pb_ref, acc_ref, pw_ref, list_ref):
    g = pl.program_id(1)
    i = pl.program_id(2)
    t0 = i * TQ
    ncp = kc_ref.shape[2]
    slopes = [slope_ref[g * Q_PER_KV + r] for r in range(Q_PER_KV)]
    rows = [slice(r * TQ, (r + 1) * TQ) for r in range(Q_PER_KV)]
    halves = [slice(hf * LANES, (hf + 1) * LANES) for hf in range(TQ // LANES)]

    q_full = q_ref[0].reshape(Q_PER_KV * TQ, 2 * HEAD_DIM)
    q3 = q_full[:, :HEAD_DIM]

    w0 = pl.multiple_of(jnp.maximum(t0 - WINDOW, 0), TQ)
    s_w = _dot_nt(q3, kw_ref[0, 0, pl.ds(w0, WIN_KEYS), :])
    wk_row = w0 - t0 + lax.broadcasted_iota(jnp.int32, (1, WIN_KEYS), 1)
    dist = (lax.broadcasted_iota(jnp.int32, (TQ, WIN_KEYS), 0)
            - (w0 - t0) - lax.broadcasted_iota(jnp.int32, (TQ, WIN_KEYS), 1))
    w_valid = (dist >= 0) & (dist < WINDOW)
    wk_pos = wk_row.astype(F32)
    for r in range(Q_PER_KV):
        sr = jnp.where(w_valid, s_w[rows[r]] + slopes[r] * wk_pos, MASK_NEG)
        m = jnp.max(sr, axis=1, keepdims=True)
        pw_ref[rows[r], :] = jnp.exp2(sr - m).astype(BF16)
    acc_w = _dot(pw_ref[...], vwa_ref[0, 0, pl.ds(w0, WIN_KEYS), :])

    s_c = _dot_nt(q3, kc_ref[0, 0])
    n_idx = lax.broadcasted_iota(jnp.int32, (TQ, ncp), 1)
    t_idx = t0 + lax.broadcasted_iota(jnp.int32, (TQ, ncp), 0)
    c_valid = n_idx * CMP_STRIDE + (CMP_LEN - 1) <= t_idx
    n_row = lax.broadcasted_iota(jnp.int32, (1, ncp), 1)
    c_pos = (n_row * CMP_STRIDE - t0).astype(F32) + 0.5 * (CMP_LEN - 1)
    vca = vca_ref[0, 0]
    o_c = []
    imp = jnp.zeros((TQ, LANES), F32)
    for r in range(Q_PER_KV):
        sr = jnp.where(c_valid, s_c[rows[r]] + slopes[r] * c_pos, -jnp.inf)
        m = jnp.max(sr, axis=1, keepdims=True)
        m = jnp.where(m == -jnp.inf, 0.0, m)
        p = jnp.exp2(sr - m)
        inv = 1.0 / jnp.maximum(jnp.sum(p, axis=1, keepdims=True), 1e-30)
        pv = _dot(p.astype(BF16), vca) * inv
        o_c.append(pv[:, :HEAD_DIM])
        imp = imp + pv

    gates = gate_ref[0]
    lane = lax.broadcasted_iota(jnp.int32, (TQ, LANES), 1)
    partial, g_sel = [], []
    for r in range(Q_PER_KV):
        col = g * (Q_PER_KV * N_BRANCH) + r * N_BRANCH
        gc, gs, gw = (jnp.sum(jnp.where(lane == col + br, gates, 0.0), axis=1, keepdims=True)
                      for br in range(N_BRANCH))
        a = acc_w[rows[r]]
        o_w = a[:, :HEAD_DIM] / a[:, HEAD_DIM:HEAD_DIM + 1]
        partial.append(gc * o_c[r] + gw * o_w)
        g_sel.append(jnp.broadcast_to(gs, (TQ, HEAD_DIM)))

    imp_t = imp.T[HEAD_DIM:, :]
    j_t = lax.broadcasted_iota(jnp.int32, (LANES - HEAD_DIM, TQ), 0)
    cur_t = (t0 + lax.broadcasted_iota(jnp.int32, (LANES - HEAD_DIM, TQ), 1)) // SEL_BLOCK
    forced = (j_t == 0) | (j_t == cur_t) | (j_t == cur_t - 1)
    cand = jnp.where(j_t <= cur_t, imp_t + jnp.where(forced, FORCED_BONUS, 0.0), -jnp.inf)
    sel = jnp.zeros(cand.shape, jnp.bool_)
    for _ in range(SEL_TOPK):
        mx = jnp.max(cand, axis=0, keepdims=True)
        first = jnp.min(jnp.where(cand == mx, j_t, LANES), axis=0, keepdims=True)
        hit = j_t == first
        sel = sel | (hit & (mx > -jnp.inf))
        cand = jnp.where(hit, -jnp.inf, cand)
    bias_t = jnp.concatenate([jnp.zeros(cand.shape, F32), jnp.where(sel, 0.0, MASK_NEG)], axis=0)
    sel_bias = bias_t.T
    sel_bias_bf = sel_bias.astype(BF16)
    for r in range(Q_PER_KV):
        qa_ref[rows[r], :] = q_full[rows[r]] + sel_bias_bf

    lane_row = lax.broadcasted_iota(jnp.int32, (1, TQ), 1)
    causal = (lax.broadcasted_iota(jnp.int32, (TQ, TQ), 1)
              <= lax.broadcasted_iota(jnp.int32, (TQ, TQ), 0))
    mx_ref[...] = jnp.full(mx_ref.shape, M_INIT, F32)

    def raw_scores(chunk, dst_ref, k, diagonal):
        k0 = pl.multiple_of(chunk * TQ, TQ)
        s_all = _dot_nt(qa_ref[...], ksa_ref[0, 0, pl.ds(k0, TQ), :])
        for r in range(Q_PER_KV):
            sr = s_all[rows[r]]
            dst_ref[k, rows[r], :] = jnp.where(causal, sr, MASK_NEG) if diagonal else sr

    raw_scores(jnp.where(i > 0, 0, 1), se_ref, 0, False)

    blk_any = jnp.broadcast_to(jnp.max(sel_bias, axis=0, keepdims=True), (8, LANES))
    per_chunk = TQ // SEL_BLOCK
    n_chunks = list_ref.shape[0] - 2
    chunk_any = blk_any
    for d in range(1, per_chunk):
        chunk_any = jnp.maximum(chunk_any, pltpu.roll(blk_any, LANES - d, 1))
    blk8 = lax.broadcasted_iota(jnp.int32, (8, LANES), 1) - HEAD_DIM
    chunk_bit = jnp.where((blk8 >= 0) & (blk8 % per_chunk == 0),
                          jnp.left_shift(1, jnp.maximum(blk8, 0) // per_chunk), 0).astype(F32)
    flagged = jnp.where(chunk_any > 0.5 * MASK_NEG, chunk_bit, 0.0)
    bits = jnp.sum(flagged[0:1, :], axis=1, keepdims=True)[0, 0].astype(jnp.int32)
    cnt = jnp.int32(0)
    idle = jnp.int32(0)
    for c in range(n_chunks):
        below = c < i
        hit = (((bits >> c) & 1) == 1) & below
        list_ref[cnt] = jnp.int32(c)
        cnt = cnt + hit.astype(jnp.int32)
        idle = jnp.where(below & jnp.logical_not(hit), c, idle)
    list_ref[cnt] = jnp.where(i + 1 < n_chunks, i + 1, idle)
    list_ref[cnt + 1 - (cnt & 1)] = i
    n_pairs = (cnt + 2) // 2
    k_last = n_pairs - 1

    def alibi_row(pos, r):
        return slopes[r] * (list_ref[pos] * TQ - t0 + lane_row).astype(F32)

    def fold_max(pos, src_ref, k):
        for r in range(Q_PER_KV):
            sr = src_ref[k, rows[r], :] + alibi_row(pos, r)
            part = sr[:, halves[0]]
            for hf in halves[1:]:
                part = jnp.maximum(part, sr[:, hf])
            mx_ref[rows[r], :] = jnp.maximum(mx_ref[rows[r], :], part)

    def pass1(k, carry):
        raw_scores(list_ref[2 * k + 1], so_ref, k, False)
        fold_max(2 * k, se_ref, k)
        raw_scores(list_ref[2 * k + 2], se_ref, k + 1, False)
        fold_max(2 * k + 1, so_ref, k)
        return carry

    lax.fori_loop(0, k_last, pass1, 0)
    raw_scores(i, so_ref, k_last, True)
    fold_max(2 * k_last, se_ref, k_last)
    fold_max(2 * k_last + 1, so_ref, k_last)

    for r in range(Q_PER_KV):
        m = jnp.max(mx_ref[rows[r], :], axis=1, keepdims=True)
        mx_ref[rows[r], :] = jnp.broadcast_to(m, (TQ, LANES))
    acc_ref[...] = jnp.zeros(acc_ref.shape, F32)

    def probs(pos, src_ref, k, dst_ref):
        for r in range(Q_PER_KV):
            mb = mx_ref[rows[r], :]
            bias = alibi_row(pos, r)
            for hf in halves:
                dst_ref[rows[r], hf] = jnp.exp2(src_ref[k, rows[r], hf] + bias[:, hf] - mb).astype(BF16)

    def add_pv(pos, src_ref):
        k0 = pl.multiple_of(list_ref[pos] * TQ, TQ)
        acc_ref[...] += _dot(src_ref[...], vsa_ref[0, 0, pl.ds(k0, TQ), :])

    probs(0, se_ref, 0, pa_ref)

    def pass2(k, carry):
        add_pv(2 * k, pa_ref)
        probs(2 * k + 1, so_ref, k, pb_ref)
        add_pv(2 * k + 1, pb_ref)
        probs(2 * k + 2, se_ref, k + 1, pa_ref)
        return carry

    lax.fori_loop(0, k_last, pass2, 0)
    add_pv(2 * k_last, pa_ref)
    probs(2 * k_last + 1, so_ref, k_last, pb_ref)
    add_pv(2 * k_last + 1, pb_ref)
    outs = []
    for r in range(Q_PER_KV):
        a = acc_ref[rows[r], :]
        o_s = a[:, :HEAD_DIM] / a[:, HEAD_DIM:HEAD_DIM + 1]
        outs.append(partial[r] + g_sel[r] * o_s)
    o_ref[0, 0] = jnp.concatenate(outs, axis=1).astype(o_ref.dtype)


def _nsa(q, kc, vca, ksa, vsa, kw, vwa, gates, slopes):
    b, _, t, _ = q.shape
    ncp = kc.shape[2]
    grp = lambda bi, g, i: (bi, g, 0, 0)
    return pl.pallas_call(
        _nsa_kernel,
        grid=(b, N_KV_HEADS, t // TQ),
        in_specs=[
            pl.BlockSpec(memory_space=pltpu.SMEM),
            pl.BlockSpec((1, Q_PER_KV, TQ, 2 * HEAD_DIM), lambda bi, g, i: (bi, g, i, 0)),
            pl.BlockSpec((1, 1, ncp, HEAD_DIM), grp),
            pl.BlockSpec((1, 1, ncp, 2 * HEAD_DIM), grp),
            pl.BlockSpec((1, 1, t, 2 * HEAD_DIM), grp),
            pl.BlockSpec((1, 1, t, 2 * HEAD_DIM), grp),
            pl.BlockSpec((1, 1, t, HEAD_DIM), grp),
            pl.BlockSpec((1, 1, t, 2 * HEAD_DIM), grp),
            pl.BlockSpec((1, TQ, GATE_PAD), lambda bi, g, i: (bi, i, 0)),
        ],
        out_specs=pl.BlockSpec((1, 1, TQ, Q_PER_KV * HEAD_DIM), lambda bi, g, i: (bi, g, i, 0)),
        out_shape=jax.ShapeDtypeStruct((b, N_KV_HEADS, t, Q_PER_KV * HEAD_DIM), BF16),
        scratch_shapes=[
            pltpu.VMEM((Q_PER_KV * TQ, 2 * HEAD_DIM), BF16),
            pltpu.VMEM((t // (2 * TQ), Q_PER_KV * TQ, TQ), F32),
            pltpu.VMEM((t // (2 * TQ), Q_PER_KV * TQ, TQ), F32),
            pltpu.VMEM((Q_PER_KV * TQ, LANES), F32),
            pltpu.VMEM((Q_PER_KV * TQ, TQ), BF16),
            pltpu.VMEM((Q_PER_KV * TQ, TQ), BF16),
            pltpu.VMEM((Q_PER_KV * TQ, LANES), F32),
            pltpu.VMEM((Q_PER_KV * TQ, WIN_KEYS), BF16),
            pltpu.SMEM((t // TQ + 2,), jnp.int32),
        ],
        compiler_params=pltpu.CompilerParams(
            dimension_semantics=("arbitrary", "arbitrary", "arbitrary"),
            vmem_limit_bytes=VMEM_LIMIT),
        name="nsa_attention",
    )(slopes, q, kc, vca, ksa, vsa, kw, vwa, gates)


def _outffn_kernel(x_ref, c_ref, n_ref, woc_ref, won_ref, fg_ref, wg_ref, wu_ref, wd_ref, fin_ref,
                   o_ref, acc_ref, h_ref, *, final):
    x1 = x_ref[0] + _dot(c_ref[0], woc_ref[...])
    for g in range(N_KV_HEADS):
        x1 = x1 + _dot(n_ref[0, g], won_ref[g])
    ms = jnp.mean(x1 * x1, axis=-1, keepdims=True)
    h_ref[...] = (x1 * lax.rsqrt(ms + EPS) * fg_ref[...]).astype(BF16)
    acc_ref[...] = x1

    def body(c, carry):
        h = h_ref[...]
        gate = _dot(h, wg_ref[c])
        up = _dot(h, wu_ref[c])
        act = (gate * _sigmoid(gate) * up).astype(BF16)
        acc_ref[...] += _dot(act, wd_ref[c])
        return carry

    lax.fori_loop(0, N_FF_CHUNKS, body, 0)
    y = acc_ref[...]
    if final:
        ms = jnp.mean(y * y, axis=-1, keepdims=True)
        y = y * lax.rsqrt(ms + EPS) * fin_ref[...]
    o_ref[0] = y


def _outffn(x, conv_out, nsa_out, wo_c, wo_n, ffn_g, wg, wu, wd, final_g, final):
    b, t, _ = x.shape
    tm = min(TM_PROJ, t)
    tok = lambda bi, i: (bi, i, 0)
    return pl.pallas_call(
        functools.partial(_outffn_kernel, final=final),
        grid=(b, t // tm),
        in_specs=[
            pl.BlockSpec((1, tm, D_MODEL), tok),
            pl.BlockSpec((1, tm, CONV_CH), tok),
            pl.BlockSpec((1, N_KV_HEADS, tm, Q_PER_KV * HEAD_DIM), lambda bi, i: (bi, 0, i, 0)),
            _const_spec((CONV_CH, D_MODEL)),
            _const_spec((N_KV_HEADS, Q_PER_KV * HEAD_DIM, D_MODEL)),
            _const_spec((1, D_MODEL)),
            _const_spec((N_FF_CHUNKS, D_MODEL, FF_CHUNK)),
            _const_spec((N_FF_CHUNKS, D_MODEL, FF_CHUNK)),
            _const_spec((N_FF_CHUNKS, FF_CHUNK, D_MODEL)),
            _const_spec((1, D_MODEL)),
        ],
        out_specs=pl.BlockSpec((1, tm, D_MODEL), tok),
        out_shape=jax.ShapeDtypeStruct((b, t, D_MODEL), F32),
        scratch_shapes=[pltpu.VMEM((tm, D_MODEL), F32), pltpu.VMEM((tm, D_MODEL), BF16)],
        compiler_params=pltpu.CompilerParams(
            dimension_semantics=("arbitrary", "arbitrary"), vmem_limit_bytes=VMEM_LIMIT),
        name="outproj_ffn",
    )(x, conv_out, nsa_out, wo_c, wo_n, ffn_g, wg, wu, wd, final_g)


def _pe_rows(pe):
    lo = pe[:CMP_STRIDE].reshape(1, -1)
    hi = pe[CMP_STRIDE:].reshape(1, -1)
    z = jnp.zeros((7, lo.shape[1]), pe.dtype)
    return jnp.concatenate([lo, z, hi, z], axis=0).astype(BF16)


def _w1_cat(w1):
    lo = w1[:CMP_STRIDE].reshape(CMP_STRIDE * HEAD_DIM, CMP_HIDDEN)
    hi = w1[CMP_STRIDE:].reshape(CMP_STRIDE * HEAD_DIM, CMP_HIDDEN)
    return jnp.concatenate([lo, hi], axis=1).astype(BF16)


def kernel(x, attn_norm, w_in, conv_w, conv_b, conv_ln_g, conv_ln_b, cmp_k_pe, cmp_k_w1, cmp_k_w2,
           cmp_v_pe, cmp_v_w1, cmp_v_w2, w_out, ffn_norm, w_gate_up, w_down, final_norm):
    depth = w_in.shape[0]
    t = x.shape[1]
    assert t % (2 * TQ) == 0 and t // SEL_BLOCK <= LANES - HEAD_DIM and t >= WIN_KEYS
    slopes =jnp.asarray(_alibi_slopes(N_Q_HEADS) * np.float32(LOG2E))
    final_g = final_norm.reshape(1, D_MODEL)
    for l in range(depth):
        w_pad = jnp.pad(w_in[l], ((0, 0), (0, IN_COLS_PAD - w_in.shape[2]))).astype(BF16)
        a, q, kc_r, vc_r, ksa, vsa, kw, vwa, gates = _inproj(
            x, attn_norm[l].reshape(1, D_MODEL), w_pad)
        conv_out = _conv_mixer(
            a, jnp.pad(conv_w[l], ((0, CONV_HALO - CONV_WIDTH), (0, 0))),
            conv_b[l].reshape(1, CONV_CH), conv_ln_g[l].reshape(1, CONV_CH),
            conv_ln_b[l].reshape(1, CONV_CH))
        kc, vca = _compress(kc_r, vc_r,
                           _pe_rows(cmp_k_pe[l]), _w1_cat(cmp_k_w1[l]), cmp_k_w2[l].astype(BF16),
                           _pe_rows(cmp_v_pe[l]), _w1_cat(cmp_v_w1[l]), cmp_v_w2[l].astype(BF16))
        nsa_out = _nsa(q, kc, vca, ksa, vsa, kw, vwa, gates, slopes)
        wo = w_out[l].astype(BF16)
        wo_n = wo[CONV_CH:].reshape(N_KV_HEADS, Q_PER_KV * HEAD_DIM, D_MODEL)
        wgu = w_gate_up[l].astype(BF16)
        wg = wgu[:, :D_FF].reshape(D_MODEL, N_FF_CHUNKS, FF_CHUNK).transpose(1, 0, 2)
        wu = wgu[:, D_FF:].reshape(D_MODEL, N_FF_CHUNKS, FF_CHUNK).transpose(1, 0, 2)
        wd = w_down[l].astype(BF16).reshape(N_FF_CHUNKS, FF_CHUNK, D_MODEL)
        x = _outffn(x, conv_out, nsa_out, wo[:CONV_CH], wo_n, ffn_norm[l].reshape(1, D_MODEL),
                    wg, wu, wd, final_g, final=(l == depth - 1))
    return x
```

```python
import functools
import math

import jax
import jax.numpy as jnp
import numpy as np
from jax import lax
from jax.experimental import pallas as pl
from jax.experimental.pallas import tpu as pltpu

F32 = jnp.float32
BF16 = jnp.bfloat16

D_MODEL = 1024
HEAD_DIM = 64
CONV_CH = 256
CONV_WIDTH = 31
N_Q_HEADS = 12
N_KV_HEADS = 4
Q_PER_KV = N_Q_HEADS // N_KV_HEADS
NSA_WIDTH = N_Q_HEADS * HEAD_DIM
KV_WIDTH = N_KV_HEADS * HEAD_DIM
CMP_LEN = 32
CMP_STRIDE = 16
CMP_HIDDEN = 256
SEL_BLOCK = 64
SEL_TOPK = 8
WINDOW = 512
N_BRANCH = 3
FORCED_BONUS = 1000.0
D_FF = 2816
EPS = 1e-6
LOG2E = math.log2(math.e)

LANES = 128
SUBLANES = 8
GATE_PAD = LANES
FF_CHUNK = 256
N_FF_CHUNKS = D_FF // FF_CHUNK
TM_PROJ = 512
TC_CONV = 512
CONV_HALO = 32
TQ = 256
WIN_KEYS = WINDOW + TQ
MASK_NEG = -(2.0 ** 100)
M_INIT = -1e30
VMEM_LIMIT = 56 * 1024 * 1024


def _alibi_slopes(n):
    def pow2_slopes(m):
        start = 2.0 ** (-8.0 / m)
        return [start ** (i + 1) for i in range(m)]
    if math.log2(n).is_integer():
        s = pow2_slopes(n)
    else:
        c = 2 ** math.floor(math.log2(n))
        s = pow2_slopes(c) + pow2_slopes(2 * c)[0::2][: n - c]
    return np.asarray(s, dtype=np.float32)


def _sigmoid(v):
    return 1.0 / (1.0 + jnp.exp(-v))


def _dot(a, b):
    return jnp.dot(a, b, preferred_element_type=F32)


def _dot_nt(a, b):
    return lax.dot_general(a, b, (((1,), (1,)), ((), ())), preferred_element_type=F32)


def _const_spec(shape):
    nd = len(shape)
    return pl.BlockSpec(shape, lambda *_: (0,) * nd, pipeline_mode=pl.Buffered(1))


C_A = 0
C_Q = C_A + 2 * CONV_CH
C_KC = C_Q + NSA_WIDTH
C_VC = C_KC + KV_WIDTH
C_KS = C_VC + KV_WIDTH
C_VS = C_KS + KV_WIDTH
C_KW = C_VS + KV_WIDTH
C_VW = C_KW + KV_WIDTH
C_G = C_VW + KV_WIDTH
IN_COLS_PAD = C_G + GATE_PAD


def _inproj_kernel(x_ref, g_ref, w_ref, a_ref, q_ref, kcr_ref, vcr_ref, ksa_ref, vsa_ref, kw_ref,
                   vwa_ref, gate_ref):
    tm = x_ref.shape[1]
    x = x_ref[0]
    ms = jnp.mean(x * x, axis=-1, keepdims=True)
    h = (x * lax.rsqrt(ms + EPS) * g_ref[...]).astype(BF16)

    a_ref[0] = _dot(h, w_ref[:, C_A:C_Q])
    zq = _dot(h, w_ref[:, C_Q:C_KC]) * (HEAD_DIM ** -0.5 * LOG2E)
    zero_hi = jnp.zeros((tm, HEAD_DIM), BF16)
    for hd in range(N_Q_HEADS):
        qh = zq[:, hd * HEAD_DIM:(hd + 1) * HEAD_DIM].astype(BF16)
        q_ref[0, hd] = jnp.concatenate([qh, zero_hi], axis=1)
    zkc = _dot(h, w_ref[:, C_KC:C_VC])
    zvc = _dot(h, w_ref[:, C_VC:C_KS])
    for half in range(KV_WIDTH // LANES):
        kcr_ref[0, half] = zkc[:, half * LANES:(half + 1) * LANES]
        vcr_ref[0, half] = zvc[:, half * LANES:(half + 1) * LANES]

    t = pl.program_id(1) * tm + lax.broadcasted_iota(jnp.int32, (tm, HEAD_DIM), 0)
    blk = lax.broadcasted_iota(jnp.int32, (tm, HEAD_DIM), 1)
    onehot = jnp.where(t // SEL_BLOCK == blk, 1.0, 0.0).astype(BF16)
    ones_col = jnp.where(blk == 0, 1.0, 0.0).astype(BF16)
    zks = _dot(h, w_ref[:, C_KS:C_VS])
    zvs = _dot(h, w_ref[:, C_VS:C_KW])
    zkw = _dot(h, w_ref[:, C_KW:C_VW])
    zvw = _dot(h, w_ref[:, C_VW:C_G])
    for g in range(N_KV_HEADS):
        sl = slice(g * HEAD_DIM, (g + 1) * HEAD_DIM)
        ksa_ref[0, g] = jnp.concatenate([zks[:, sl].astype(BF16), onehot], axis=1)
        vsa_ref[0, g] = jnp.concatenate([zvs[:, sl].astype(BF16), ones_col], axis=1)
        kw_ref[0, g] = zkw[:, sl].astype(BF16)
        vwa_ref[0, g] = jnp.concatenate([zvw[:, sl].astype(BF16), ones_col], axis=1)
    gate_ref[0] = _sigmoid(_dot(h, w_ref[:, C_G:IN_COLS_PAD]))


def _inproj(x, norm_g, w_pad):
    b, t, _ = x.shape
    tm = min(TM_PROJ, t)
    grid = (b, t // tm)
    tok = lambda bi, i: (bi, i, 0)
    head = lambda bi, i: (bi, 0, i, 0)
    out_shape = (
        jax.ShapeDtypeStruct((b, t, 2 * CONV_CH), F32),
        jax.ShapeDtypeStruct((b, N_Q_HEADS, t, 2 * HEAD_DIM), BF16),
        jax.ShapeDtypeStruct((b, KV_WIDTH // LANES, t, LANES), F32),
        jax.ShapeDtypeStruct((b, KV_WIDTH // LANES, t, LANES), F32),
        jax.ShapeDtypeStruct((b, N_KV_HEADS, t, 2 * HEAD_DIM), BF16),
        jax.ShapeDtypeStruct((b, N_KV_HEADS, t, 2 * HEAD_DIM), BF16),
        jax.ShapeDtypeStruct((b, N_KV_HEADS, t, HEAD_DIM), BF16),
        jax.ShapeDtypeStruct((b, N_KV_HEADS, t, 2 * HEAD_DIM), BF16),
        jax.ShapeDtypeStruct((b, t, GATE_PAD), F32),
    )
    out_specs = (
        pl.BlockSpec((1, tm, 2 * CONV_CH), tok),
        pl.BlockSpec((1, N_Q_HEADS, tm, 2 * HEAD_DIM), head),
        pl.BlockSpec((1, KV_WIDTH // LANES, tm, LANES), head),
        pl.BlockSpec((1, KV_WIDTH // LANES, tm, LANES), head),
        pl.BlockSpec((1, N_KV_HEADS, tm, 2 * HEAD_DIM), head),
        pl.BlockSpec((1, N_KV_HEADS, tm, 2 * HEAD_DIM), head),
        pl.BlockSpec((1, N_KV_HEADS, tm, HEAD_DIM), head),
        pl.BlockSpec((1, N_KV_HEADS, tm, 2 * HEAD_DIM), head),
        pl.BlockSpec((1, tm, GATE_PAD), tok),
    )
    return pl.pallas_call(
        _inproj_kernel,
        grid=grid,
        in_specs=[
            pl.BlockSpec((1, tm, D_MODEL), tok),
            _const_spec((1, D_MODEL)),
            _const_spec((D_MODEL, IN_COLS_PAD)),
        ],
        out_specs=out_specs,
        out_shape=out_shape,
        compiler_params=pltpu.CompilerParams(
            dimension_semantics=("arbitrary", "arbitrary"), vmem_limit_bytes=VMEM_LIMIT),
        name="inproj",
    )(x, norm_g, w_pad)


def _conv_kernel(a_ref, halo_ref, w_ref, b_ref, lg_ref, lb_ref, o_ref, y_ref):
    tc = a_ref.shape[1]
    am = a_ref[0]
    ah = halo_ref[0]
    ym = am[:, :CONV_CH] * _sigmoid(am[:, CONV_CH:])
    yh = ah[:, :CONV_CH] * _sigmoid(ah[:, CONV_CH:])
    yh = jnp.where(pl.program_id(1) == 0, 0.0, yh)
    y_ref[0:CONV_HALO, :] = yh
    y_ref[CONV_HALO:CONV_HALO + tc, :] = ym
    first = CONV_HALO - (CONV_WIDTH - 1)
    acc = b_ref[...]
    for rho in range(SUBLANES):
        n_rows = tc if rho == 0 else tc + SUBLANES
        z = None
        for base in range(0, CONV_HALO + 1, SUBLANES):
            k = base + rho - first
            if 0 <= k < CONV_WIDTH:
                term = w_ref[k:k + 1, :] * y_ref[pl.ds(base, n_rows), :]
                z = term if z is None else z + term
        acc = acc + (z if rho == 0 else z[rho:rho + tc, :])
    mu = jnp.mean(acc, axis=-1, keepdims=True)
    d = acc - mu
    var = jnp.mean(d * d, axis=-1, keepdims=True)
    yn = d * lax.rsqrt(var + EPS) * lg_ref[...] + lb_ref[...]
    o_ref[0] = (yn * _sigmoid(yn)).astype(o_ref.dtype)


def _conv_mixer(a, w_dw, b_dw, ln_g, ln_b):
    b, t, _ = a.shape
    tc = min(TC_CONV, t)
    per = tc // CONV_HALO
    return pl.pallas_call(
        _conv_kernel,
        grid=(b, t // tc),
        in_specs=[
            pl.BlockSpec((1, tc, 2 * CONV_CH), lambda bi, i: (bi, i, 0)),
            pl.BlockSpec((1, CONV_HALO, 2 * CONV_CH),
                         lambda bi, i: (bi, jnp.maximum(i * per - 1, 0), 0)),
            _const_spec((CONV_HALO, CONV_CH)),
            _const_spec((1, CONV_CH)),
            _const_spec((1, CONV_CH)),
            _const_spec((1, CONV_CH)),
        ],
        out_specs=pl.BlockSpec((1, tc, CONV_CH), lambda bi, i: (bi, i, 0)),
        out_shape=jax.ShapeDtypeStruct((b, t, CONV_CH), BF16),
        scratch_shapes=[pltpu.VMEM((CONV_HALO + tc, CONV_CH), F32)],
        compiler_params=pltpu.CompilerParams(
            dimension_semantics=("arbitrary", "arbitrary"), vmem_limit_bytes=VMEM_LIMIT),
        name="conv_mixer",
    )(a, a, w_dw, b_dw, ln_g, ln_b)


def _compress_kernel(kr_ref, vr_ref, kpe_ref, kw1_ref, kw2_ref, vpe_ref, vw1_ref, vw2_ref,
                     kc_ref, vc_ref):
    nch = kr_ref.shape[2] // CMP_STRIDE
    row = lax.broadcasted_iota(jnp.int32, (nch, HEAD_DIM), 0)
    c_start = row * CMP_STRIDE
    s_start = lax.broadcasted_iota(jnp.int32, (nch, HEAD_DIM), 1) * SEL_BLOCK
    overlap = jnp.where((c_start < s_start + SEL_BLOCK) & (c_start + CMP_LEN > s_start),
                        1.0, 0.0).astype(BF16)
    for r_ref, pe_ref, w1_ref, w2_ref, o_ref, tail in (
            (kr_ref, kpe_ref, kw1_ref, kw2_ref, kc_ref, None),
            (vr_ref, vpe_ref, vw1_ref, vw2_ref, vc_ref, overlap)):
        pb = _dot(pe_ref[...], w1_ref[...])
        bias = pb[0:1, :CMP_HIDDEN] + pb[8:9, CMP_HIDDEN:]
        rows = [[r_ref[0, half, pl.ds(l, nch, stride=CMP_STRIDE), :] for l in range(CMP_STRIDE)]
                for half in range(KV_WIDTH // LANES)]
        for g in range(N_KV_HEADS):
            half, odd = divmod(g, LANES // HEAD_DIM)
            sl = slice(odd * HEAD_DIM, (odd + 1) * HEAD_DIM)
            xg = jnp.concatenate([rw[:, sl] for rw in rows[half]], axis=1).astype(BF16)
            lohi = _dot(xg, w1_ref[...])
            lo = lohi[:, :CMP_HIDDEN]
            hi = lohi[:, CMP_HIDDEN:]
            hi_next = jnp.concatenate([hi[1:], jnp.zeros((1, CMP_HIDDEN), F32)], axis=0)
            hid = lo + hi_next + bias
            act = (hid * _sigmoid(hid)).astype(BF16)
            out = _dot(act, w2_ref[...])
            out = jnp.where(row < nch - 1, out, 0.0)
            out = out.astype(o_ref.dtype)
            o_ref[0, g] = out if tail is None else jnp.concatenate([out, tail], axis=1)


def _compress(kc_r, vc_r, kpe, kw1, kw2, vpe, vw1, vw2):
    b, _, t, _ = kc_r.shape
    nch = t // CMP_STRIDE
    blk_len = CMP_STRIDE * HEAD_DIM
    out_sds = [jax.ShapeDtypeStruct((b, N_KV_HEADS, nch, w), BF16) for w in (HEAD_DIM, 2 * HEAD_DIM)]
    out_spec = [pl.BlockSpec((1, N_KV_HEADS, nch, w), lambda bi: (bi, 0, 0, 0))
                for w in (HEAD_DIM, 2 * HEAD_DIM)]
    raw_spec = pl.BlockSpec((1, KV_WIDTH // LANES, t, LANES), lambda bi: (bi, 0, 0, 0))
    w_specs = [_const_spec((16, blk_len)), _const_spec((blk_len, 2 * CMP_HIDDEN)),
               _const_spec((CMP_HIDDEN, HEAD_DIM))]
    return pl.pallas_call(
        _compress_kernel,
        grid=(b,),
        in_specs=[raw_spec, raw_spec] + w_specs + w_specs,
        out_specs=tuple(out_spec),
        out_shape=tuple(out_sds),
        compiler_params=pltpu.CompilerParams(
            dimension_semantics=("arbitrary",), vmem_limit_bytes=VMEM_LIMIT),
        name="compress",
    )(kc_r, vc_r, kpe, kw1, kw2, vpe, vw1, vw2)


def _nsa_kernel(slope_ref, q_ref, kc_ref, vca_ref, ksa_ref, vsa_ref, kw_ref, vwa_ref, gate_ref,
                o_ref, qa_ref, se_ref, so_ref, mx_ref, pa_ref, pb_ref, acc_ref, pw_ref, list_ref):
    g = pl.program_id(1)
    i = pl.program_id(2)
    t0 = i * TQ
    ncp = kc_ref.shape[2]
    slopes = [slope_ref[g * Q_PER_KV + r] for r in range(Q_PER_KV)]
    rows = [slice(r * TQ, (r + 1) * TQ) for r in range(Q_PER_KV)]
    halves = [slice(hf * LANES, (hf + 1) * LANES) for hf in range(TQ // LANES)]

    q_full = q_ref[0].reshape(Q_PER_KV * TQ, 2 * HEAD_DIM)
    q3 = q_full[:, :HEAD_DIM]

    w0 = pl.multiple_of(jnp.maximum(t0 - WINDOW, 0), TQ)
    s_w = _dot_nt(q3, kw_ref[0, 0, pl.ds(w0, WIN_KEYS), :])
    wk_row = w0 - t0 + lax.broadcasted_iota(jnp.int32, (1, WIN_KEYS), 1)
    dist = (lax.broadcasted_iota(jnp.int32, (TQ, WIN_KEYS), 0)
            - (w0 - t0) - lax.broadcasted_iota(jnp.int32, (TQ, WIN_KEYS), 1))
    w_valid = (dist >= 0) & (dist < WINDOW)
    wk_pos = wk_row.astype(F32)
    for r in range(Q_PER_KV):
        sr = jnp.where(w_valid, s_w[rows[r]] + slopes[r] * wk_pos, MASK_NEG)
        m = jnp.max(sr, axis=1, keepdims=True)
        pw_ref[rows[r], :] = jnp.exp2(sr - m).astype(BF16)
    acc_w = _dot(pw_ref[...], vwa_ref[0, 0, pl.ds(w0, WIN_KEYS), :])

    s_c = _dot_nt(q3, kc_ref[0, 0])
    n_idx = lax.broadcasted_iota(jnp.int32, (TQ, ncp), 1)
    t_idx = t0 + lax.broadcasted_iota(jnp.int32, (TQ, ncp), 0)
    c_valid = n_idx * CMP_STRIDE + (CMP_LEN - 1) <= t_idx
    n_row = lax.broadcasted_iota(jnp.int32, (1, ncp), 1)
    c_pos = (n_row * CMP_STRIDE - t0).astype(F32) + 0.5 * (CMP_LEN - 1)
    vca = vca_ref[0, 0]
    o_c = []
    imp = jnp.zeros((TQ, LANES), F32)
    for r in range(Q_PER_KV):
        sr = jnp.where(c_valid, s_c[rows[r]] + slopes[r] * c_pos, -jnp.inf)
        m = jnp.max(sr, axis=1, keepdims=True)
        m = jnp.where(m == -jnp.inf, 0.0, m)
        p = jnp.exp2(sr - m)
        inv = 1.0 / jnp.maximum(jnp.sum(p, axis=1, keepdims=True), 1e-30)
        pv = _dot(p.astype(BF16), vca) * inv
        o_c.append(pv[:, :HEAD_DIM])
        imp = imp + pv

    gates = gate_ref[0]
    lane = lax.broadcasted_iota(jnp.int32, (TQ, LANES), 1)
    partial, g_sel = [], []
    for r in range(Q_PER_KV):
        col = g * (Q_PER_KV * N_BRANCH) + r * N_BRANCH
        gc, gs, gw = (jnp.sum(jnp.where(lane == col + br, gates, 0.0), axis=1, keepdims=True)
                      for br in range(N_BRANCH))
        a = acc_w[rows[r]]
        o_w = a[:, :HEAD_DIM] / a[:, HEAD_DIM:HEAD_DIM + 1]
        partial.append(gc * o_c[r] + gw * o_w)
        g_sel.append(jnp.broadcast_to(gs, (TQ, HEAD_DIM)))

    imp_t = imp.T[HEAD_DIM:, :]
    j_t = lax.broadcasted_iota(jnp.int32, (LANES - HEAD_DIM, TQ), 0)
    cur_t = (t0 + lax.broadcasted_iota(jnp.int32, (LANES - HEAD_DIM, TQ), 1)) // SEL_BLOCK
    forced = (j_t == 0) | (j_t == cur_t) | (j_t == cur_t - 1)
    cand = jnp.where(j_t <= cur_t, imp_t + jnp.where(forced, FORCED_BONUS, 0.0), -jnp.inf)
    sel = jnp.zeros(cand.shape, jnp.bool_)
    for _ in range(SEL_TOPK):
        mx = jnp.max(cand, axis=0, keepdims=True)
        first = jnp.min(jnp.where(cand == mx, j_t, LANES), axis=0, keepdims=True)
        hit = j_t == first
        sel = sel | (hit & (mx > -jnp.inf))
        cand = jnp.where(hit, -jnp.inf, cand)
    bias_t = jnp.concatenate([jnp.zeros(cand.shape, F32), jnp.where(sel, 0.0, MASK_NEG)], axis=0)
    sel_bias = bias_t.T
    sel_bias_bf = sel_bias.astype(BF16)
    for r in range(Q_PER_KV):
        qa_ref[rows[r], :] = q_full[rows[r]] + sel_bias_bf

    lane_row = lax.broadcasted_iota(jnp.int32, (1, TQ), 1)
    causal = (lax.broadcasted_iota(jnp.int32, (TQ, TQ), 1)
              <= lax.broadcasted_iota(jnp.int32, (TQ, TQ), 0))
    mx_ref[...] = jnp.full(mx_ref.shape, M_INIT, F32)

    def raw_scores(chunk, dst_ref, k, diagonal):
        k0 = pl.multiple_of(chunk * TQ, TQ)
        s_all = _dot_nt(qa_ref[...], ksa_ref[0, 0, pl.ds(k0, TQ), :])
        for r in range(Q_PER_KV):
            sr = s_all[rows[r]]
            dst_ref[k, rows[r], :] = jnp.where(causal, sr, MASK_NEG) if diagonal else sr

    raw_scores(jnp.where(i > 0, 0, 1), se_ref, 0, False)

    blk_any = jnp.broadcast_to(jnp.max(sel_bias, axis=0, keepdims=True), (8, LANES))
    per_chunk = TQ // SEL_BLOCK
    n_chunks = list_ref.shape[0] - 2
    chunk_any = blk_any
    for d in range(1, per_chunk):
        chunk_any = jnp.maximum(chunk_any, pltpu.roll(blk_any, LANES - d, 1))
    blk8 = lax.broadcasted_iota(jnp.int32, (8, LANES), 1) - HEAD_DIM
    chunk_bit = jnp.where((blk8 >= 0) & (blk8 % per_chunk == 0),
                          jnp.left_shift(1, jnp.maximum(blk8, 0) // per_chunk), 0).astype(F32)
    flagged = jnp.where(chunk_any > 0.5 * MASK_NEG, chunk_bit, 0.0)
    bits = jnp.sum(flagged[0:1, :], axis=1, keepdims=True)[0, 0].astype(jnp.int32)
    cnt = jnp.int32(0)
    idle = jnp.int32(0)
    for c in range(n_chunks):
        below = c < i
        hit = (((bits >> c) & 1) == 1) & below
        list_ref[cnt] = jnp.int32(c)
        cnt = cnt + hit.astype(jnp.int32)
        idle = jnp.where(below & jnp.logical_not(hit), c, idle)
    list_ref[cnt] = jnp.where(i + 1 < n_chunks, i + 1, idle)
    list_ref[cnt + 1 - (cnt & 1)] = i
    n_pairs = (cnt + 2) // 2
    k_last = n_pairs - 1

    def alibi_row(pos, r):
        return slopes[r] * (list_ref[pos] * TQ - t0 + lane_row).astype(F32)

    def fold_max(pos, src_ref, k):
        for r in range(Q_PER_KV):
            sr = src_ref[k, rows[r], :] + alibi_row(pos, r)
            part = sr[:, halves[0]]
            for hf in halves[1:]:
                part = jnp.maximum(part, sr[:, hf])
            mx_ref[rows[r], :] = jnp.maximum(mx_ref[rows[r], :], part)

    def pass1(k, carry):
        raw_scores(list_ref[2 * k + 1], so_ref, k, False)
        fold_max(2 * k, se_ref, k)
        raw_scores(list_ref[2 * k + 2], se_ref, k + 1, False)
        fold_max(2 * k + 1, so_ref, k)
        return carry

    lax.fori_loop(0, k_last, pass1, 0)
    raw_scores(i, so_ref, k_last, True)
    fold_max(2 * k_last, se_ref, k_last)
    fold_max(2 * k_last + 1, so_ref, k_last)

    for r in range(Q_PER_KV):
        m = jnp.max(mx_ref[rows[r], :], axis=1, keepdims=True)
        mx_ref[rows[r], :] = jnp.broadcast_to(m, (TQ, LANES))
    acc_ref[...] = jnp.zeros(acc_ref.shape, F32)

    def probs(pos, src_ref, k, dst_ref):
        for r in range(Q_PER_KV):
            mb = mx_ref[rows[r], :]
            bias = alibi_row(pos, r)
            for hf in halves:
                dst_ref[rows[r], hf] = jnp.exp2(src_ref[k, rows[r], hf] + bias[:, hf] - mb).astype(BF16)

    def add_pv(pos, src_ref):
        k0 = pl.multiple_of(list_ref[pos] * TQ, TQ)
        acc_ref[...] += _dot(src_ref[...], vsa_ref[0, 0, pl.ds(k0, TQ), :])

    probs(0, se_ref, 0, pa_ref)

    def pass2(k, carry):
        add_pv(2 * k, pa_ref)
        probs(2 * k + 1, so_ref, k, pb_ref)
        add_pv(2 * k + 1, pb_ref)
        probs(2 * k + 2, se_ref, k + 1, pa_ref)
        return carry

    lax.fori_loop(0, k_last, pass2, 0)
    add_pv(2 * k_last, pa_ref)
    probs(2 * k_last + 1, so_ref, k_last, pb_ref)
    add_pv(2 * k_last + 1, pb_ref)
    outs = []
    for r in range(Q_PER_KV):
        a = acc_ref[rows[r], :]
        o_s = a[:, :HEAD_DIM] / a[:, HEAD_DIM:HEAD_DIM + 1]
        outs.append(partial[r] + g_sel[r] * o_s)
    o_ref[0, 0] = jnp.concatenate(outs, axis=1).astype(o_ref.dtype)


def _nsa(q, kc, vca, ksa, vsa, kw, vwa, gates, slopes):
    b, _, t, _ = q.shape
    ncp = kc.shape[2]
    grp = lambda bi, g, i: (bi, g, 0, 0)
    return pl.pallas_call(
        _nsa_kernel,
        grid=(b, N_KV_HEADS, t // TQ),
        in_specs=[
            pl.BlockSpec(memory_space=pltpu.SMEM),
            pl.BlockSpec((1, Q_PER_KV, TQ, 2 * HEAD_DIM), lambda bi, g, i: (bi, g, i, 0)),
            pl.BlockSpec((1, 1, ncp, HEAD_DIM), grp),
            pl.BlockSpec((1, 1, ncp, 2 * HEAD_DIM), grp),
            pl.BlockSpec((1, 1, t, 2 * HEAD_DIM), grp),
            pl.BlockSpec((1, 1, t, 2 * HEAD_DIM), grp),
            pl.BlockSpec((1, 1, t, HEAD_DIM), grp),
            pl.BlockSpec((1, 1, t, 2 * HEAD_DIM), grp),
            pl.BlockSpec((1, TQ, GATE_PAD), lambda bi, g, i: (bi, i, 0)),
        ],
        out_specs=pl.BlockSpec((1, 1, TQ, Q_PER_KV * HEAD_DIM), lambda bi, g, i: (bi, g, i, 0)),
        out_shape=jax.ShapeDtypeStruct((b, N_KV_HEADS, t, Q_PER_KV * HEAD_DIM), BF16),
        scratch_shapes=[
            pltpu.VMEM((Q_PER_KV * TQ, 2 * HEAD_DIM), BF16),
            pltpu.VMEM((t // (2 * TQ), Q_PER_KV * TQ, TQ), F32),
            pltpu.VMEM((t // (2 * TQ), Q_PER_KV * TQ, TQ), F32),
            pltpu.VMEM((Q_PER_KV * TQ, LANES), F32),
            pltpu.VMEM((Q_PER_KV * TQ, TQ), BF16),
            pltpu.VMEM((Q_PER_KV * TQ, TQ), BF16),
            pltpu.VMEM((Q_PER_KV * TQ, LANES), F32),
            pltpu.VMEM((Q_PER_KV * TQ, WIN_KEYS), BF16),
            pltpu.SMEM((t // TQ + 2,), jnp.int32),
        ],
        compiler_params=pltpu.CompilerParams(
            dimension_semantics=("arbitrary", "arbitrary", "arbitrary"),
            vmem_limit_bytes=VMEM_LIMIT),
        name="nsa_attention",
    )(slopes, q, kc, vca, ksa, vsa, kw, vwa, gates)


def _outffn_kernel(x_ref, c_ref, n_ref, woc_ref, won_ref, fg_ref, wgu_ref, wd_ref, fin_ref,
                   o_ref, acc_ref, h_ref, act_ref, *, final):
    x1 = x_ref[0] + _dot(c_ref[0], woc_ref[...])
    for g in range(N_KV_HEADS):
        x1 = x1 + _dot(n_ref[0, g], won_ref[g])
    ms = jnp.mean(x1 * x1, axis=-1, keepdims=True)
    h_ref[...] = (x1 * lax.rsqrt(ms + EPS) * fg_ref[...]).astype(BF16)
    acc_ref[...] = x1

    for c in range(N_FF_CHUNKS):
        h = h_ref[...]
        gate = _dot(h, wgu_ref[:, c * FF_CHUNK:(c + 1) * FF_CHUNK])
        up = _dot(h, wgu_ref[:, D_FF + c * FF_CHUNK:D_FF + (c + 1) * FF_CHUNK])
        act_ref[:, c * FF_CHUNK:(c + 1) * FF_CHUNK] = (gate * _sigmoid(gate) * up).astype(BF16)
    y = acc_ref[...] + _dot(act_ref[...], wd_ref[...])
    if final:
        ms = jnp.mean(y * y, axis=-1, keepdims=True)
        y = y * lax.rsqrt(ms + EPS) * fin_ref[...]
    o_ref[0] = y


def _outffn(x, conv_out, nsa_out, wo_c, wo_n, ffn_g, wgu, wd, final_g, final):
    b, t, _ = x.shape
    tm = min(TM_PROJ, t)
    tok = lambda bi, i: (bi, i, 0)
    return pl.pallas_call(
        functools.partial(_outffn_kernel, final=final),
        grid=(b, t // tm),
        in_specs=[
            pl.BlockSpec((1, tm, D_MODEL), tok),
            pl.BlockSpec((1, tm, CONV_CH), tok),
            pl.BlockSpec((1, N_KV_HEADS, tm, Q_PER_KV * HEAD_DIM), lambda bi, i: (bi, 0, i, 0)),
            _const_spec((CONV_CH, D_MODEL)),
            _const_spec((N_KV_HEADS, Q_PER_KV * HEAD_DIM, D_MODEL)),
            _const_spec((1, D_MODEL)),
            _const_spec((D_MODEL, 2 * D_FF)),
            _const_spec((D_FF, D_MODEL)),
            _const_spec((1, D_MODEL)),
        ],
        out_specs=pl.BlockSpec((1, tm, D_MODEL), tok),
        out_shape=jax.ShapeDtypeStruct((b, t, D_MODEL), F32),
        scratch_shapes=[pltpu.VMEM((tm, D_MODEL), F32), pltpu.VMEM((tm, D_MODEL), BF16),
                        pltpu.VMEM((tm, D_FF), BF16)],
        compiler_params=pltpu.CompilerParams(
            dimension_semantics=("arbitrary", "arbitrary"), vmem_limit_bytes=VMEM_LIMIT),
        name="outproj_ffn",
    )(x, conv_out, nsa_out, wo_c, wo_n, ffn_g, wgu, wd, final_g)


def _pe_rows(pe):
    lo = pe[:CMP_STRIDE].reshape(1, -1)
    hi = pe[CMP_STRIDE:].reshape(1, -1)
    z = jnp.zeros((7, lo.shape[1]), pe.dtype)
    return jnp.concatenate([lo, z, hi, z], axis=0).astype(BF16)


def _w1_cat(w1):
    lo = w1[:CMP_STRIDE].reshape(CMP_STRIDE * HEAD_DIM, CMP_HIDDEN)
    hi = w1[CMP_STRIDE:].reshape(CMP_STRIDE * HEAD_DIM, CMP_HIDDEN)
    return jnp.concatenate([lo, hi], axis=1).astype(BF16)


def kernel(x, attn_norm, w_in, conv_w, conv_b, conv_ln_g, conv_ln_b, cmp_k_pe, cmp_k_w1, cmp_k_w2,
           cmp_v_pe, cmp_v_w1, cmp_v_w2, w_out, ffn_norm, w_gate_up, w_down, final_norm):
    depth = w_in.shape[0]
    t = x.shape[1]
    assert t % (2 * TQ) == 0 and t // SEL_BLOCK <= LANES - HEAD_DIM and t >= WIN_KEYS
    slopes =jnp.asarray(_alibi_slopes(N_Q_HEADS) * np.float32(LOG2E))
    final_g = final_norm.reshape(1, D_MODEL)
    for l in range(depth):
        w_pad = jnp.pad(w_in[l], ((0, 0), (0, IN_COLS_PAD - w_in.shape[2]))).astype(BF16)
        a, q, kc_r, vc_r, ksa, vsa, kw, vwa, gates = _inproj(
            x, attn_norm[l].reshape(1, D_MODEL), w_pad)
        conv_out = _conv_mixer(
            a, jnp.pad(conv_w[l], ((0, CONV_HALO - CONV_WIDTH), (0, 0))),
            conv_b[l].reshape(1, CONV_CH), conv_ln_g[l].reshape(1, CONV_CH),
            conv_ln_b[l].reshape(1, CONV_CH))
        kc, vca = _compress(kc_r, vc_r,
                           _pe_rows(cmp_k_pe[l]), _w1_cat(cmp_k_w1[l]), cmp_k_w2[l].astype(BF16),
                           _pe_rows(cmp_v_pe[l]), _w1_cat(cmp_v_w1[l]), cmp_v_w2[l].astype(BF16))
        nsa_out = _nsa(q, kc, vca, ksa, vsa, kw, vwa, gates, slopes)
        wo = w_out[l].astype(BF16)
        wo_n = wo[CONV_CH:].reshape(N_KV_HEADS, Q_PER_KV * HEAD_DIM, D_MODEL)
        x = _outffn(x, conv_out, nsa_out, wo[:CONV_CH], wo_n, ffn_norm[l].reshape(1, D_MODEL),
                    w_gate_up[l].astype(BF16), w_down[l].astype(BF16), final_g,
                    final=(l == depth - 1))
    return x
```

```python
import functools
import math

import jax
import jax.numpy as jnp
import numpy as np
from jax import lax
from jax.experimental import pallas as pl
from jax.experimental.pallas import tpu as pltpu

F32 = jnp.float32
BF16 = jnp.bfloat16

D_MODEL = 1024
HEAD_DIM = 64
CONV_CH = 256
CONV_WIDTH = 31
N_Q_HEADS = 12
N_KV_HEADS = 4
Q_PER_KV = N_Q_HEADS // N_KV_HEADS
NSA_WIDTH = N_Q_HEADS * HEAD_DIM
KV_WIDTH = N_KV_HEADS * HEAD_DIM
CMP_LEN = 32
CMP_STRIDE = 16
CMP_HIDDEN = 256
SEL_BLOCK = 64
SEL_TOPK = 8
WINDOW = 512
N_BRANCH = 3
FORCED_BONUS = 1000.0
D_FF = 2816
EPS = 1e-6
LOG2E = math.log2(math.e)

LANES = 128
SUBLANES = 8
GATE_PAD = LANES
FF_CHUNK = 256
N_FF_CHUNKS = D_FF // FF_CHUNK
TM_PROJ = 512
TC_CONV = 512
CONV_HALO = 32
TQ = 256
WIN_KEYS = WINDOW + TQ
SHORT_LIST = 4
MASK_NEG = -(2.0 ** 100)
M_INIT = -1e30
VMEM_LIMIT = 56 * 1024 * 1024


def _alibi_slopes(n):
    def pow2_slopes(m):
        start = 2.0 ** (-8.0 / m)
        return [start ** (i + 1) for i in range(m)]
    if math.log2(n).is_integer():
        s = pow2_slopes(n)
    else:
        c = 2 ** math.floor(math.log2(n))
        s = pow2_slopes(c) + pow2_slopes(2 * c)[0::2][: n - c]
    return np.asarray(s, dtype=np.float32)


def _sigmoid(v):
    return 1.0 / (1.0 + jnp.exp(-v))


def _dot(a, b):
    return jnp.dot(a, b, preferred_element_type=F32)


def _dot_nt(a, b):
    return lax.dot_general(a, b, (((1,), (1,)), ((), ())), preferred_element_type=F32)


def _const_spec(shape):
    nd = len(shape)
    return pl.BlockSpec(shape, lambda *_: (0,) * nd, pipeline_mode=pl.Buffered(1))


C_A = 0
C_Q = C_A + 2 * CONV_CH
C_KC = C_Q + NSA_WIDTH
C_VC = C_KC + KV_WIDTH
C_KS = C_VC + KV_WIDTH
C_VS = C_KS + KV_WIDTH
C_KW = C_VS + KV_WIDTH
C_VW = C_KW + KV_WIDTH
C_G = C_VW + KV_WIDTH
IN_COLS_PAD = C_G + GATE_PAD


def _inproj_kernel(x_ref, g_ref, w_ref, a_ref, q_ref, kcr_ref, vcr_ref, ksa_ref, vsa_ref, kw_ref,
                   vwa_ref, gate_ref):
    tm = x_ref.shape[1]
    x = x_ref[0]
    ms = jnp.mean(x * x, axis=-1, keepdims=True)
    h = (x * lax.rsqrt(ms + EPS) * g_ref[...]).astype(BF16)

    a_ref[0] = _dot(h, w_ref[:, C_A:C_Q])
    zq = _dot(h, w_ref[:, C_Q:C_KC]) * (HEAD_DIM ** -0.5 * LOG2E)
    zero_hi = jnp.zeros((tm, HEAD_DIM), BF16)
    for hd in range(N_Q_HEADS):
        qh = zq[:, hd * HEAD_DIM:(hd + 1) * HEAD_DIM].astype(BF16)
        q_ref[0, hd] = jnp.concatenate([qh, zero_hi], axis=1)
    zkc = _dot(h, w_ref[:, C_KC:C_VC])
    zvc = _dot(h, w_ref[:, C_VC:C_KS])
    for half in range(KV_WIDTH // LANES):
        kcr_ref[0, half] = zkc[:, half * LANES:(half + 1) * LANES]
        vcr_ref[0, half] = zvc[:, half * LANES:(half + 1) * LANES]

    t = pl.program_id(1) * tm + lax.broadcasted_iota(jnp.int32, (tm, HEAD_DIM), 0)
    blk = lax.broadcasted_iota(jnp.int32, (tm, HEAD_DIM), 1)
    onehot = jnp.where(t // SEL_BLOCK == blk, 1.0, 0.0).astype(BF16)
    ones_col = jnp.where(blk == 0, 1.0, 0.0).astype(BF16)
    zks = _dot(h, w_ref[:, C_KS:C_VS])
    zvs = _dot(h, w_ref[:, C_VS:C_KW])
    zkw = _dot(h, w_ref[:, C_KW:C_VW])
    zvw = _dot(h, w_ref[:, C_VW:C_G])
    for g in range(N_KV_HEADS):
        sl = slice(g * HEAD_DIM, (g + 1) * HEAD_DIM)
        ksa_ref[0, g] = jnp.concatenate([zks[:, sl].astype(BF16), onehot], axis=1)
        vsa_ref[0, g] = jnp.concatenate([zvs[:, sl].astype(BF16), ones_col], axis=1)
        kw_ref[0, g] = zkw[:, sl].astype(BF16)
        vwa_ref[0, g] = jnp.concatenate([zvw[:, sl].astype(BF16), ones_col], axis=1)
    gate_ref[0] = _sigmoid(_dot(h, w_ref[:, C_G:IN_COLS_PAD]))


def _inproj(x, norm_g, w_pad):
    b, t, _ = x.shape
    tm = min(TM_PROJ, t)
    grid = (b, t // tm)
    tok = lambda bi, i: (bi, i, 0)
    head = lambda bi, i: (bi, 0, i, 0)
    out_shape = (
        jax.ShapeDtypeStruct((b, t, 2 * CONV_CH), F32),
        jax.ShapeDtypeStruct((b, N_Q_HEADS, t, 2 * HEAD_DIM), BF16),
        jax.ShapeDtypeStruct((b, KV_WIDTH // LANES, t, LANES), F32),
        jax.ShapeDtypeStruct((b, KV_WIDTH // LANES, t, LANES), F32),
        jax.ShapeDtypeStruct((b, N_KV_HEADS, t, 2 * HEAD_DIM), BF16),
        jax.ShapeDtypeStruct((b, N_KV_HEADS, t, 2 * HEAD_DIM), BF16),
        jax.ShapeDtypeStruct((b, N_KV_HEADS, t, HEAD_DIM), BF16),
        jax.ShapeDtypeStruct((b, N_KV_HEADS, t, 2 * HEAD_DIM), BF16),
        jax.ShapeDtypeStruct((b, t, GATE_PAD), F32),
    )
    out_specs = (
        pl.BlockSpec((1, tm, 2 * CONV_CH), tok),
        pl.BlockSpec((1, N_Q_HEADS, tm, 2 * HEAD_DIM), head),
        pl.BlockSpec((1, KV_WIDTH // LANES, tm, LANES), head),
        pl.BlockSpec((1, KV_WIDTH // LANES, tm, LANES), head),
        pl.BlockSpec((1, N_KV_HEADS, tm, 2 * HEAD_DIM), head),
        pl.BlockSpec((1, N_KV_HEADS, tm, 2 * HEAD_DIM), head),
        pl.BlockSpec((1, N_KV_HEADS, tm, HEAD_DIM), head),
        pl.BlockSpec((1, N_KV_HEADS, tm, 2 * HEAD_DIM), head),
        pl.BlockSpec((1, tm, GATE_PAD), tok),
    )
    return pl.pallas_call(
        _inproj_kernel,
        grid=grid,
        in_specs=[
            pl.BlockSpec((1, tm, D_MODEL), tok),
            _const_spec((1, D_MODEL)),
            _const_spec((D_MODEL, IN_COLS_PAD)),
        ],
        out_specs=out_specs,
        out_shape=out_shape,
        compiler_params=pltpu.CompilerParams(
            dimension_semantics=("arbitrary", "arbitrary"), vmem_limit_bytes=VMEM_LIMIT),
        name="inproj",
    )(x, norm_g, w_pad)


def _conv_kernel(a_ref, halo_ref, w_ref, b_ref, lg_ref, lb_ref, o_ref, y_ref):
    tc = a_ref.shape[1]
    am = a_ref[0]
    ah = halo_ref[0]
    ym = am[:, :CONV_CH] * _sigmoid(am[:, CONV_CH:])
    yh = ah[:, :CONV_CH] * _sigmoid(ah[:, CONV_CH:])
    yh = jnp.where(pl.program_id(1) == 0, 0.0, yh)
    y_ref[0:CONV_HALO, :] = yh
    y_ref[CONV_HALO:CONV_HALO + tc, :] = ym
    first = CONV_HALO - (CONV_WIDTH - 1)
    acc = b_ref[...]
    for rho in range(SUBLANES):
        n_rows = tc if rho == 0 else tc + SUBLANES
        z = None
        for base in range(0, CONV_HALO + 1, SUBLANES):
            k = base + rho - first
            if 0 <= k < CONV_WIDTH:
                term = w_ref[k:k + 1, :] * y_ref[pl.ds(base, n_rows), :]
                z = term if z is None else z + term
        acc = acc + (z if rho == 0 else z[rho:rho + tc, :])
    mu = jnp.mean(acc, axis=-1, keepdims=True)
    d = acc - mu
    var = jnp.mean(d * d, axis=-1, keepdims=True)
    yn = d * lax.rsqrt(var + EPS) * lg_ref[...] + lb_ref[...]
    o_ref[0] = (yn * _sigmoid(yn)).astype(o_ref.dtype)


def _conv_mixer(a, w_dw, b_dw, ln_g, ln_b):
    b, t, _ = a.shape
    tc = min(TC_CONV, t)
    per = tc // CONV_HALO
    return pl.pallas_call(
        _conv_kernel,
        grid=(b, t // tc),
        in_specs=[
            pl.BlockSpec((1, tc, 2 * CONV_CH), lambda bi, i: (bi, i, 0)),
            pl.BlockSpec((1, CONV_HALO, 2 * CONV_CH),
                         lambda bi, i: (bi, jnp.maximum(i * per - 1, 0), 0)),
            _const_spec((CONV_HALO, CONV_CH)),
            _const_spec((1, CONV_CH)),
            _const_spec((1, CONV_CH)),
            _const_spec((1, CONV_CH)),
        ],
        out_specs=pl.BlockSpec((1, tc, CONV_CH), lambda bi, i: (bi, i, 0)),
        out_shape=jax.ShapeDtypeStruct((b, t, CONV_CH), BF16),
        scratch_shapes=[pltpu.VMEM((CONV_HALO + tc, CONV_CH), F32)],
        compiler_params=pltpu.CompilerParams(
            dimension_semantics=("arbitrary", "arbitrary"), vmem_limit_bytes=VMEM_LIMIT),
        name="conv_mixer",
    )(a, a, w_dw, b_dw, ln_g, ln_b)


def _compress_kernel(kr_ref, vr_ref, kpe_ref, kw1_ref, kw2_ref, vpe_ref, vw1_ref, vw2_ref,
                     kc_ref, vc_ref):
    nch = kr_ref.shape[2] // CMP_STRIDE
    row = lax.broadcasted_iota(jnp.int32, (nch, HEAD_DIM), 0)
    c_start = row * CMP_STRIDE
    s_start = lax.broadcasted_iota(jnp.int32, (nch, HEAD_DIM), 1) * SEL_BLOCK
    overlap = jnp.where((c_start < s_start + SEL_BLOCK) & (c_start + CMP_LEN > s_start),
                        1.0, 0.0).astype(BF16)
    for r_ref, pe_ref, w1_ref, w2_ref, o_ref, tail in (
            (kr_ref, kpe_ref, kw1_ref, kw2_ref, kc_ref, None),
            (vr_ref, vpe_ref, vw1_ref, vw2_ref, vc_ref, overlap)):
        pb = _dot(pe_ref[...], w1_ref[...])
        bias = pb[0:1, :CMP_HIDDEN] + pb[8:9, CMP_HIDDEN:]
        rows = [[r_ref[0, half, pl.ds(l, nch, stride=CMP_STRIDE), :] for l in range(CMP_STRIDE)]
                for half in range(KV_WIDTH // LANES)]
        for g in range(N_KV_HEADS):
            half, odd = divmod(g, LANES // HEAD_DIM)
            sl = slice(odd * HEAD_DIM, (odd + 1) * HEAD_DIM)
            xg = jnp.concatenate([rw[:, sl] for rw in rows[half]], axis=1).astype(BF16)
            lohi = _dot(xg, w1_ref[...])
            lo = lohi[:, :CMP_HIDDEN]
            hi = lohi[:, CMP_HIDDEN:]
            hi_next = jnp.concatenate([hi[1:], jnp.zeros((1, CMP_HIDDEN), F32)], axis=0)
            hid = lo + hi_next + bias
            act = (hid * _sigmoid(hid)).astype(BF16)
            out = _dot(act, w2_ref[...])
            out = jnp.where(row < nch - 1, out, 0.0)
            out = out.astype(o_ref.dtype)
            o_ref[0, g] = out if tail is None else jnp.concatenate([out, tail], axis=1)


def _compress(kc_r, vc_r, kpe, kw1, kw2, vpe, vw1, vw2):
    b, _, t, _ = kc_r.shape
    nch = t // CMP_STRIDE
    blk_len = CMP_STRIDE * HEAD_DIM
    out_sds = [jax.ShapeDtypeStruct((b, N_KV_HEADS, nch, w), BF16) for w in (HEAD_DIM, 2 * HEAD_DIM)]
    out_spec = [pl.BlockSpec((1, N_KV_HEADS, nch, w), lambda bi: (bi, 0, 0, 0))
                for w in (HEAD_DIM, 2 * HEAD_DIM)]
    raw_spec = pl.BlockSpec((1, KV_WIDTH // LANES, t, LANES), lambda bi: (bi, 0, 0, 0))
    w_specs = [_const_spec((16, blk_len)), _const_spec((blk_len, 2 * CMP_HIDDEN)),
               _const_spec((CMP_HIDDEN, HEAD_DIM))]
    return pl.pallas_call(
        _compress_kernel,
        grid=(b,),
        in_specs=[raw_spec, raw_spec] + w_specs + w_specs,
        out_specs=tuple(out_spec),
        out_shape=tuple(out_sds),
        compiler_params=pltpu.CompilerParams(
            dimension_semantics=("arbitrary",), vmem_limit_bytes=VMEM_LIMIT),
        name="compress",
    )(kc_r, vc_r, kpe, kw1, kw2, vpe, vw1, vw2)


def _nsa_kernel(slope_ref, q_ref, kc_ref, vca_ref, ksa_ref, vsa_ref, kw_ref, vwa_ref, gate_ref,
                o_ref, qa_ref, se_ref, so_ref, mx_ref, pa_ref, pb_ref, acc_ref, pw_ref, p4_ref, vg_ref,
                m1_ref, acc1_ref, list_ref):
    g = pl.program_id(1)
    i = pl.program_id(2)
    t0 = i * TQ
    ncp = kc_ref.shape[2]
    slopes = [slope_ref[g * Q_PER_KV + r] for r in range(Q_PER_KV)]
    rows = [slice(r * TQ, (r + 1) * TQ) for r in range(Q_PER_KV)]
    halves = [slice(hf * LANES, (hf + 1) * LANES) for hf in range(TQ // LANES)]

    q_full = q_ref[0].reshape(Q_PER_KV * TQ, 2 * HEAD_DIM)
    q3 = q_full[:, :HEAD_DIM]

    w0 = pl.multiple_of(jnp.maximum(t0 - WINDOW, 0), TQ)
    s_w = _dot_nt(q3, kw_ref[0, 0, pl.ds(w0, WIN_KEYS), :])
    wk_row = w0 - t0 + lax.broadcasted_iota(jnp.int32, (1, WIN_KEYS), 1)
    dist = (lax.broadcasted_iota(jnp.int32, (TQ, WIN_KEYS), 0)
            - (w0 - t0) - lax.broadcasted_iota(jnp.int32, (TQ, WIN_KEYS), 1))
    w_valid = (dist >= 0) & (dist < WINDOW)
    wk_pos = wk_row.astype(F32)
    for r in range(Q_PER_KV):
        sr = jnp.where(w_valid, s_w[rows[r]] + slopes[r] * wk_pos, MASK_NEG)
        m = jnp.max(sr, axis=1, keepdims=True)
        pw_ref[rows[r], :] = jnp.exp2(sr - m).astype(BF16)
    acc_w = _dot(pw_ref[...], vwa_ref[0, 0, pl.ds(w0, WIN_KEYS), :])

    s_c = _dot_nt(q3, kc_ref[0, 0])
    n_idx = lax.broadcasted_iota(jnp.int32, (TQ, ncp), 1)
    t_idx = t0 + lax.broadcasted_iota(jnp.int32, (TQ, ncp), 0)
    c_valid = n_idx * CMP_STRIDE + (CMP_LEN - 1) <= t_idx
    n_row = lax.broadcasted_iota(jnp.int32, (1, ncp), 1)
    c_pos = (n_row * CMP_STRIDE - t0).astype(F32) + 0.5 * (CMP_LEN - 1)
    vca = vca_ref[0, 0]
    o_c = []
    imp = jnp.zeros((TQ, LANES), F32)
    for r in range(Q_PER_KV):
        sr = jnp.where(c_valid, s_c[rows[r]] + slopes[r] * c_pos, -jnp.inf)
        m = jnp.max(sr, axis=1, keepdims=True)
        m = jnp.where(m == -jnp.inf, 0.0, m)
        p = jnp.exp2(sr - m)
        inv = 1.0 / jnp.maximum(jnp.sum(p, axis=1, keepdims=True), 1e-30)
        pv = _dot(p.astype(BF16), vca) * inv
        o_c.append(pv[:, :HEAD_DIM])
        imp = imp + pv

    gates = gate_ref[0]
    lane = lax.broadcasted_iota(jnp.int32, (TQ, LANES), 1)
    partial, g_sel = [], []
    for r in range(Q_PER_KV):
        col = g * (Q_PER_KV * N_BRANCH) + r * N_BRANCH
        gc, gs, gw = (jnp.sum(jnp.where(lane == col + br, gates, 0.0), axis=1, keepdims=True)
                      for br in range(N_BRANCH))
        a = acc_w[rows[r]]
        o_w = a[:, :HEAD_DIM] / a[:, HEAD_DIM:HEAD_DIM + 1]
        partial.append(gc * o_c[r] + gw * o_w)
        g_sel.append(jnp.broadcast_to(gs, (TQ, HEAD_DIM)))

    imp_t = imp.T[HEAD_DIM:, :]
    j_t = lax.broadcasted_iota(jnp.int32, (LANES - HEAD_DIM, TQ), 0)
    cur_t = (t0 + lax.broadcasted_iota(jnp.int32, (LANES - HEAD_DIM, TQ), 1)) // SEL_BLOCK
    forced = (j_t == 0) | (j_t == cur_t) | (j_t == cur_t - 1)
    cand = jnp.where(j_t <= cur_t, imp_t + jnp.where(forced, FORCED_BONUS, 0.0), -jnp.inf)
    sel = jnp.zeros(cand.shape, jnp.bool_)
    for _ in range(SEL_TOPK):
        mx = jnp.max(cand, axis=0, keepdims=True)
        first = jnp.min(jnp.where(cand == mx, j_t, LANES), axis=0, keepdims=True)
        hit = j_t == first
        sel = sel | (hit & (mx > -jnp.inf))
        cand = jnp.where(hit, -jnp.inf, cand)
    bias_t = jnp.concatenate([jnp.zeros(cand.shape, F32), jnp.where(sel, 0.0, MASK_NEG)], axis=0)
    sel_bias = bias_t.T
    sel_bias_bf = sel_bias.astype(BF16)
    for r in range(Q_PER_KV):
        qa_ref[rows[r], :] = q_full[rows[r]] + sel_bias_bf

    lane_row = lax.broadcasted_iota(jnp.int32, (1, TQ), 1)
    causal = (lax.broadcasted_iota(jnp.int32, (TQ, TQ), 1)
              <= lax.broadcasted_iota(jnp.int32, (TQ, TQ), 0))

    blk_any = jnp.broadcast_to(jnp.max(sel_bias, axis=0, keepdims=True), (8, LANES))
    per_chunk = TQ // SEL_BLOCK
    n_chunks = list_ref.shape[0] - 2
    chunk_any = blk_any
    for d in range(1, per_chunk):
        chunk_any = jnp.maximum(chunk_any, pltpu.roll(blk_any, LANES - d, 1))
    blk8 = lax.broadcasted_iota(jnp.int32, (8, LANES), 1) - HEAD_DIM
    chunk_bit = jnp.where((blk8 >= 0) & (blk8 % per_chunk == 0),
                          jnp.left_shift(1, jnp.maximum(blk8, 0) // per_chunk), 0).astype(F32)
    flagged = jnp.where(chunk_any > 0.5 * MASK_NEG, chunk_bit, 0.0)
    bits = jnp.sum(flagged[0:1, :], axis=1, keepdims=True)[0, 0].astype(jnp.int32)
    cnt = jnp.int32(0)
    idle = jnp.int32(0)
    for c in range(n_chunks):
        below = c < i
        hit = (((bits >> c) & 1) == 1) & below
        list_ref[cnt] = jnp.int32(c)
        cnt = cnt + hit.astype(jnp.int32)
        idle = jnp.where(below & jnp.logical_not(hit), c, idle)
    filler = jnp.where(i + 1 < n_chunks, i + 1, idle)
    list_ref[cnt] = filler

    head = [jnp.where(cnt > j, list_ref[j], filler) for j in range(SHORT_LIST - 1)] + [i]
    scores = []
    for j, c in enumerate(head):
        k0 = pl.multiple_of(c * TQ, TQ)
        vg_ref[j * TQ:(j + 1) * TQ, :] = vsa_ref[0, 0, pl.ds(k0, TQ), :]
        scores.append(_dot_nt(qa_ref[...], ksa_ref[0, 0, pl.ds(k0, TQ), :]))
    for r in range(Q_PER_KV):
        biased = []
        for j, c in enumerate(head):
            sr = scores[j][rows[r]] + slopes[r] * (c * TQ - t0 + lane_row).astype(F32)
            biased.append(jnp.where(causal, sr, MASK_NEG) if j == SHORT_LIST - 1 else sr)
        part = biased[0][:, halves[0]]
        for sr in biased:
            for hf in halves:
                part = jnp.maximum(part, sr[:, hf])
        m = jnp.max(part, axis=1, keepdims=True)
        m1_ref[rows[r], :] = jnp.broadcast_to(m, (TQ, LANES))
        for j, sr in enumerate(biased):
            p4_ref[rows[r], j * TQ:(j + 1) * TQ] = jnp.exp2(sr - m).astype(BF16)
    acc1_ref[...] = _dot(p4_ref[...], vg_ref[...])

    @pl.when(cnt > SHORT_LIST - 1)
    def _rest_of_list():
        k_last = (cnt - SHORT_LIST + 2) // 2 - 1

        def entry(pos):
            return list_ref[SHORT_LIST - 1 + pos]

        def raw_scores(pos, dst_ref, k):
            k0 = pl.multiple_of(entry(pos) * TQ, TQ)
            s_all = _dot_nt(qa_ref[...], ksa_ref[0, 0, pl.ds(k0, TQ), :])
            for r in range(Q_PER_KV):
                dst_ref[k, rows[r], :] = s_all[rows[r]]

        def alibi_row(pos, r):
            return slopes[r] * (entry(pos) * TQ - t0 + lane_row).astype(F32)

        def fold_max(pos, src_ref, k):
            for r in range(Q_PER_KV):
                sr = src_ref[k, rows[r], :] + alibi_row(pos, r)
                part = sr[:, halves[0]]
                for hf in halves[1:]:
                    part = jnp.maximum(part, sr[:, hf])
                mx_ref[rows[r], :] = jnp.maximum(mx_ref[rows[r], :], part)

        mx_ref[...] = jnp.full(mx_ref.shape, M_INIT, F32)
        raw_scores(0, se_ref, 0)

        def pass1(k, carry):
            raw_scores(2 * k + 1, so_ref, k)
            fold_max(2 * k, se_ref, k)
            raw_scores(2 * k + 2, se_ref, k + 1)
            fold_max(2 * k + 1, so_ref, k)
            return carry

        lax.fori_loop(0, k_last, pass1, 0)
        raw_scores(2 * k_last + 1, so_ref, k_last)
        fold_max(2 * k_last, se_ref, k_last)
        fold_max(2 * k_last + 1, so_ref, k_last)

        for r in range(Q_PER_KV):
            m = jnp.max(mx_ref[rows[r], :], axis=1, keepdims=True)
            mx_ref[rows[r], :] = jnp.broadcast_to(m, (TQ, LANES))
        acc_ref[...] = jnp.zeros(acc_ref.shape, F32)

        def probs(pos, src_ref, k, dst_ref):
            for r in range(Q_PER_KV):
                mb = mx_ref[rows[r], :]
                bias = alibi_row(pos, r)
                for hf in halves:
                    dst_ref[rows[r], hf] = jnp.exp2(src_ref[k, rows[r], hf] + bias[:, hf] - mb).astype(BF16)

        def add_pv(pos, src_ref):
            k0 = pl.multiple_of(entry(pos) * TQ, TQ)
            acc_ref[...] += _dot(src_ref[...], vsa_ref[0, 0, pl.ds(k0, TQ), :])

        probs(0, se_ref, 0, pa_ref)

        def pass2(k, carry):
            add_pv(2 * k, pa_ref)
            probs(2 * k + 1, so_ref, k, pb_ref)
            add_pv(2 * k + 1, pb_ref)
            probs(2 * k + 2, se_ref, k + 1, pa_ref)
            return carry

        lax.fori_loop(0, k_last, pass2, 0)
        add_pv(2 * k_last, pa_ref)
        probs(2 * k_last + 1, so_ref, k_last, pb_ref)
        add_pv(2 * k_last + 1, pb_ref)

        for r in range(Q_PER_KV):
            m_head = m1_ref[rows[r], :]
            m_rest = mx_ref[rows[r], :]
            m = jnp.maximum(m_head, m_rest)
            acc1_ref[rows[r], :] = (acc1_ref[rows[r], :] * jnp.exp2(m_head - m)
                                    + acc_ref[rows[r], :] * jnp.exp2(m_rest - m))

    outs = []
    for r in range(Q_PER_KV):
        a = acc1_ref[rows[r], :]
        o_s = a[:, :HEAD_DIM] / a[:, HEAD_DIM:HEAD_DIM + 1]
        outs.append(partial[r] + g_sel[r] * o_s)
    o_ref[0, 0] = jnp.concatenate(outs, axis=1).astype(o_ref.dtype)


def _nsa(q, kc, vca, ksa, vsa, kw, vwa, gates, slopes):
    b, _, t, _ = q.shape
    ncp = kc.shape[2]
    grp = lambda bi, g, i: (bi, g, 0, 0)
    return pl.pallas_call(
        _nsa_kernel,
        grid=(b, N_KV_HEADS, t // TQ),
        in_specs=[
            pl.BlockSpec(memory_space=pltpu.SMEM),
            pl.BlockSpec((1, Q_PER_KV, TQ, 2 * HEAD_DIM), lambda bi, g, i: (bi, g, i, 0)),
            pl.BlockSpec((1, 1, ncp, HEAD_DIM), grp),
            pl.BlockSpec((1, 1, ncp, 2 * HEAD_DIM), grp),
            pl.BlockSpec((1, 1, t, 2 * HEAD_DIM), grp),
            pl.BlockSpec((1, 1, t, 2 * HEAD_DIM), grp),
            pl.BlockSpec((1, 1, t, HEAD_DIM), grp),
            pl.BlockSpec((1, 1, t, 2 * HEAD_DIM), grp),
            pl.BlockSpec((1, TQ, GATE_PAD), lambda bi, g, i: (bi, i, 0)),
        ],
        out_specs=pl.BlockSpec((1, 1, TQ, Q_PER_KV * HEAD_DIM), lambda bi, g, i: (bi, g, i, 0)),
        out_shape=jax.ShapeDtypeStruct((b, N_KV_HEADS, t, Q_PER_KV * HEAD_DIM), BF16),
        scratch_shapes=[
            pltpu.VMEM((Q_PER_KV * TQ, 2 * HEAD_DIM), BF16),
            pltpu.VMEM((t // (2 * TQ), Q_PER_KV * TQ, TQ), F32),
            pltpu.VMEM((t // (2 * TQ), Q_PER_KV * TQ, TQ), F32),
            pltpu.VMEM((Q_PER_KV * TQ, LANES), F32),
            pltpu.VMEM((Q_PER_KV * TQ, TQ), BF16),
            pltpu.VMEM((Q_PER_KV * TQ, TQ), BF16),
            pltpu.VMEM((Q_PER_KV * TQ, LANES), F32),
            pltpu.VMEM((Q_PER_KV * TQ, WIN_KEYS), BF16),
            pltpu.VMEM((Q_PER_KV * TQ, SHORT_LIST * TQ), BF16),
            pltpu.VMEM((SHORT_LIST * TQ, 2 * HEAD_DIM), BF16),
            pltpu.VMEM((Q_PER_KV * TQ, LANES), F32),
            pltpu.VMEM((Q_PER_KV * TQ, LANES), F32),
            pltpu.SMEM((t // TQ + 2,), jnp.int32),
        ],
        compiler_params=pltpu.CompilerParams(
            dimension_semantics=("arbitrary", "arbitrary", "arbitrary"),
            vmem_limit_bytes=VMEM_LIMIT),
        name="nsa_attention",
    )(slopes, q, kc, vca, ksa, vsa, kw, vwa, gates)


def _outffn_kernel(x_ref, c_ref, n_ref, woc_ref, won_ref, fg_ref, wgu_ref, wd_ref, fin_ref,
                   o_ref, acc_ref, h_ref, act_ref, *, final):
    x1 = x_ref[0] + _dot(c_ref[0], woc_ref[...])
    for g in range(N_KV_HEADS):
        x1 = x1 + _dot(n_ref[0, g], won_ref[g])
    ms = jnp.mean(x1 * x1, axis=-1, keepdims=True)
    h_ref[...] = (x1 * lax.rsqrt(ms + EPS) * fg_ref[...]).astype(BF16)
    acc_ref[...] = x1

    for c in range(N_FF_CHUNKS):
        h = h_ref[...]
        gate = _dot(h, wgu_ref[:, c * FF_CHUNK:(c + 1) * FF_CHUNK])
        up = _dot(h, wgu_ref[:, D_FF + c * FF_CHUNK:D_FF + (c + 1) * FF_CHUNK])
        act_ref[:, c * FF_CHUNK:(c + 1) * FF_CHUNK] = (gate * _sigmoid(gate) * up).astype(BF16)
    y = acc_ref[...] + _dot(act_ref[...], wd_ref[...])
    if final:
        ms = jnp.mean(y * y, axis=-1, keepdims=True)
        y = y * lax.rsqrt(ms + EPS) * fin_ref[...]
    o_ref[0] = y


def _outffn(x, conv_out, nsa_out, wo_c, wo_n, ffn_g, wgu, wd, final_g, final):
    b, t, _ = x.shape
    tm = min(TM_PROJ, t)
    tok = lambda bi, i: (bi, i, 0)
    return pl.pallas_call(
        functools.partial(_outffn_kernel, final=final),
        grid=(b, t // tm),
        in_specs=[
            pl.BlockSpec((1, tm, D_MODEL), tok),
            pl.BlockSpec((1, tm, CONV_CH), tok),
            pl.BlockSpec((1, N_KV_HEADS, tm, Q_PER_KV * HEAD_DIM), lambda bi, i: (bi, 0, i, 0)),
            _const_spec((CONV_CH, D_MODEL)),
            _const_spec((N_KV_HEADS, Q_PER_KV * HEAD_DIM, D_MODEL)),
            _const_spec((1, D_MODEL)),
            _const_spec((D_MODEL, 2 * D_FF)),
            _const_spec((D_FF, D_MODEL)),
            _const_spec((1, D_MODEL)),
        ],
        out_specs=pl.BlockSpec((1, tm, D_MODEL), tok),
        out_shape=jax.ShapeDtypeStruct((b, t, D_MODEL), F32),
        scratch_shapes=[pltpu.VMEM((tm, D_MODEL), F32), pltpu.VMEM((tm, D_MODEL), BF16),
                        pltpu.VMEM((tm, D_FF), BF16)],
        compiler_params=pltpu.CompilerParams(
            dimension_semantics=("arbitrary", "arbitrary"), vmem_limit_bytes=VMEM_LIMIT),
        name="outproj_ffn",
    )(x, conv_out, nsa_out, wo_c, wo_n, ffn_g, wgu, wd, final_g)


def _pe_rows(pe):
    lo = pe[:CMP_STRIDE].reshape(1, -1)
    hi = pe[CMP_STRIDE:].reshape(1, -1)
    z = jnp.zeros((7, lo.shape[1]), pe.dtype)
    return jnp.concatenate([lo, z, hi, z], axis=0).astype(BF16)


def _w1_cat(w1):
    lo = w1[:CMP_STRIDE].reshape(CMP_STRIDE * HEAD_DIM, CMP_HIDDEN)
    hi = w1[CMP_STRIDE:].reshape(CMP_STRIDE * HEAD_DIM, CMP_HIDDEN)
    return jnp.concatenate([lo, hi], axis=1).astype(BF16)


def kernel(x, attn_norm, w_in, conv_w, conv_b, conv_ln_g, conv_ln_b, cmp_k_pe, cmp_k_w1, cmp_k_w2,
           cmp_v_pe, cmp_v_w1, cmp_v_w2, w_out, ffn_norm, w_gate_up, w_down, final_norm):
    depth = w_in.shape[0]
    t = x.shape[1]
    assert t % (2 * TQ) == 0 and t // SEL_BLOCK <= LANES - HEAD_DIM and t >= WIN_KEYS
    slopes =jnp.asarray(_alibi_slopes(N_Q_HEADS) * np.float32(LOG2E))
    final_g = final_norm.reshape(1, D_MODEL)
    for l in range(depth):
        w_pad = jnp.pad(w_in[l], ((0, 0), (0, IN_COLS_PAD - w_in.shape[2]))).astype(BF16)
        a, q, kc_r, vc_r, ksa, vsa, kw, vwa, gates = _inproj(
            x, attn_norm[l].reshape(1, D_MODEL), w_pad)
        conv_out = _conv_mixer(
            a, jnp.pad(conv_w[l], ((0, CONV_HALO - CONV_WIDTH), (0, 0))),
            conv_b[l].reshape(1, CONV_CH), conv_ln_g[l].reshape(1, CONV_CH),
            conv_ln_b[l].reshape(1, CONV_CH))
        kc, vca = _compress(kc_r, vc_r,
                           _pe_rows(cmp_k_pe[l]), _w1_cat(cmp_k_w1[l]), cmp_k_w2[l].astype(BF16),
                           _pe_rows(cmp_v_pe[l]), _w1_cat(cmp_v_w1[l]), cmp_v_w2[l].astype(BF16))
        nsa_out = _nsa(q, kc, vca, ksa, vsa, kw, vwa, gates, slopes)
        wo = w_out[l].astype(BF16)
        wo_n = wo[CONV_CH:].reshape(N_KV_HEADS, Q_PER_KV * HEAD_DIM, D_MODEL)
        x = _outffn(x, conv_out, nsa_out, wo[:CONV_CH], wo_n, ffn_norm[l].reshape(1, D_MODEL),
                    w_gate_up[l].astype(BF16), w_down[l].astype(BF16), final_g,
                    final=(l == depth - 1))
    return x
```

```python
import functools
import math

import jax
import jax.numpy as jnp
import numpy as np
from jax import lax
from jax.experimental import pallas as pl
from jax.experimental.pallas import tpu as pltpu

F32 = jnp.float32
BF16 = jnp.bfloat16

D_MODEL = 1024
HEAD_DIM = 64
CONV_CH = 256
CONV_WIDTH = 31
N_Q_HEADS = 12
N_KV_HEADS = 4
Q_PER_KV = N_Q_HEADS // N_KV_HEADS
NSA_WIDTH = N_Q_HEADS * HEAD_DIM
KV_WIDTH = N_KV_HEADS * HEAD_DIM
CMP_LEN = 32
CMP_STRIDE = 16
CMP_HIDDEN = 256
SEL_BLOCK = 64
SEL_TOPK = 8
WINDOW = 512
N_BRANCH = 3
FORCED_BONUS = 1000.0
D_FF = 2816
EPS = 1e-6
LOG2E = math.log2(math.e)

LANES = 128
SUBLANES = 8
GATE_PAD = LANES
FF_CHUNK = 256
N_FF_CHUNKS = D_FF // FF_CHUNK
TM_PROJ = 512
TC_CONV = 512
CONV_HALO = 32
TQ = 256
WIN_KEYS = WINDOW + TQ
SHORT_LIST = 4
TILES_PER_STEP = 2
MASK_NEG = -(2.0 ** 100)
M_INIT = -1e30
VMEM_LIMIT = 56 * 1024 * 1024


def _alibi_slopes(n):
    def pow2_slopes(m):
        start = 2.0 ** (-8.0 / m)
        return [start ** (i + 1) for i in range(m)]
    if math.log2(n).is_integer():
        s = pow2_slopes(n)
    else:
        c = 2 ** math.floor(math.log2(n))
        s = pow2_slopes(c) + pow2_slopes(2 * c)[0::2][: n - c]
    return np.asarray(s, dtype=np.float32)


def _sigmoid(v):
    return 1.0 / (1.0 + jnp.exp(-v))


def _dot(a, b):
    return jnp.dot(a, b, preferred_element_type=F32)


def _dot_nt(a, b):
    return lax.dot_general(a, b, (((1,), (1,)), ((), ())), preferred_element_type=F32)


def _const_spec(shape):
    nd = len(shape)
    return pl.BlockSpec(shape, lambda *_: (0,) * nd, pipeline_mode=pl.Buffered(1))


C_A = 0
C_Q = C_A + 2 * CONV_CH
C_KC = C_Q + NSA_WIDTH
C_VC = C_KC + KV_WIDTH
C_KS = C_VC + KV_WIDTH
C_VS = C_KS + KV_WIDTH
C_KW = C_VS + KV_WIDTH
C_VW = C_KW + KV_WIDTH
C_G = C_VW + KV_WIDTH
IN_COLS_PAD = C_G + GATE_PAD


def _inproj_kernel(x_ref, g_ref, w_ref, a_ref, q_ref, kcr_ref, vcr_ref, ksa_ref, vsa_ref, kw_ref,
                   vwa_ref, gate_ref):
    tm = x_ref.shape[1]
    x = x_ref[0]
    ms = jnp.mean(x * x, axis=-1, keepdims=True)
    h = (x * lax.rsqrt(ms + EPS) * g_ref[...]).astype(BF16)

    a_ref[0] = _dot(h, w_ref[:, C_A:C_Q])
    zq = _dot(h, w_ref[:, C_Q:C_KC]) * (HEAD_DIM ** -0.5 * LOG2E)
    zero_hi = jnp.zeros((tm, HEAD_DIM), BF16)
    for hd in range(N_Q_HEADS):
        qh = zq[:, hd * HEAD_DIM:(hd + 1) * HEAD_DIM].astype(BF16)
        q_ref[0, hd] = jnp.concatenate([qh, zero_hi], axis=1)
    zkc = _dot(h, w_ref[:, C_KC:C_VC])
    zvc = _dot(h, w_ref[:, C_VC:C_KS])
    for half in range(KV_WIDTH // LANES):
        kcr_ref[0, half] = zkc[:, half * LANES:(half + 1) * LANES]
        vcr_ref[0, half] = zvc[:, half * LANES:(half + 1) * LANES]

    t = pl.program_id(1) * tm + lax.broadcasted_iota(jnp.int32, (tm, HEAD_DIM), 0)
    blk = lax.broadcasted_iota(jnp.int32, (tm, HEAD_DIM), 1)
    onehot = jnp.where(t // SEL_BLOCK == blk, 1.0, 0.0).astype(BF16)
    ones_col = jnp.where(blk == 0, 1.0, 0.0).astype(BF16)
    zks = _dot(h, w_ref[:, C_KS:C_VS])
    zvs = _dot(h, w_ref[:, C_VS:C_KW])
    zkw = _dot(h, w_ref[:, C_KW:C_VW])
    zvw = _dot(h, w_ref[:, C_VW:C_G])
    for g in range(N_KV_HEADS):
        sl = slice(g * HEAD_DIM, (g + 1) * HEAD_DIM)
        ksa_ref[0, g] = jnp.concatenate([zks[:, sl].astype(BF16), onehot], axis=1)
        vsa_ref[0, g] = jnp.concatenate([zvs[:, sl].astype(BF16), ones_col], axis=1)
        kw_ref[0, g] = zkw[:, sl].astype(BF16)
        vwa_ref[0, g] = jnp.concatenate([zvw[:, sl].astype(BF16), ones_col], axis=1)
    gate_ref[0] = _sigmoid(_dot(h, w_ref[:, C_G:IN_COLS_PAD]))


def _inproj(x, norm_g, w_pad):
    b, t, _ = x.shape
    tm = min(TM_PROJ, t)
    grid = (b, t // tm)
    tok = lambda bi, i: (bi, i, 0)
    head = lambda bi, i: (bi, 0, i, 0)
    out_shape = (
        jax.ShapeDtypeStruct((b, t, 2 * CONV_CH), F32),
        jax.ShapeDtypeStruct((b, N_Q_HEADS, t, 2 * HEAD_DIM), BF16),
        jax.ShapeDtypeStruct((b, KV_WIDTH // LANES, t, LANES), F32),
        jax.ShapeDtypeStruct((b, KV_WIDTH // LANES, t, LANES), F32),
        jax.ShapeDtypeStruct((b, N_KV_HEADS, t, 2 * HEAD_DIM), BF16),
        jax.ShapeDtypeStruct((b, N_KV_HEADS, t, 2 * HEAD_DIM), BF16),
        jax.ShapeDtypeStruct((b, N_KV_HEADS, t, HEAD_DIM), BF16),
        jax.ShapeDtypeStruct((b, N_KV_HEADS, t, 2 * HEAD_DIM), BF16),
        jax.ShapeDtypeStruct((b, t, GATE_PAD), F32),
    )
    out_specs = (
        pl.BlockSpec((1, tm, 2 * CONV_CH), tok),
        pl.BlockSpec((1, N_Q_HEADS, tm, 2 * HEAD_DIM), head),
        pl.BlockSpec((1, KV_WIDTH // LANES, tm, LANES), head),
        pl.BlockSpec((1, KV_WIDTH // LANES, tm, LANES), head),
        pl.BlockSpec((1, N_KV_HEADS, tm, 2 * HEAD_DIM), head),
        pl.BlockSpec((1, N_KV_HEADS, tm, 2 * HEAD_DIM), head),
        pl.BlockSpec((1, N_KV_HEADS, tm, HEAD_DIM), head),
        pl.BlockSpec((1, N_KV_HEADS, tm, 2 * HEAD_DIM), head),
        pl.BlockSpec((1, tm, GATE_PAD), tok),
    )
    return pl.pallas_call(
        _inproj_kernel,
        grid=grid,
        in_specs=[
            pl.BlockSpec((1, tm, D_MODEL), tok),
            _const_spec((1, D_MODEL)),
            _const_spec((D_MODEL, IN_COLS_PAD)),
        ],
        out_specs=out_specs,
        out_shape=out_shape,
        compiler_params=pltpu.CompilerParams(
            dimension_semantics=("arbitrary", "arbitrary"), vmem_limit_bytes=VMEM_LIMIT),
        name="inproj",
    )(x, norm_g, w_pad)


def _conv_kernel(a_ref, halo_ref, w_ref, b_ref, lg_ref, lb_ref, o_ref, y_ref):
    tc = a_ref.shape[1]
    am = a_ref[0]
    ah = halo_ref[0]
    ym = am[:, :CONV_CH] * _sigmoid(am[:, CONV_CH:])
    yh = ah[:, :CONV_CH] * _sigmoid(ah[:, CONV_CH:])
    yh = jnp.where(pl.program_id(1) == 0, 0.0, yh)
    y_ref[0:CONV_HALO, :] = yh
    y_ref[CONV_HALO:CONV_HALO + tc, :] = ym
    first = CONV_HALO - (CONV_WIDTH - 1)
    acc = b_ref[...]
    for rho in range(SUBLANES):
        n_rows = tc if rho == 0 else tc + SUBLANES
        z = None
        for base in range(0, CONV_HALO + 1, SUBLANES):
            k = base + rho - first
            if 0 <= k < CONV_WIDTH:
                term = w_ref[k:k + 1, :] * y_ref[pl.ds(base, n_rows), :]
                z = term if z is None else z + term
        acc = acc + (z if rho == 0 else z[rho:rho + tc, :])
    mu = jnp.mean(acc, axis=-1, keepdims=True)
    d = acc - mu
    var = jnp.mean(d * d, axis=-1, keepdims=True)
    yn = d * lax.rsqrt(var + EPS) * lg_ref[...] + lb_ref[...]
    o_ref[0] = (yn * _sigmoid(yn)).astype(o_ref.dtype)


def _conv_mixer(a, w_dw, b_dw, ln_g, ln_b):
    b, t, _ = a.shape
    tc = min(TC_CONV, t)
    per = tc // CONV_HALO
    return pl.pallas_call(
        _conv_kernel,
        grid=(b, t // tc),
        in_specs=[
            pl.BlockSpec((1, tc, 2 * CONV_CH), lambda bi, i: (bi, i, 0)),
            pl.BlockSpec((1, CONV_HALO, 2 * CONV_CH),
                         lambda bi, i: (bi, jnp.maximum(i * per - 1, 0), 0)),
            _const_spec((CONV_HALO, CONV_CH)),
            _const_spec((1, CONV_CH)),
            _const_spec((1, CONV_CH)),
            _const_spec((1, CONV_CH)),
        ],
        out_specs=pl.BlockSpec((1, tc, CONV_CH), lambda bi, i: (bi, i, 0)),
        out_shape=jax.ShapeDtypeStruct((b, t, CONV_CH), BF16),
        scratch_shapes=[pltpu.VMEM((CONV_HALO + tc, CONV_CH), F32)],
        compiler_params=pltpu.CompilerParams(
            dimension_semantics=("arbitrary", "arbitrary"), vmem_limit_bytes=VMEM_LIMIT),
        name="conv_mixer",
    )(a, a, w_dw, b_dw, ln_g, ln_b)


def _compress_kernel(kr_ref, vr_ref, kpe_ref, kw1_ref, kw2_ref, vpe_ref, vw1_ref, vw2_ref,
                     kc_ref, vc_ref):
    nch = kr_ref.shape[2] // CMP_STRIDE
    row = lax.broadcasted_iota(jnp.int32, (nch, HEAD_DIM), 0)
    c_start = row * CMP_STRIDE
    s_start = lax.broadcasted_iota(jnp.int32, (nch, HEAD_DIM), 1) * SEL_BLOCK
    overlap = jnp.where((c_start < s_start + SEL_BLOCK) & (c_start + CMP_LEN > s_start),
                        1.0, 0.0).astype(BF16)
    for r_ref, pe_ref, w1_ref, w2_ref, o_ref, tail in (
            (kr_ref, kpe_ref, kw1_ref, kw2_ref, kc_ref, None),
            (vr_ref, vpe_ref, vw1_ref, vw2_ref, vc_ref, overlap)):
        pb = _dot(pe_ref[...], w1_ref[...])
        bias = pb[0:1, :CMP_HIDDEN] + pb[8:9, CMP_HIDDEN:]
        rows = [[r_ref[0, half, pl.ds(l, nch, stride=CMP_STRIDE), :] for l in range(CMP_STRIDE)]
                for half in range(KV_WIDTH // LANES)]
        for g in range(N_KV_HEADS):
            half, odd = divmod(g, LANES // HEAD_DIM)
            sl = slice(odd * HEAD_DIM, (odd + 1) * HEAD_DIM)
            xg = jnp.concatenate([rw[:, sl] for rw in rows[half]], axis=1).astype(BF16)
            lohi = _dot(xg, w1_ref[...])
            lo = lohi[:, :CMP_HIDDEN]
            hi = lohi[:, CMP_HIDDEN:]
            hi_next = jnp.concatenate([hi[1:], jnp.zeros((1, CMP_HIDDEN), F32)], axis=0)
            hid = lo + hi_next + bias
            act = (hid * _sigmoid(hid)).astype(BF16)
            out = _dot(act, w2_ref[...])
            out = jnp.where(row < nch - 1, out, 0.0)
            out = out.astype(o_ref.dtype)
            o_ref[0, g] = out if tail is None else jnp.concatenate([out, tail], axis=1)


def _compress(kc_r, vc_r, kpe, kw1, kw2, vpe, vw1, vw2):
    b, _, t, _ = kc_r.shape
    nch = t // CMP_STRIDE
    blk_len = CMP_STRIDE * HEAD_DIM
    out_sds = [jax.ShapeDtypeStruct((b, N_KV_HEADS, nch, w), BF16) for w in (HEAD_DIM, 2 * HEAD_DIM)]
    out_spec = [pl.BlockSpec((1, N_KV_HEADS, nch, w), lambda bi: (bi, 0, 0, 0))
                for w in (HEAD_DIM, 2 * HEAD_DIM)]
    raw_spec = pl.BlockSpec((1, KV_WIDTH // LANES, t, LANES), lambda bi: (bi, 0, 0, 0))
    w_specs = [_const_spec((16, blk_len)), _const_spec((blk_len, 2 * CMP_HIDDEN)),
               _const_spec((CMP_HIDDEN, HEAD_DIM))]
    return pl.pallas_call(
        _compress_kernel,
        grid=(b,),
        in_specs=[raw_spec, raw_spec] + w_specs + w_specs,
        out_specs=tuple(out_spec),
        out_shape=tuple(out_sds),
        compiler_params=pltpu.CompilerParams(
            dimension_semantics=("arbitrary",), vmem_limit_bytes=VMEM_LIMIT),
        name="compress",
    )(kc_r, vc_r, kpe, kw1, kw2, vpe, vw1, vw2)


def _nsa_kernel(slope_ref, q_ref, kc_ref, vca_ref, ksa_ref, vsa_ref, kw_ref, vwa_ref, gate_ref,
                o_ref, se_ref, so_ref, mx_ref, pa_ref, pb_ref, acc_ref,
                qa_ref, pw_ref, p4_ref, vg_ref, m1_ref, acc1_ref, list_ref):
    g = pl.program_id(1)
    ncp = kc_ref.shape[2]
    n_chunks = list_ref.shape[1] - 2
    slopes = [slope_ref[g * Q_PER_KV + r] for r in range(Q_PER_KV)]
    rows = [slice(r * TQ, (r + 1) * TQ) for r in range(Q_PER_KV)]
    halves = [slice(hf * LANES, (hf + 1) * LANES) for hf in range(TQ // LANES)]
    lane_row = lax.broadcasted_iota(jnp.int32, (1, TQ), 1)
    causal = (lax.broadcasted_iota(jnp.int32, (TQ, TQ), 1)
              <= lax.broadcasted_iota(jnp.int32, (TQ, TQ), 0))

    def main(u):
        i = pl.program_id(2) * TILES_PER_STEP + u
        t0 = i * TQ
        tile = slice(u * TQ, (u + 1) * TQ)
        q_full = q_ref[0, :, tile, :].reshape(Q_PER_KV * TQ, 2 * HEAD_DIM)
        q3 = q_full[:, :HEAD_DIM]

        s_c = _dot_nt(q3, kc_ref[0, 0])
        n_idx = lax.broadcasted_iota(jnp.int32, (TQ, ncp), 1)
        t_idx = t0 + lax.broadcasted_iota(jnp.int32, (TQ, ncp), 0)
        c_valid = n_idx * CMP_STRIDE + (CMP_LEN - 1) <= t_idx
        n_row = lax.broadcasted_iota(jnp.int32, (1, ncp), 1)
        c_pos = (n_row * CMP_STRIDE - t0).astype(F32) + 0.5 * (CMP_LEN - 1)
        vca = vca_ref[0, 0]
        o_c = []
        imp = jnp.zeros((TQ, LANES), F32)
        for r in range(Q_PER_KV):
            sr = jnp.where(c_valid, s_c[rows[r]] + slopes[r] * c_pos, -jnp.inf)
            m = jnp.max(sr, axis=1, keepdims=True)
            m = jnp.where(m == -jnp.inf, 0.0, m)
            p = jnp.exp2(sr - m)
            inv = 1.0 / jnp.maximum(jnp.sum(p, axis=1, keepdims=True), 1e-30)
            pv = _dot(p.astype(BF16), vca) * inv
            o_c.append(pv[:, :HEAD_DIM])
            imp = imp + pv

        imp_t = imp.T[HEAD_DIM:, :]
        j_t = lax.broadcasted_iota(jnp.int32, (LANES - HEAD_DIM, TQ), 0)
        cur_t = (t0 + lax.broadcasted_iota(jnp.int32, (LANES - HEAD_DIM, TQ), 1)) // SEL_BLOCK
        forced = (j_t == 0) | (j_t == cur_t) | (j_t == cur_t - 1)
        cand = jnp.where(j_t <= cur_t, imp_t + jnp.where(forced, FORCED_BONUS, 0.0), -jnp.inf)
        sel = jnp.zeros(cand.shape, jnp.bool_)
        for _ in range(SEL_TOPK):
            mx = jnp.max(cand, axis=0, keepdims=True)
            first = jnp.min(jnp.where(cand == mx, j_t, LANES), axis=0, keepdims=True)
            hit = j_t == first
            sel = sel | (hit & (mx > -jnp.inf))
            cand = jnp.where(hit, -jnp.inf, cand)
        bias_t = jnp.concatenate([jnp.zeros(cand.shape, F32), jnp.where(sel, 0.0, MASK_NEG)], axis=0)
        sel_bias = bias_t.T
        sel_bias_bf = sel_bias.astype(BF16)
        for r in range(Q_PER_KV):
            qa_ref[u, rows[r], :] = q_full[rows[r]] + sel_bias_bf

        blk_any = jnp.broadcast_to(jnp.max(sel_bias, axis=0, keepdims=True), (8, LANES))
        per_chunk = TQ // SEL_BLOCK
        chunk_any = blk_any
        for d in range(1, per_chunk):
            chunk_any = jnp.maximum(chunk_any, pltpu.roll(blk_any, LANES - d, 1))
        blk8 = lax.broadcasted_iota(jnp.int32, (8, LANES), 1) - HEAD_DIM
        chunk_bit = jnp.where((blk8 >= 0) & (blk8 % per_chunk == 0),
                              jnp.left_shift(1, jnp.maximum(blk8, 0) // per_chunk), 0).astype(F32)
        flagged = jnp.where(chunk_any > 0.5 * MASK_NEG, chunk_bit, 0.0)
        bits = jnp.sum(flagged[0:1, :], axis=1, keepdims=True)[0, 0].astype(jnp.int32)
        cnt = jnp.int32(0)
        idle = jnp.int32(0)
        for c in range(n_chunks):
            below = c < i
            hit = (((bits >> c) & 1) == 1) & below
            list_ref[u, cnt] = jnp.int32(c)
            cnt = cnt + hit.astype(jnp.int32)
            idle = jnp.where(below & jnp.logical_not(hit), c, idle)
        filler = jnp.where(i + 1 < n_chunks, i + 1, idle)
        list_ref[u, cnt] = filler

        w0 = pl.multiple_of(jnp.maximum(t0 - WINDOW, 0), TQ)
        s_w = _dot_nt(q3, kw_ref[0, 0, pl.ds(w0, WIN_KEYS), :])
        wk_row = w0 - t0 + lax.broadcasted_iota(jnp.int32, (1, WIN_KEYS), 1)
        dist = (lax.broadcasted_iota(jnp.int32, (TQ, WIN_KEYS), 0)
                - (w0 - t0) - lax.broadcasted_iota(jnp.int32, (TQ, WIN_KEYS), 1))
        w_valid = (dist >= 0) & (dist < WINDOW)
        wk_pos = wk_row.astype(F32)
        for r in range(Q_PER_KV):
            sr = jnp.where(w_valid, s_w[rows[r]] + slopes[r] * wk_pos, MASK_NEG)
            m = jnp.max(sr, axis=1, keepdims=True)
            pw_ref[u, rows[r], :] = jnp.exp2(sr - m).astype(BF16)
        acc_w = _dot(pw_ref[u], vwa_ref[0, 0, pl.ds(w0, WIN_KEYS), :])

        gates = gate_ref[0, tile, :]
        lane = lax.broadcasted_iota(jnp.int32, (TQ, LANES), 1)
        partial, g_sel = [], []
        for r in range(Q_PER_KV):
            col = g * (Q_PER_KV * N_BRANCH) + r * N_BRANCH
            gc, gs, gw = (jnp.sum(jnp.where(lane == col + br, gates, 0.0), axis=1, keepdims=True)
                          for br in range(N_BRANCH))
            a = acc_w[rows[r]]
            o_w = a[:, :HEAD_DIM] / a[:, HEAD_DIM:HEAD_DIM + 1]
            partial.append(gc * o_c[r] + gw * o_w)
            g_sel.append(jnp.broadcast_to(gs, (TQ, HEAD_DIM)))

        head = [jnp.where(cnt > j, list_ref[u, j], filler) for j in range(SHORT_LIST - 1)] + [i]
        scores = []
        for j, c in enumerate(head):
            k0 = pl.multiple_of(c * TQ, TQ)
            vg_ref[u, j * TQ:(j + 1) * TQ, :] = vsa_ref[0, 0, pl.ds(k0, TQ), :]
            scores.append(_dot_nt(qa_ref[u], ksa_ref[0, 0, pl.ds(k0, TQ), :]))
        for r in range(Q_PER_KV):
            biased = []
            for j, c in enumerate(head):
                sr = scores[j][rows[r]] + slopes[r] * (c * TQ - t0 + lane_row).astype(F32)
                biased.append(jnp.where(causal, sr, MASK_NEG) if j == SHORT_LIST - 1 else sr)
            part = biased[0][:, halves[0]]
            for sr in biased:
                for hf in halves:
                    part = jnp.maximum(part, sr[:, hf])
            m = jnp.max(part, axis=1, keepdims=True)
            m1_ref[u, rows[r], :] = jnp.broadcast_to(m, (TQ, LANES))
            for j, sr in enumerate(biased):
                p4_ref[u, rows[r], j * TQ:(j + 1) * TQ] = jnp.exp2(sr - m).astype(BF16)
        acc1_ref[u] = _dot(p4_ref[u], vg_ref[u])
        return t0, cnt, partial, g_sel

    def rest_of_list(u, t0, cnt):
        k_last = (cnt - SHORT_LIST + 2) // 2 - 1

        def entry(pos):
            return list_ref[u, SHORT_LIST - 1 + pos]

        def raw_scores(pos, dst_ref, k):
            k0 = pl.multiple_of(entry(pos) * TQ, TQ)
            s_all = _dot_nt(qa_ref[u], ksa_ref[0, 0, pl.ds(k0, TQ), :])
            for r in range(Q_PER_KV):
                dst_ref[k, rows[r], :] = s_all[rows[r]]

        def alibi_row(pos, r):
            return slopes[r] * (entry(pos) * TQ - t0 + lane_row).astype(F32)

        def fold_max(pos, src_ref, k):
            for r in range(Q_PER_KV):
                sr = src_ref[k, rows[r], :] + alibi_row(pos, r)
                part = sr[:, halves[0]]
                for hf in halves[1:]:
                    part = jnp.maximum(part, sr[:, hf])
                mx_ref[rows[r], :] = jnp.maximum(mx_ref[rows[r], :], part)

        mx_ref[...] = jnp.full(mx_ref.shape, M_INIT, F32)
        raw_scores(0, se_ref, 0)

        def pass1(k, carry):
            raw_scores(2 * k + 1, so_ref, k)
            fold_max(2 * k, se_ref, k)
            raw_scores(2 * k + 2, se_ref, k + 1)
            fold_max(2 * k + 1, so_ref, k)
            return carry

        lax.fori_loop(0, k_last, pass1, 0)
        raw_scores(2 * k_last + 1, so_ref, k_last)
        fold_max(2 * k_last, se_ref, k_last)
        fold_max(2 * k_last + 1, so_ref, k_last)

        for r in range(Q_PER_KV):
            m = jnp.max(mx_ref[rows[r], :], axis=1, keepdims=True)
            mx_ref[rows[r], :] = jnp.broadcast_to(m, (TQ, LANES))
        acc_ref[...] = jnp.zeros(acc_ref.shape, F32)

        def probs(pos, src_ref, k, dst_ref):
            for r in range(Q_PER_KV):
                mb = mx_ref[rows[r], :]
                bias = alibi_row(pos, r)
                for hf in halves:
                    dst_ref[rows[r], hf] = jnp.exp2(src_ref[k, rows[r], hf] + bias[:, hf] - mb).astype(BF16)

        def add_pv(pos, src_ref):
            k0 = pl.multiple_of(entry(pos) * TQ, TQ)
            acc_ref[...] += _dot(src_ref[...], vsa_ref[0, 0, pl.ds(k0, TQ), :])

        probs(0, se_ref, 0, pa_ref)

        def pass2(k, carry):
            add_pv(2 * k, pa_ref)
            probs(2 * k + 1, so_ref, k, pb_ref)
            add_pv(2 * k + 1, pb_ref)
            probs(2 * k + 2, se_ref, k + 1, pa_ref)
            return carry

        lax.fori_loop(0, k_last, pass2, 0)
        add_pv(2 * k_last, pa_ref)
        probs(2 * k_last + 1, so_ref, k_last, pb_ref)
        add_pv(2 * k_last + 1, pb_ref)

        for r in range(Q_PER_KV):
            m_head = m1_ref[u, rows[r], :]
            m_rest = mx_ref[rows[r], :]
            m = jnp.maximum(m_head, m_rest)
            acc1_ref[u, rows[r], :] = (acc1_ref[u, rows[r], :] * jnp.exp2(m_head - m)
                                       + acc_ref[rows[r], :] * jnp.exp2(m_rest - m))

    states = [main(u) for u in range(TILES_PER_STEP)]
    for u, (t0, cnt, _, _) in enumerate(states):
        pl.when(cnt > SHORT_LIST - 1)(functools.partial(rest_of_list, u, t0, cnt))
    for u, (_, _, partial, g_sel) in enumerate(states):
        outs = []
        for r in range(Q_PER_KV):
            a = acc1_ref[u, rows[r], :]
            o_s = a[:, :HEAD_DIM] / a[:, HEAD_DIM:HEAD_DIM + 1]
            outs.append(partial[r] + g_sel[r] * o_s)
        o_ref[0, 0, u * TQ:(u + 1) * TQ, :] = jnp.concatenate(outs, axis=1).astype(o_ref.dtype)


def _nsa(q, kc, vca, ksa, vsa, kw, vwa, gates, slopes):
    b, _, t, _ = q.shape
    ncp = kc.shape[2]
    tstep = TILES_PER_STEP * TQ
    grp = lambda bi, g, i: (bi, g, 0, 0)
    per_tile = lambda shape, dtype: pltpu.VMEM((TILES_PER_STEP,) + shape, dtype)
    return pl.pallas_call(
        _nsa_kernel,
        grid=(b, N_KV_HEADS, t // tstep),
        in_specs=[
            pl.BlockSpec(memory_space=pltpu.SMEM),
            pl.BlockSpec((1, Q_PER_KV, tstep, 2 * HEAD_DIM), lambda bi, g, i: (bi, g, i, 0)),
            pl.BlockSpec((1, 1, ncp, HEAD_DIM), grp),
            pl.BlockSpec((1, 1, ncp, 2 * HEAD_DIM), grp),
            pl.BlockSpec((1, 1, t, 2 * HEAD_DIM), grp),
            pl.BlockSpec((1, 1, t, 2 * HEAD_DIM), grp),
            pl.BlockSpec((1, 1, t, HEAD_DIM), grp),
            pl.BlockSpec((1, 1, t, 2 * HEAD_DIM), grp),
            pl.BlockSpec((1, tstep, GATE_PAD), lambda bi, g, i: (bi, i, 0)),
        ],
        out_specs=pl.BlockSpec((1, 1, tstep, Q_PER_KV * HEAD_DIM), lambda bi, g, i: (bi, g, i, 0)),
        out_shape=jax.ShapeDtypeStruct((b, N_KV_HEADS, t, Q_PER_KV * HEAD_DIM), BF16),
        scratch_shapes=[
            pltpu.VMEM((t // (2 * TQ), Q_PER_KV * TQ, TQ), F32),
            pltpu.VMEM((t // (2 * TQ), Q_PER_KV * TQ, TQ), F32),
            pltpu.VMEM((Q_PER_KV * TQ, LANES), F32),
            pltpu.VMEM((Q_PER_KV * TQ, TQ), BF16),
            pltpu.VMEM((Q_PER_KV * TQ, TQ), BF16),
            pltpu.VMEM((Q_PER_KV * TQ, LANES), F32),
            per_tile((Q_PER_KV * TQ, 2 * HEAD_DIM), BF16),
            per_tile((Q_PER_KV * TQ, WIN_KEYS), BF16),
            per_tile((Q_PER_KV * TQ, SHORT_LIST * TQ), BF16),
            per_tile((SHORT_LIST * TQ, 2 * HEAD_DIM), BF16),
            per_tile((Q_PER_KV * TQ, LANES), F32),
            per_tile((Q_PER_KV * TQ, LANES), F32),
            pltpu.SMEM((TILES_PER_STEP, t // TQ + 2), jnp.int32),
        ],
        compiler_params=pltpu.CompilerParams(
            dimension_semantics=("arbitrary", "arbitrary", "arbitrary"),
            vmem_limit_bytes=VMEM_LIMIT),
        name="nsa_attention",
    )(slopes, q, kc, vca, ksa, vsa, kw, vwa, gates)


def _outffn_kernel(x_ref, c_ref, n_ref, woc_ref, won_ref, fg_ref, wgu_ref, wd_ref, fin_ref,
                   o_ref, acc_ref, h_ref, act_ref, *, final):
    x1 = x_ref[0] + _dot(c_ref[0], woc_ref[...])
    for g in range(N_KV_HEADS):
        x1 = x1 + _dot(n_ref[0, g], won_ref[g])
    ms = jnp.mean(x1 * x1, axis=-1, keepdims=True)
    h_ref[...] = (x1 * lax.rsqrt(ms + EPS) * fg_ref[...]).astype(BF16)
    acc_ref[...] = x1

    for c in range(N_FF_CHUNKS):
        h = h_ref[...]
        gate = _dot(h, wgu_ref[:, c * FF_CHUNK:(c + 1) * FF_CHUNK])
        up = _dot(h, wgu_ref[:, D_FF + c * FF_CHUNK:D_FF + (c + 1) * FF_CHUNK])
        act_ref[:, c * FF_CHUNK:(c + 1) * FF_CHUNK] = (gate * _sigmoid(gate) * up).astype(BF16)
    y = acc_ref[...] + _dot(act_ref[...], wd_ref[...])
    if final:
        ms = jnp.mean(y * y, axis=-1, keepdims=True)
        y = y * lax.rsqrt(ms + EPS) * fin_ref[...]
    o_ref[0] = y


def _outffn(x, conv_out, nsa_out, wo_c, wo_n, ffn_g, wgu, wd, final_g, final):
    b, t, _ = x.shape
    tm = min(TM_PROJ, t)
    tok = lambda bi, i: (bi, i, 0)
    return pl.pallas_call(
        functools.partial(_outffn_kernel, final=final),
        grid=(b, t // tm),
        in_specs=[
            pl.BlockSpec((1, tm, D_MODEL), tok),
            pl.BlockSpec((1, tm, CONV_CH), tok),
            pl.BlockSpec((1, N_KV_HEADS, tm, Q_PER_KV * HEAD_DIM), lambda bi, i: (bi, 0, i, 0)),
            _const_spec((CONV_CH, D_MODEL)),
            _const_spec((N_KV_HEADS, Q_PER_KV * HEAD_DIM, D_MODEL)),
            _const_spec((1, D_MODEL)),
            _const_spec((D_MODEL, 2 * D_FF)),
            _const_spec((D_FF, D_MODEL)),
            _const_spec((1, D_MODEL)),
        ],
        out_specs=pl.BlockSpec((1, tm, D_MODEL), tok),
        out_shape=jax.ShapeDtypeStruct((b, t, D_MODEL), F32),
        scratch_shapes=[pltpu.VMEM((tm, D_MODEL), F32), pltpu.VMEM((tm, D_MODEL), BF16),
                        pltpu.VMEM((tm, D_FF), BF16)],
        compiler_params=pltpu.CompilerParams(
            dimension_semantics=("arbitrary", "arbitrary"), vmem_limit_bytes=VMEM_LIMIT),
        name="outproj_ffn",
    )(x, conv_out, nsa_out, wo_c, wo_n, ffn_g, wgu, wd, final_g)


def _pe_rows(pe):
    lo = pe[:CMP_STRIDE].reshape(1, -1)
    hi = pe[CMP_STRIDE:].reshape(1, -1)
    z = jnp.zeros((7, lo.shape[1]), pe.dtype)
    return jnp.concatenate([lo, z, hi, z], axis=0).astype(BF16)


def _w1_cat(w1):
    lo = w1[:CMP_STRIDE].reshape(CMP_STRIDE * HEAD_DIM, CMP_HIDDEN)
    hi = w1[CMP_STRIDE:].reshape(CMP_STRIDE * HEAD_DIM, CMP_HIDDEN)
    return jnp.concatenate([lo, hi], axis=1).astype(BF16)


def kernel(x, attn_norm, w_in, conv_w, conv_b, conv_ln_g, conv_ln_b, cmp_k_pe, cmp_k_w1, cmp_k_w2,
           cmp_v_pe, cmp_v_w1, cmp_v_w2, w_out, ffn_norm, w_gate_up, w_down, final_norm):
    depth = w_in.shape[0]
    t = x.shape[1]
    assert t % (2 * TQ) == 0 and t % (TILES_PER_STEP * TQ) == 0
    assert t // SEL_BLOCK <= LANES - HEAD_DIM and t >= WIN_KEYS
    slopes = jnp.asarray(_alibi_slopes(N_Q_HEADS) * np.float32(LOG2E))
    final_g = final_norm.reshape(1, D_MODEL)
    for l in range(depth):
        w_pad = jnp.pad(w_in[l], ((0, 0), (0, IN_COLS_PAD - w_in.shape[2]))).astype(BF16)
        a, q, kc_r, vc_r, ksa, vsa, kw, vwa, gates = _inproj(
            x, attn_norm[l].reshape(1, D_MODEL), w_pad)
        conv_out = _conv_mixer(
            a, jnp.pad(conv_w[l], ((0, CONV_HALO - CONV_WIDTH), (0, 0))),
            conv_b[l].reshape(1, CONV_CH), conv_ln_g[l].reshape(1, CONV_CH),
            conv_ln_b[l].reshape(1, CONV_CH))
        kc, vca = _compress(kc_r, vc_r,
                            _pe_rows(cmp_k_pe[l]), _w1_cat(cmp_k_w1[l]), cmp_k_w2[l].astype(BF16),
                            _pe_rows(cmp_v_pe[l]), _w1_cat(cmp_v_w1[l]), cmp_v_w2[l].astype(BF16))
        nsa_out = _nsa(q, kc, vca, ksa, vsa, kw, vwa, gates, slopes)
        wo = w_out[l].astype(BF16)
        wo_n = wo[CONV_CH:].reshape(N_KV_HEADS, Q_PER_KV * HEAD_DIM, D_MODEL)
        x = _outffn(x, conv_out, nsa_out, wo[:CONV_CH], wo_n, ffn_norm[l].reshape(1, D_MODEL),
                    w_gate_up[l].astype(BF16), w_down[l].astype(BF16), final_g,
                    final=(l == depth - 1))
    return x
```

```python
import functools
import math

import jax
import jax.numpy as jnp
import numpy as np
from jax import lax
from jax.experimental import pallas as pl
from jax.experimental.pallas import tpu as pltpu

F32 = jnp.float32
BF16 = jnp.bfloat16

D_MODEL = 1024
HEAD_DIM = 64
CONV_CH = 256
CONV_WIDTH = 31
N_Q_HEADS = 12
N_KV_HEADS = 4
Q_PER_KV = N_Q_HEADS // N_KV_HEADS
NSA_WIDTH = N_Q_HEADS * HEAD_DIM
KV_WIDTH = N_KV_HEADS * HEAD_DIM
CMP_LEN = 32
CMP_STRIDE = 16
CMP_HIDDEN = 256
SEL_BLOCK = 64
SEL_TOPK = 8
WINDOW = 512
N_BRANCH = 3
FORCED_BONUS = 1000.0
D_FF = 2816
EPS = 1e-6
LOG2E = math.log2(math.e)

LANES = 128
SUBLANES = 8
GATE_PAD = LANES
FF_CHUNK = 256
N_FF_CHUNKS = D_FF // FF_CHUNK
TM_PROJ = 512
TC_CONV = 512
CONV_HALO = 32
TQ = 256
WIN_KEYS = WINDOW + TQ
SHORT_LIST = 4
TILES_PER_STEP = 4
MASK_NEG = -(2.0 ** 100)
M_INIT = -1e30
VMEM_LIMIT = 56 * 1024 * 1024


def _alibi_slopes(n):
    def pow2_slopes(m):
        start = 2.0 ** (-8.0 / m)
        return [start ** (i + 1) for i in range(m)]
    if math.log2(n).is_integer():
        s = pow2_slopes(n)
    else:
        c = 2 ** math.floor(math.log2(n))
        s = pow2_slopes(c) + pow2_slopes(2 * c)[0::2][: n - c]
    return np.asarray(s, dtype=np.float32)


def _sigmoid(v):
    return 1.0 / (1.0 + jnp.exp(-v))


def _dot(a, b):
    return jnp.dot(a, b, preferred_element_type=F32)


def _dot_nt(a, b):
    return lax.dot_general(a, b, (((1,), (1,)), ((), ())), preferred_element_type=F32)


def _const_spec(shape):
    nd = len(shape)
    return pl.BlockSpec(shape, lambda *_: (0,) * nd, pipeline_mode=pl.Buffered(1))


C_A = 0
C_Q = C_A + 2 * CONV_CH
C_KC = C_Q + NSA_WIDTH
C_VC = C_KC + KV_WIDTH
C_KS = C_VC + KV_WIDTH
C_VS = C_KS + KV_WIDTH
C_KW = C_VS + KV_WIDTH
C_VW = C_KW + KV_WIDTH
C_G = C_VW + KV_WIDTH
IN_COLS_PAD = C_G + GATE_PAD


def _inproj_kernel(x_ref, g_ref, w_ref, a_ref, q_ref, kcr_ref, vcr_ref, ksa_ref, vsa_ref, kw_ref,
                   vwa_ref, gate_ref):
    tm = x_ref.shape[1]
    x = x_ref[0]
    ms = jnp.mean(x * x, axis=-1, keepdims=True)
    h = (x * lax.rsqrt(ms + EPS) * g_ref[...]).astype(BF16)

    a_ref[0] = _dot(h, w_ref[:, C_A:C_Q])
    zq = _dot(h, w_ref[:, C_Q:C_KC]) * (HEAD_DIM ** -0.5 * LOG2E)
    zero_hi = jnp.zeros((tm, HEAD_DIM), BF16)
    for hd in range(N_Q_HEADS):
        qh = zq[:, hd * HEAD_DIM:(hd + 1) * HEAD_DIM].astype(BF16)
        q_ref[0, hd] = jnp.concatenate([qh, zero_hi], axis=1)
    zkc = _dot(h, w_ref[:, C_KC:C_VC])
    zvc = _dot(h, w_ref[:, C_VC:C_KS])
    for half in range(KV_WIDTH // LANES):
        kcr_ref[0, half] = zkc[:, half * LANES:(half + 1) * LANES]
        vcr_ref[0, half] = zvc[:, half * LANES:(half + 1) * LANES]

    t = pl.program_id(1) * tm + lax.broadcasted_iota(jnp.int32, (tm, HEAD_DIM), 0)
    blk = lax.broadcasted_iota(jnp.int32, (tm, HEAD_DIM), 1)
    onehot = jnp.where(t // SEL_BLOCK == blk, 1.0, 0.0).astype(BF16)
    ones_col = jnp.where(blk == 0, 1.0, 0.0).astype(BF16)
    zks = _dot(h, w_ref[:, C_KS:C_VS])
    zvs = _dot(h, w_ref[:, C_VS:C_KW])
    zkw = _dot(h, w_ref[:, C_KW:C_VW])
    zvw = _dot(h, w_ref[:, C_VW:C_G])
    for g in range(N_KV_HEADS):
        sl = slice(g * HEAD_DIM, (g + 1) * HEAD_DIM)
        ksa_ref[0, g] = jnp.concatenate([zks[:, sl].astype(BF16), onehot], axis=1)
        vsa_ref[0, g] = jnp.concatenate([zvs[:, sl].astype(BF16), ones_col], axis=1)
        kw_ref[0, g] = zkw[:, sl].astype(BF16)
        vwa_ref[0, g] = jnp.concatenate([zvw[:, sl].astype(BF16), ones_col], axis=1)
    gate_ref[0] = _sigmoid(_dot(h, w_ref[:, C_G:IN_COLS_PAD]))


def _inproj(x, norm_g, w_pad):
    b, t, _ = x.shape
    tm = min(TM_PROJ, t)
    grid = (b, t // tm)
    tok = lambda bi, i: (bi, i, 0)
    head = lambda bi, i: (bi, 0, i, 0)
    out_shape = (
        jax.ShapeDtypeStruct((b, t, 2 * CONV_CH), F32),
        jax.ShapeDtypeStruct((b, N_Q_HEADS, t, 2 * HEAD_DIM), BF16),
        jax.ShapeDtypeStruct((b, KV_WIDTH // LANES, t, LANES), F32),
        jax.ShapeDtypeStruct((b, KV_WIDTH // LANES, t, LANES), F32),
        jax.ShapeDtypeStruct((b, N_KV_HEADS, t, 2 * HEAD_DIM), BF16),
        jax.ShapeDtypeStruct((b, N_KV_HEADS, t, 2 * HEAD_DIM), BF16),
        jax.ShapeDtypeStruct((b, N_KV_HEADS, t, HEAD_DIM), BF16),
        jax.ShapeDtypeStruct((b, N_KV_HEADS, t, 2 * HEAD_DIM), BF16),
        jax.ShapeDtypeStruct((b, t, GATE_PAD), F32),
    )
    out_specs = (
        pl.BlockSpec((1, tm, 2 * CONV_CH), tok),
        pl.BlockSpec((1, N_Q_HEADS, tm, 2 * HEAD_DIM), head),
        pl.BlockSpec((1, KV_WIDTH // LANES, tm, LANES), head),
        pl.BlockSpec((1, KV_WIDTH // LANES, tm, LANES), head),
        pl.BlockSpec((1, N_KV_HEADS, tm, 2 * HEAD_DIM), head),
        pl.BlockSpec((1, N_KV_HEADS, tm, 2 * HEAD_DIM), head),
        pl.BlockSpec((1, N_KV_HEADS, tm, HEAD_DIM), head),
        pl.BlockSpec((1, N_KV_HEADS, tm, 2 * HEAD_DIM), head),
        pl.BlockSpec((1, tm, GATE_PAD), tok),
    )
    return pl.pallas_call(
        _inproj_kernel,
        grid=grid,
        in_specs=[
            pl.BlockSpec((1, tm, D_MODEL), tok),
            _const_spec((1, D_MODEL)),
            _const_spec((D_MODEL, IN_COLS_PAD)),
        ],
        out_specs=out_specs,
        out_shape=out_shape,
        compiler_params=pltpu.CompilerParams(
            dimension_semantics=("arbitrary", "arbitrary"), vmem_limit_bytes=VMEM_LIMIT),
        name="inproj",
    )(x, norm_g, w_pad)


def _conv_kernel(a_ref, halo_ref, w_ref, b_ref, lg_ref, lb_ref, o_ref, y_ref):
    tc = a_ref.shape[1]
    am = a_ref[0]
    ah = halo_ref[0]
    ym = am[:, :CONV_CH] * _sigmoid(am[:, CONV_CH:])
    yh = ah[:, :CONV_CH] * _sigmoid(ah[:, CONV_CH:])
    yh = jnp.where(pl.program_id(1) == 0, 0.0, yh)
    y_ref[0:CONV_HALO, :] = yh
    y_ref[CONV_HALO:CONV_HALO + tc, :] = ym
    first = CONV_HALO - (CONV_WIDTH - 1)
    acc = b_ref[...]
    for rho in range(SUBLANES):
        n_rows = tc if rho == 0 else tc + SUBLANES
        z = None
        for base in range(0, CONV_HALO + 1, SUBLANES):
            k = base + rho - first
            if 0 <= k < CONV_WIDTH:
                term = w_ref[k:k + 1, :] * y_ref[pl.ds(base, n_rows), :]
                z = term if z is None else z + term
        acc = acc + (z if rho == 0 else z[rho:rho + tc, :])
    mu = jnp.mean(acc, axis=-1, keepdims=True)
    d = acc - mu
    var = jnp.mean(d * d, axis=-1, keepdims=True)
    yn = d * lax.rsqrt(var + EPS) * lg_ref[...] + lb_ref[...]
    o_ref[0] = (yn * _sigmoid(yn)).astype(o_ref.dtype)


def _conv_mixer(a, w_dw, b_dw, ln_g, ln_b):
    b, t, _ = a.shape
    tc = min(TC_CONV, t)
    per = tc // CONV_HALO
    return pl.pallas_call(
        _conv_kernel,
        grid=(b, t // tc),
        in_specs=[
            pl.BlockSpec((1, tc, 2 * CONV_CH), lambda bi, i: (bi, i, 0)),
            pl.BlockSpec((1, CONV_HALO, 2 * CONV_CH),
                         lambda bi, i: (bi, jnp.maximum(i * per - 1, 0), 0)),
            _const_spec((CONV_HALO, CONV_CH)),
            _const_spec((1, CONV_CH)),
            _const_spec((1, CONV_CH)),
            _const_spec((1, CONV_CH)),
        ],
        out_specs=pl.BlockSpec((1, tc, CONV_CH), lambda bi, i: (bi, i, 0)),
        out_shape=jax.ShapeDtypeStruct((b, t, CONV_CH), BF16),
        scratch_shapes=[pltpu.VMEM((CONV_HALO + tc, CONV_CH), F32)],
        compiler_params=pltpu.CompilerParams(
            dimension_semantics=("arbitrary", "arbitrary"), vmem_limit_bytes=VMEM_LIMIT),
        name="conv_mixer",
    )(a, a, w_dw, b_dw, ln_g, ln_b)


def _compress_kernel(kr_ref, vr_ref, kpe_ref, kw1_ref, kw2_ref, vpe_ref, vw1_ref, vw2_ref,
                     kc_ref, vc_ref):
    nch = kr_ref.shape[2] // CMP_STRIDE
    row = lax.broadcasted_iota(jnp.int32, (nch, HEAD_DIM), 0)
    c_start = row * CMP_STRIDE
    s_start = lax.broadcasted_iota(jnp.int32, (nch, HEAD_DIM), 1) * SEL_BLOCK
    overlap = jnp.where((c_start < s_start + SEL_BLOCK) & (c_start + CMP_LEN > s_start),
                        1.0, 0.0).astype(BF16)
    for r_ref, pe_ref, w1_ref, w2_ref, o_ref, tail in (
            (kr_ref, kpe_ref, kw1_ref, kw2_ref, kc_ref, None),
            (vr_ref, vpe_ref, vw1_ref, vw2_ref, vc_ref, overlap)):
        pb = _dot(pe_ref[...], w1_ref[...])
        bias = pb[0:1, :CMP_HIDDEN] + pb[8:9, CMP_HIDDEN:]
        rows = [[r_ref[0, half, pl.ds(l, nch, stride=CMP_STRIDE), :] for l in range(CMP_STRIDE)]
                for half in range(KV_WIDTH // LANES)]
        for g in range(N_KV_HEADS):
            half, odd = divmod(g, LANES // HEAD_DIM)
            sl = slice(odd * HEAD_DIM, (odd + 1) * HEAD_DIM)
            xg = jnp.concatenate([rw[:, sl] for rw in rows[half]], axis=1).astype(BF16)
            lohi = _dot(xg, w1_ref[...])
            lo = lohi[:, :CMP_HIDDEN]
            hi = lohi[:, CMP_HIDDEN:]
            hi_next = jnp.concatenate([hi[1:], jnp.zeros((1, CMP_HIDDEN), F32)], axis=0)
            hid = lo + hi_next + bias
            act = (hid * _sigmoid(hid)).astype(BF16)
            out = _dot(act, w2_ref[...])
            out = jnp.where(row < nch - 1, out, 0.0)
            out = out.astype(o_ref.dtype)
            o_ref[0, g] = out if tail is None else jnp.concatenate([out, tail], axis=1)


def _compress(kc_r, vc_r, kpe, kw1, kw2, vpe, vw1, vw2):
    b, _, t, _ = kc_r.shape
    nch = t // CMP_STRIDE
    blk_len = CMP_STRIDE * HEAD_DIM
    out_sds = [jax.ShapeDtypeStruct((b, N_KV_HEADS, nch, w), BF16) for w in (HEAD_DIM, 2 * HEAD_DIM)]
    out_spec = [pl.BlockSpec((1, N_KV_HEADS, nch, w), lambda bi: (bi, 0, 0, 0))
                for w in (HEAD_DIM, 2 * HEAD_DIM)]
    raw_spec = pl.BlockSpec((1, KV_WIDTH // LANES, t, LANES), lambda bi: (bi, 0, 0, 0))
    w_specs = [_const_spec((16, blk_len)), _const_spec((blk_len, 2 * CMP_HIDDEN)),
               _const_spec((CMP_HIDDEN, HEAD_DIM))]
    return pl.pallas_call(
        _compress_kernel,
        grid=(b,),
        in_specs=[raw_spec, raw_spec] + w_specs + w_specs,
        out_specs=tuple(out_spec),
        out_shape=tuple(out_sds),
        compiler_params=pltpu.CompilerParams(
            dimension_semantics=("arbitrary",), vmem_limit_bytes=VMEM_LIMIT),
        name="compress",
    )(kc_r, vc_r, kpe, kw1, kw2, vpe, vw1, vw2)


def _nsa_kernel(slope_ref, q_ref, kc_ref, vca_ref, ksa_ref, vsa_ref, kw_ref, vwa_ref, gate_ref,
                o_ref, se_ref, so_ref, mx_ref, pa_ref, pb_ref, acc_ref,
                qa_ref, pw_ref, p4_ref, vg_ref, m1_ref, acc1_ref, list_ref):
    g = pl.program_id(1)
    ncp = kc_ref.shape[2]
    n_chunks = list_ref.shape[1] - 2
    slopes = [slope_ref[g * Q_PER_KV + r] for r in range(Q_PER_KV)]
    rows = [slice(r * TQ, (r + 1) * TQ) for r in range(Q_PER_KV)]
    halves = [slice(hf * LANES, (hf + 1) * LANES) for hf in range(TQ // LANES)]
    lane_row = lax.broadcasted_iota(jnp.int32, (1, TQ), 1)
    causal = (lax.broadcasted_iota(jnp.int32, (TQ, TQ), 1)
              <= lax.broadcasted_iota(jnp.int32, (TQ, TQ), 0))

    def main(u):
        i = pl.program_id(2) * TILES_PER_STEP + u
        t0 = i * TQ
        tile = slice(u * TQ, (u + 1) * TQ)
        q_full = q_ref[0, :, tile, :].reshape(Q_PER_KV * TQ, 2 * HEAD_DIM)
        q3 = q_full[:, :HEAD_DIM]

        s_c = _dot_nt(q3, kc_ref[0, 0])
        n_idx = lax.broadcasted_iota(jnp.int32, (TQ, ncp), 1)
        t_idx = t0 + lax.broadcasted_iota(jnp.int32, (TQ, ncp), 0)
        c_valid = n_idx * CMP_STRIDE + (CMP_LEN - 1) <= t_idx
        n_row = lax.broadcasted_iota(jnp.int32, (1, ncp), 1)
        c_pos = (n_row * CMP_STRIDE - t0).astype(F32) + 0.5 * (CMP_LEN - 1)
        vca = vca_ref[0, 0]
        o_c = []
        imp = jnp.zeros((TQ, LANES), F32)
        for r in range(Q_PER_KV):
            sr = jnp.where(c_valid, s_c[rows[r]] + slopes[r] * c_pos, -jnp.inf)
            m = jnp.max(sr, axis=1, keepdims=True)
            m = jnp.where(m == -jnp.inf, 0.0, m)
            p = jnp.exp2(sr - m)
            inv = 1.0 / jnp.maximum(jnp.sum(p, axis=1, keepdims=True), 1e-30)
            pv = _dot(p.astype(BF16), vca) * inv
            o_c.append(pv[:, :HEAD_DIM])
            imp = imp + pv

        imp_t = imp.T[HEAD_DIM:, :]
        j_t = lax.broadcasted_iota(jnp.int32, (LANES - HEAD_DIM, TQ), 0)
        cur_t = (t0 + lax.broadcasted_iota(jnp.int32, (LANES - HEAD_DIM, TQ), 1)) // SEL_BLOCK
        forced = (j_t == 0) | (j_t == cur_t) | (j_t == cur_t - 1)
        cand = jnp.where(j_t <= cur_t, imp_t + jnp.where(forced, FORCED_BONUS, 0.0), -jnp.inf)
        sel = jnp.zeros(cand.shape, jnp.bool_)
        for _ in range(SEL_TOPK):
            mx = jnp.max(cand, axis=0, keepdims=True)
            first = jnp.min(jnp.where(cand == mx, j_t, LANES), axis=0, keepdims=True)
            hit = j_t == first
            sel = sel | (hit & (mx > -jnp.inf))
            cand = jnp.where(hit, -jnp.inf, cand)
        bias_t = jnp.concatenate([jnp.zeros(cand.shape, F32), jnp.where(sel, 0.0, MASK_NEG)], axis=0)
        sel_bias = bias_t.T
        sel_bias_bf = sel_bias.astype(BF16)
        for r in range(Q_PER_KV):
            qa_ref[u, rows[r], :] = q_full[rows[r]] + sel_bias_bf

        blk_any = jnp.broadcast_to(jnp.max(sel_bias, axis=0, keepdims=True), (8, LANES))
        per_chunk = TQ // SEL_BLOCK
        chunk_any = blk_any
        for d in range(1, per_chunk):
            chunk_any = jnp.maximum(chunk_any, pltpu.roll(blk_any, LANES - d, 1))
        blk8 = lax.broadcasted_iota(jnp.int32, (8, LANES), 1) - HEAD_DIM
        chunk_bit = jnp.where((blk8 >= 0) & (blk8 % per_chunk == 0),
                              jnp.left_shift(1, jnp.maximum(blk8, 0) // per_chunk), 0).astype(F32)
        flagged = jnp.where(chunk_any > 0.5 * MASK_NEG, chunk_bit, 0.0)
        bits = jnp.sum(flagged[0:1, :], axis=1, keepdims=True)[0, 0].astype(jnp.int32)
        cnt = jnp.int32(0)
        idle = jnp.int32(0)
        for c in range(n_chunks):
            below = c < i
            hit = (((bits >> c) & 1) == 1) & below
            list_ref[u, cnt] = jnp.int32(c)
            cnt = cnt + hit.astype(jnp.int32)
            idle = jnp.where(below & jnp.logical_not(hit), c, idle)
        filler = jnp.where(i + 1 < n_chunks, i + 1, idle)
        list_ref[u, cnt] = filler

        w0 = pl.multiple_of(jnp.maximum(t0 - WINDOW, 0), TQ)
        s_w = _dot_nt(q3, kw_ref[0, 0, pl.ds(w0, WIN_KEYS), :])
        wk_row = w0 - t0 + lax.broadcasted_iota(jnp.int32, (1, WIN_KEYS), 1)
        dist = (lax.broadcasted_iota(jnp.int32, (TQ, WIN_KEYS), 0)
                - (w0 - t0) - lax.broadcasted_iota(jnp.int32, (TQ, WIN_KEYS), 1))
        w_valid = (dist >= 0) & (dist < WINDOW)
        wk_pos = wk_row.astype(F32)
        for r in range(Q_PER_KV):
            sr = jnp.where(w_valid, s_w[rows[r]] + slopes[r] * wk_pos, MASK_NEG)
            m = jnp.max(sr, axis=1, keepdims=True)
            pw_ref[u, rows[r], :] = jnp.exp2(sr - m).astype(BF16)
        acc_w = _dot(pw_ref[u], vwa_ref[0, 0, pl.ds(w0, WIN_KEYS), :])

        gates = gate_ref[0, tile, :]
        lane = lax.broadcasted_iota(jnp.int32, (TQ, LANES), 1)
        partial, g_sel = [], []
        for r in range(Q_PER_KV):
            col = g * (Q_PER_KV * N_BRANCH) + r * N_BRANCH
            gc, gs, gw = (jnp.sum(jnp.where(lane == col + br, gates, 0.0), axis=1, keepdims=True)
                          for br in range(N_BRANCH))
            a = acc_w[rows[r]]
            o_w = a[:, :HEAD_DIM] / a[:, HEAD_DIM:HEAD_DIM + 1]
            partial.append(gc * o_c[r] + gw * o_w)
            g_sel.append(jnp.broadcast_to(gs, (TQ, HEAD_DIM)))

        head = [jnp.where(cnt > j, list_ref[u, j], filler) for j in range(SHORT_LIST - 1)] + [i]
        scores = []
        for j, c in enumerate(head):
            k0 = pl.multiple_of(c * TQ, TQ)
            vg_ref[u, j * TQ:(j + 1) * TQ, :] = vsa_ref[0, 0, pl.ds(k0, TQ), :]
            scores.append(_dot_nt(qa_ref[u], ksa_ref[0, 0, pl.ds(k0, TQ), :]))
        for r in range(Q_PER_KV):
            biased = []
            for j, c in enumerate(head):
                sr = scores[j][rows[r]] + slopes[r] * (c * TQ - t0 + lane_row).astype(F32)
                biased.append(jnp.where(causal, sr, MASK_NEG) if j == SHORT_LIST - 1 else sr)
            part = biased[0][:, halves[0]]
            for sr in biased:
                for hf in halves:
                    part = jnp.maximum(part, sr[:, hf])
            m = jnp.max(part, axis=1, keepdims=True)
            m1_ref[u, rows[r], :] = jnp.broadcast_to(m, (TQ, LANES))
            for j, sr in enumerate(biased):
                p4_ref[u, rows[r], j * TQ:(j + 1) * TQ] = jnp.exp2(sr - m).astype(BF16)
        acc1_ref[u] = _dot(p4_ref[u], vg_ref[u])
        return t0, cnt, partial, g_sel

    def rest_of_list(u, t0, cnt):
        k_last = (cnt - SHORT_LIST + 2) // 2 - 1

        def entry(pos):
            return list_ref[u, SHORT_LIST - 1 + pos]

        def raw_scores(pos, dst_ref, k):
            k0 = pl.multiple_of(entry(pos) * TQ, TQ)
            s_all = _dot_nt(qa_ref[u], ksa_ref[0, 0, pl.ds(k0, TQ), :])
            for r in range(Q_PER_KV):
                dst_ref[k, rows[r], :] = s_all[rows[r]]

        def alibi_row(pos, r):
            return slopes[r] * (entry(pos) * TQ - t0 + lane_row).astype(F32)

        def fold_max(pos, src_ref, k):
            for r in range(Q_PER_KV):
                sr = src_ref[k, rows[r], :] + alibi_row(pos, r)
                part = sr[:, halves[0]]
                for hf in halves[1:]:
                    part = jnp.maximum(part, sr[:, hf])
                mx_ref[rows[r], :] = jnp.maximum(mx_ref[rows[r], :], part)

        mx_ref[...] = jnp.full(mx_ref.shape, M_INIT, F32)
        raw_scores(0, se_ref, 0)

        def pass1(k, carry):
            raw_scores(2 * k + 1, so_ref, k)
            fold_max(2 * k, se_ref, k)
            raw_scores(2 * k + 2, se_ref, k + 1)
            fold_max(2 * k + 1, so_ref, k)
            return carry

        lax.fori_loop(0, k_last, pass1, 0)
        raw_scores(2 * k_last + 1, so_ref, k_last)
        fold_max(2 * k_last, se_ref, k_last)
        fold_max(2 * k_last + 1, so_ref, k_last)

        for r in range(Q_PER_KV):
            m = jnp.max(mx_ref[rows[r], :], axis=1, keepdims=True)
            mx_ref[rows[r], :] = jnp.broadcast_to(m, (TQ, LANES))
        acc_ref[...] = jnp.zeros(acc_ref.shape, F32)

        def probs(pos, src_ref, k, dst_ref):
            for r in range(Q_PER_KV):
                mb = mx_ref[rows[r], :]
                bias = alibi_row(pos, r)
                for hf in halves:
                    dst_ref[rows[r], hf] = jnp.exp2(src_ref[k, rows[r], hf] + bias[:, hf] - mb).astype(BF16)

        def add_pv(pos, src_ref):
            k0 = pl.multiple_of(entry(pos) * TQ, TQ)
            acc_ref[...] += _dot(src_ref[...], vsa_ref[0, 0, pl.ds(k0, TQ), :])

        probs(0, se_ref, 0, pa_ref)

        def pass2(k, carry):
            add_pv(2 * k, pa_ref)
            probs(2 * k + 1, so_ref, k, pb_ref)
            add_pv(2 * k + 1, pb_ref)
            probs(2 * k + 2, se_ref, k + 1, pa_ref)
            return carry

        lax.fori_loop(0, k_last, pass2, 0)
        add_pv(2 * k_last, pa_ref)
        probs(2 * k_last + 1, so_ref, k_last, pb_ref)
        add_pv(2 * k_last + 1, pb_ref)

        for r in range(Q_PER_KV):
            m_head = m1_ref[u, rows[r], :]
            m_rest = mx_ref[rows[r], :]
            m = jnp.maximum(m_head, m_rest)
            acc1_ref[u, rows[r], :] = (acc1_ref[u, rows[r], :] * jnp.exp2(m_head - m)
                                       + acc_ref[rows[r], :] * jnp.exp2(m_rest - m))

    states = [main(u) for u in range(TILES_PER_STEP)]
    for u, (t0, cnt, _, _) in enumerate(states):
        pl.when(cnt > SHORT_LIST - 1)(functools.partial(rest_of_list, u, t0, cnt))
    for u, (_, _, partial, g_sel) in enumerate(states):
        outs = []
        for r in range(Q_PER_KV):
            a = acc1_ref[u, rows[r], :]
            o_s = a[:, :HEAD_DIM] / a[:, HEAD_DIM:HEAD_DIM + 1]
            outs.append(partial[r] + g_sel[r] * o_s)
        o_ref[0, 0, u * TQ:(u + 1) * TQ, :] = jnp.concatenate(outs, axis=1).astype(o_ref.dtype)


def _nsa(q, kc, vca, ksa, vsa, kw, vwa, gates, slopes):
    b, _, t, _ = q.shape
    ncp = kc.shape[2]
    tstep = TILES_PER_STEP * TQ
    grp = lambda bi, g, i: (bi, g, 0, 0)
    per_tile = lambda shape, dtype: pltpu.VMEM((TILES_PER_STEP,) + shape, dtype)
    return pl.pallas_call(
        _nsa_kernel,
        grid=(b, N_KV_HEADS, t // tstep),
        in_specs=[
            pl.BlockSpec(memory_space=pltpu.SMEM),
            pl.BlockSpec((1, Q_PER_KV, tstep, 2 * HEAD_DIM), lambda bi, g, i: (bi, g, i, 0)),
            pl.BlockSpec((1, 1, ncp, HEAD_DIM), grp),
            pl.BlockSpec((1, 1, ncp, 2 * HEAD_DIM), grp),
            pl.BlockSpec((1, 1, t, 2 * HEAD_DIM), grp),
            pl.BlockSpec((1, 1, t, 2 * HEAD_DIM), grp),
            pl.BlockSpec((1, 1, t, HEAD_DIM), grp),
            pl.BlockSpec((1, 1, t, 2 * HEAD_DIM), grp),
            pl.BlockSpec((1, tstep, GATE_PAD), lambda bi, g, i: (bi, i, 0)),
        ],
        out_specs=pl.BlockSpec((1, 1, tstep, Q_PER_KV * HEAD_DIM), lambda bi, g, i: (bi, g, i, 0)),
        out_shape=jax.ShapeDtypeStruct((b, N_KV_HEADS, t, Q_PER_KV * HEAD_DIM), BF16),
        scratch_shapes=[
            pltpu.VMEM((t // (2 * TQ), Q_PER_KV * TQ, TQ), F32),
            pltpu.VMEM((t // (2 * TQ), Q_PER_KV * TQ, TQ), F32),
            pltpu.VMEM((Q_PER_KV * TQ, LANES), F32),
            pltpu.VMEM((Q_PER_KV * TQ, TQ), BF16),
            pltpu.VMEM((Q_PER_KV * TQ, TQ), BF16),
            pltpu.VMEM((Q_PER_KV * TQ, LANES), F32),
            per_tile((Q_PER_KV * TQ, 2 * HEAD_DIM), BF16),
            per_tile((Q_PER_KV * TQ, WIN_KEYS), BF16),
            per_tile((Q_PER_KV * TQ, SHORT_LIST * TQ), BF16),
            per_tile((SHORT_LIST * TQ, 2 * HEAD_DIM), BF16),
            per_tile((Q_PER_KV * TQ, LANES), F32),
            per_tile((Q_PER_KV * TQ, LANES), F32),
            pltpu.SMEM((TILES_PER_STEP, t // TQ + 2), jnp.int32),
        ],
        compiler_params=pltpu.CompilerParams(
            dimension_semantics=("arbitrary", "arbitrary", "arbitrary"),
            vmem_limit_bytes=VMEM_LIMIT),
        name="nsa_attention",
    )(slopes, q, kc, vca, ksa, vsa, kw, vwa, gates)


def _outffn_kernel(x_ref, c_ref, n_ref, woc_ref, won_ref, fg_ref, wgu_ref, wd_ref, fin_ref,
                   o_ref, acc_ref, h_ref, act_ref, *, final):
    x1 = x_ref[0] + _dot(c_ref[0], woc_ref[...])
    for g in range(N_KV_HEADS):
        x1 = x1 + _dot(n_ref[0, g], won_ref[g])
    ms = jnp.mean(x1 * x1, axis=-1, keepdims=True)
    h_ref[...] = (x1 * lax.rsqrt(ms + EPS) * fg_ref[...]).astype(BF16)
    acc_ref[...] = x1

    for c in range(N_FF_CHUNKS):
        h = h_ref[...]
        gate = _dot(h, wgu_ref[:, c * FF_CHUNK:(c + 1) * FF_CHUNK])
        up = _dot(h, wgu_ref[:, D_FF + c * FF_CHUNK:D_FF + (c + 1) * FF_CHUNK])
        act_ref[:, c * FF_CHUNK:(c + 1) * FF_CHUNK] = (gate * _sigmoid(gate) * up).astype(BF16)
    y = acc_ref[...] + _dot(act_ref[...], wd_ref[...])
    if final:
        ms = jnp.mean(y * y, axis=-1, keepdims=True)
        y = y * lax.rsqrt(ms + EPS) * fin_ref[...]
    o_ref[0] = y


def _outffn(x, conv_out, nsa_out, wo_c, wo_n, ffn_g, wgu, wd, final_g, final):
    b, t, _ = x.shape
    tm = min(TM_PROJ, t)
    tok = lambda bi, i: (bi, i, 0)
    return pl.pallas_call(
        functools.partial(_outffn_kernel, final=final),
        grid=(b, t // tm),
        in_specs=[
            pl.BlockSpec((1, tm, D_MODEL), tok),
            pl.BlockSpec((1, tm, CONV_CH), tok),
            pl.BlockSpec((1, N_KV_HEADS, tm, Q_PER_KV * HEAD_DIM), lambda bi, i: (bi, 0, i, 0)),
            _const_spec((CONV_CH, D_MODEL)),
            _const_spec((N_KV_HEADS, Q_PER_KV * HEAD_DIM, D_MODEL)),
            _const_spec((1, D_MODEL)),
            _const_spec((D_MODEL, 2 * D_FF)),
            _const_spec((D_FF, D_MODEL)),
            _const_spec((1, D_MODEL)),
        ],
        out_specs=pl.BlockSpec((1, tm, D_MODEL), tok),
        out_shape=jax.ShapeDtypeStruct((b, t, D_MODEL), F32),
        scratch_shapes=[pltpu.VMEM((tm, D_MODEL), F32), pltpu.VMEM((tm, D_MODEL), BF16),
                        pltpu.VMEM((tm, D_FF), BF16)],
        compiler_params=pltpu.CompilerParams(
            dimension_semantics=("arbitrary", "arbitrary"), vmem_limit_bytes=VMEM_LIMIT),
        name="outproj_ffn",
    )(x, conv_out, nsa_out, wo_c, wo_n, ffn_g, wgu, wd, final_g)


def _pe_rows(pe):
    lo = pe[:CMP_STRIDE].reshape(1, -1)
    hi = pe[CMP_STRIDE:].reshape(1, -1)
    z = jnp.zeros((7, lo.shape[1]), pe.dtype)
    return jnp.concatenate([lo, z, hi, z], axis=0).astype(BF16)


def _w1_cat(w1):
    lo = w1[:CMP_STRIDE].reshape(CMP_STRIDE * HEAD_DIM, CMP_HIDDEN)
    hi = w1[CMP_STRIDE:].reshape(CMP_STRIDE * HEAD_DIM, CMP_HIDDEN)
    return jnp.concatenate([lo, hi], axis=1).astype(BF16)


def kernel(x, attn_norm, w_in, conv_w, conv_b, conv_ln_g, conv_ln_b, cmp_k_pe, cmp_k_w1, cmp_k_w2,
           cmp_v_pe, cmp_v_w1, cmp_v_w2, w_out, ffn_norm, w_gate_up, w_down, final_norm):
    depth = w_in.shape[0]
    t = x.shape[1]
    assert t % (2 * TQ) == 0 and t % (TILES_PER_STEP * TQ) == 0
    assert t // SEL_BLOCK <= LANES - HEAD_DIM and t >= WIN_KEYS
    slopes = jnp.asarray(_alibi_slopes(N_Q_HEADS) * np.float32(LOG2E))
    final_g = final_norm.reshape(1, D_MODEL)
    for l in range(depth):
        w_pad = jnp.pad(w_in[l], ((0, 0), (0, IN_COLS_PAD - w_in.shape[2]))).astype(BF16)
        a, q, kc_r, vc_r, ksa, vsa, kw, vwa, gates = _inproj(
            x, attn_norm[l].reshape(1, D_MODEL), w_pad)
        conv_out = _conv_mixer(
            a, jnp.pad(conv_w[l], ((0, CONV_HALO - CONV_WIDTH), (0, 0))),
            conv_b[l].reshape(1, CONV_CH), conv_ln_g[l].reshape(1, CONV_CH),
            conv_ln_b[l].reshape(1, CONV_CH))
        kc, vca = _compress(kc_r, vc_r,
                            _pe_rows(cmp_k_pe[l]), _w1_cat(cmp_k_w1[l]), cmp_k_w2[l].astype(BF16),
                            _pe_rows(cmp_v_pe[l]), _w1_cat(cmp_v_w1[l]), cmp_v_w2[l].astype(BF16))
        nsa_out = _nsa(q, kc, vca, ksa, vsa, kw, vwa, gates, slopes)
        wo = w_out[l].astype(BF16)
        wo_n = wo[CONV_CH:].reshape(N_KV_HEADS, Q_PER_KV * HEAD_DIM, D_MODEL)
        x = _outffn(x, conv_out, nsa_out, wo[:CONV_CH], wo_n, ffn_norm[l].reshape(1, D_MODEL),
                    w_gate_up[l].astype(BF16), w_down[l].astype(BF16), final_g,
                    final=(l == depth - 1))
    return x
```

```python
import functools
import math

import jax
import jax.numpy as jnp
import numpy as np
from jax import lax
from jax.experimental import pallas as pl
from jax.experimental.pallas import tpu as pltpu

F32 = jnp.float32
BF16 = jnp.bfloat16

D_MODEL = 1024
HEAD_DIM = 64
CONV_CH = 256
CONV_WIDTH = 31
N_Q_HEADS = 12
N_KV_HEADS = 4
Q_PER_KV = N_Q_HEADS // N_KV_HEADS
NSA_WIDTH = N_Q_HEADS * HEAD_DIM
KV_WIDTH = N_KV_HEADS * HEAD_DIM
CMP_LEN = 32
CMP_STRIDE = 16
CMP_HIDDEN = 256
SEL_BLOCK = 64
SEL_TOPK = 8
WINDOW = 512
N_BRANCH = 3
FORCED_BONUS = 1000.0
D_FF = 2816
EPS = 1e-6
LOG2E = math.log2(math.e)

LANES = 128
SUBLANES = 8
GATE_PAD = LANES
FF_CHUNK = 256
N_FF_CHUNKS = D_FF // FF_CHUNK
TM_PROJ = 512
CONV_HALO = 32
CONV_ROWS = TM_PROJ
TQ = 256
WIN_KEYS = WINDOW + TQ
SHORT_LIST = 4
TILES_PER_STEP = 4
MASK_NEG = -(2.0 ** 100)
M_INIT = -1e30
VMEM_LIMIT = 56 * 1024 * 1024


def _alibi_slopes(n):
    def pow2_slopes(m):
        start = 2.0 ** (-8.0 / m)
        return [start ** (i + 1) for i in range(m)]
    if math.log2(n).is_integer():
        s = pow2_slopes(n)
    else:
        c = 2 ** math.floor(math.log2(n))
        s = pow2_slopes(c) + pow2_slopes(2 * c)[0::2][: n - c]
    return np.asarray(s, dtype=np.float32)


def _sigmoid(v):
    return 1.0 / (1.0 + jnp.exp(-v))


def _dot(a, b):
    return jnp.dot(a, b, preferred_element_type=F32)


def _dot_nt(a, b):
    return lax.dot_general(a, b, (((1,), (1,)), ((), ())), preferred_element_type=F32)


def _const_spec(shape):
    nd = len(shape)
    return pl.BlockSpec(shape, lambda *_: (0,) * nd, pipeline_mode=pl.Buffered(1))


C_A = 0
C_Q = C_A + 2 * CONV_CH
C_KC = C_Q + NSA_WIDTH
C_VC = C_KC + KV_WIDTH
C_KS = C_VC + KV_WIDTH
C_VS = C_KS + KV_WIDTH
C_KW = C_VS + KV_WIDTH
C_VW = C_KW + KV_WIDTH
C_G = C_VW + KV_WIDTH
IN_COLS_PAD = C_G + GATE_PAD


def _conv_mixer(a, w_ref, b_ref, lg_ref, lb_ref, y_ref, o_ref):
    tm = a.shape[0]
    y_ref[CONV_HALO:CONV_HALO + tm, :] = a[:, :CONV_CH] * _sigmoid(a[:, CONV_CH:])
    first = CONV_HALO - (CONV_WIDTH - 1)
    for r0 in range(0, tm, CONV_ROWS):
        acc = b_ref[...]
        for rho in range(SUBLANES):
            n_rows = CONV_ROWS if rho == 0 else CONV_ROWS + SUBLANES
            z = None
            for base in range(0, CONV_HALO + 1, SUBLANES):
                k = base + rho - first
                if 0 <= k < CONV_WIDTH:
                    term = w_ref[k:k + 1, :] * y_ref[pl.ds(r0 + base, n_rows), :]
                    z = term if z is None else z + term
            acc = acc + (z if rho == 0 else z[rho:rho + CONV_ROWS, :])
        mu = jnp.mean(acc, axis=-1, keepdims=True)
        d = acc - mu
        var = jnp.mean(d * d, axis=-1, keepdims=True)
        yn = d * lax.rsqrt(var + EPS) * lg_ref[...] + lb_ref[...]
        o_ref[0, r0:r0 + CONV_ROWS, :] = (yn * _sigmoid(yn)).astype(o_ref.dtype)


def _inproj_kernel(x_ref, g_ref, w_ref, cw_ref, cb_ref, clg_ref, clb_ref,
                   conv_ref, q_ref, kcr_ref, vcr_ref, ksa_ref, vsa_ref, kw_ref, vwa_ref, gate_ref,
                   y_ref):
    tm = x_ref.shape[1]

    @pl.when(pl.program_id(1) == 0)
    def _zero_halo():
        y_ref[0:CONV_HALO, :] = jnp.zeros((CONV_HALO, CONV_CH), F32)

    @pl.when(pl.program_id(1) > 0)
    def _carry_halo():
        y_ref[0:CONV_HALO, :] = y_ref[tm:tm + CONV_HALO, :]

    x = x_ref[0]
    ms = jnp.mean(x * x, axis=-1, keepdims=True)
    h = (x * lax.rsqrt(ms + EPS) * g_ref[...]).astype(BF16)

    a = _dot(h, w_ref[:, C_A:C_Q])
    _conv_mixer(a, cw_ref, cb_ref, clg_ref, clb_ref, y_ref, conv_ref)
    zq = _dot(h, w_ref[:, C_Q:C_KC]) * (HEAD_DIM ** -0.5 * LOG2E)
    zero_hi = jnp.zeros((tm, HEAD_DIM), BF16)
    for hd in range(N_Q_HEADS):
        qh = zq[:, hd * HEAD_DIM:(hd + 1) * HEAD_DIM].astype(BF16)
        q_ref[0, hd] = jnp.concatenate([qh, zero_hi], axis=1)
    zkc = _dot(h, w_ref[:, C_KC:C_VC])
    zvc = _dot(h, w_ref[:, C_VC:C_KS])
    for half in range(KV_WIDTH // LANES):
        kcr_ref[0, half] = zkc[:, half * LANES:(half + 1) * LANES]
        vcr_ref[0, half] = zvc[:, half * LANES:(half + 1) * LANES]

    t = pl.program_id(1) * tm + lax.broadcasted_iota(jnp.int32, (tm, HEAD_DIM), 0)
    blk = lax.broadcasted_iota(jnp.int32, (tm, HEAD_DIM), 1)
    onehot = jnp.where(t // SEL_BLOCK == blk, 1.0, 0.0).astype(BF16)
    ones_col = jnp.where(blk == 0, 1.0, 0.0).astype(BF16)
    zks = _dot(h, w_ref[:, C_KS:C_VS])
    zvs = _dot(h, w_ref[:, C_VS:C_KW])
    zkw = _dot(h, w_ref[:, C_KW:C_VW])
    zvw = _dot(h, w_ref[:, C_VW:C_G])
    for g in range(N_KV_HEADS):
        sl = slice(g * HEAD_DIM, (g + 1) * HEAD_DIM)
        ksa_ref[0, g] = jnp.concatenate([zks[:, sl].astype(BF16), onehot], axis=1)
        vsa_ref[0, g] = jnp.concatenate([zvs[:, sl].astype(BF16), ones_col], axis=1)
        kw_ref[0, g] = zkw[:, sl].astype(BF16)
        vwa_ref[0, g] = jnp.concatenate([zvw[:, sl].astype(BF16), ones_col], axis=1)
    gate_ref[0] = _sigmoid(_dot(h, w_ref[:, C_G:IN_COLS_PAD]))


def _inproj(x, norm_g, w_pad, conv_w, conv_b, conv_ln_g, conv_ln_b):
    b, t, _ = x.shape
    tm = min(TM_PROJ, t)
    grid = (b, t // tm)
    tok = lambda bi, i: (bi, i, 0)
    head = lambda bi, i: (bi, 0, i, 0)
    out_shape = (
        jax.ShapeDtypeStruct((b, t, CONV_CH), BF16),
        jax.ShapeDtypeStruct((b, N_Q_HEADS, t, 2 * HEAD_DIM), BF16),
        jax.ShapeDtypeStruct((b, KV_WIDTH // LANES, t, LANES), F32),
        jax.ShapeDtypeStruct((b, KV_WIDTH // LANES, t, LANES), F32),
        jax.ShapeDtypeStruct((b, N_KV_HEADS, t, 2 * HEAD_DIM), BF16),
        jax.ShapeDtypeStruct((b, N_KV_HEADS, t, 2 * HEAD_DIM), BF16),
        jax.ShapeDtypeStruct((b, N_KV_HEADS, t, HEAD_DIM), BF16),
        jax.ShapeDtypeStruct((b, N_KV_HEADS, t, 2 * HEAD_DIM), BF16),
        jax.ShapeDtypeStruct((b, t, GATE_PAD), F32),
    )
    out_specs = (
        pl.BlockSpec((1, tm, CONV_CH), tok),
        pl.BlockSpec((1, N_Q_HEADS, tm, 2 * HEAD_DIM), head),
        pl.BlockSpec((1, KV_WIDTH // LANES, tm, LANES), head),
        pl.BlockSpec((1, KV_WIDTH // LANES, tm, LANES), head),
        pl.BlockSpec((1, N_KV_HEADS, tm, 2 * HEAD_DIM), head),
        pl.BlockSpec((1, N_KV_HEADS, tm, 2 * HEAD_DIM), head),
        pl.BlockSpec((1, N_KV_HEADS, tm, HEAD_DIM), head),
        pl.BlockSpec((1, N_KV_HEADS, tm, 2 * HEAD_DIM), head),
        pl.BlockSpec((1, tm, GATE_PAD), tok),
    )
    return pl.pallas_call(
        _inproj_kernel,
        grid=grid,
        in_specs=[
            pl.BlockSpec((1, tm, D_MODEL), tok),
            _const_spec((1, D_MODEL)),
            _const_spec((D_MODEL, IN_COLS_PAD)),
            _const_spec((CONV_HALO, CONV_CH)),
            _const_spec((1, CONV_CH)),
            _const_spec((1, CONV_CH)),
            _const_spec((1, CONV_CH)),
        ],
        out_specs=out_specs,
        out_shape=out_shape,
        scratch_shapes=[pltpu.VMEM((CONV_HALO + tm, CONV_CH), F32)],
        compiler_params=pltpu.CompilerParams(
            dimension_semantics=("arbitrary", "arbitrary"), vmem_limit_bytes=VMEM_LIMIT),
        name="inproj_conv",
    )(x, norm_g, w_pad, conv_w, conv_b, conv_ln_g, conv_ln_b)


def _compress_kernel(kr_ref, vr_ref, kpe_ref, kw1_ref, kw2_ref, vpe_ref, vw1_ref, vw2_ref,
                     kc_ref, vc_ref):
    nch = kr_ref.shape[2] // CMP_STRIDE
    row = lax.broadcasted_iota(jnp.int32, (nch, HEAD_DIM), 0)
    c_start = row * CMP_STRIDE
    s_start = lax.broadcasted_iota(jnp.int32, (nch, HEAD_DIM), 1) * SEL_BLOCK
    overlap = jnp.where((c_start < s_start + SEL_BLOCK) & (c_start + CMP_LEN > s_start),
                        1.0, 0.0).astype(BF16)
    for r_ref, pe_ref, w1_ref, w2_ref, o_ref, tail in (
            (kr_ref, kpe_ref, kw1_ref, kw2_ref, kc_ref, None),
            (vr_ref, vpe_ref, vw1_ref, vw2_ref, vc_ref, overlap)):
        pb = _dot(pe_ref[...], w1_ref[...])
        bias = pb[0:1, :CMP_HIDDEN] + pb[8:9, CMP_HIDDEN:]
        rows = [[r_ref[0, half, pl.ds(l, nch, stride=CMP_STRIDE), :] for l in range(CMP_STRIDE)]
                for half in range(KV_WIDTH // LANES)]
        for g in range(N_KV_HEADS):
            half, odd = divmod(g, LANES // HEAD_DIM)
            sl = slice(odd * HEAD_DIM, (odd + 1) * HEAD_DIM)
            xg = jnp.concatenate([rw[:, sl] for rw in rows[half]], axis=1).astype(BF16)
            lohi = _dot(xg, w1_ref[...])
            lo = lohi[:, :CMP_HIDDEN]
            hi = lohi[:, CMP_HIDDEN:]
            hi_next = jnp.concatenate([hi[1:], jnp.zeros((1, CMP_HIDDEN), F32)], axis=0)
            hid = lo + hi_next + bias
            act = (hid * _sigmoid(hid)).astype(BF16)
            out = _dot(act, w2_ref[...])
            out = jnp.where(row < nch - 1, out, 0.0)
            out = out.astype(o_ref.dtype)
            o_ref[0, g] = out if tail is None else jnp.concatenate([out, tail], axis=1)


def _compress(kc_r, vc_r, kpe, kw1, kw2, vpe, vw1, vw2):
    b, _, t, _ = kc_r.shape
    nch = t // CMP_STRIDE
    blk_len = CMP_STRIDE * HEAD_DIM
    out_sds = [jax.ShapeDtypeStruct((b, N_KV_HEADS, nch, w), BF16) for w in (HEAD_DIM, 2 * HEAD_DIM)]
    out_spec = [pl.BlockSpec((1, N_KV_HEADS, nch, w), lambda bi: (bi, 0, 0, 0))
                for w in (HEAD_DIM, 2 * HEAD_DIM)]
    raw_spec = pl.BlockSpec((1, KV_WIDTH // LANES, t, LANES), lambda bi: (bi, 0, 0, 0))
    w_specs = [_const_spec((16, blk_len)), _const_spec((blk_len, 2 * CMP_HIDDEN)),
               _const_spec((CMP_HIDDEN, HEAD_DIM))]
    return pl.pallas_call(
        _compress_kernel,
        grid=(b,),
        in_specs=[raw_spec, raw_spec] + w_specs + w_specs,
        out_specs=tuple(out_spec),
        out_shape=tuple(out_sds),
        compiler_params=pltpu.CompilerParams(
            dimension_semantics=("arbitrary",), vmem_limit_bytes=VMEM_LIMIT),
        name="compress",
    )(kc_r, vc_r, kpe, kw1, kw2, vpe, vw1, vw2)


def _nsa_kernel(slope_ref, q_ref, kc_ref, vca_ref, ksa_ref, vsa_ref, kw_ref, vwa_ref, gate_ref,
                o_ref, se_ref, so_ref, mx_ref, pa_ref, pb_ref, acc_ref,
                qa_ref, pw_ref, p4_ref, vg_ref, m1_ref, acc1_ref, list_ref):
    g = pl.program_id(1)
    ncp = kc_ref.shape[2]
    n_chunks = list_ref.shape[1] - 2
    slopes = [slope_ref[g * Q_PER_KV + r] for r in range(Q_PER_KV)]
    rows = [slice(r * TQ, (r + 1) * TQ) for r in range(Q_PER_KV)]
    halves = [slice(hf * LANES, (hf + 1) * LANES) for hf in range(TQ // LANES)]
    lane_row = lax.broadcasted_iota(jnp.int32, (1, TQ), 1)
    causal = (lax.broadcasted_iota(jnp.int32, (TQ, TQ), 1)
              <= lax.broadcasted_iota(jnp.int32, (TQ, TQ), 0))

    def main(u):
        i = pl.program_id(2) * TILES_PER_STEP + u
        t0 = i * TQ
        tile = slice(u * TQ, (u + 1) * TQ)
        q_full = q_ref[0, :, tile, :].reshape(Q_PER_KV * TQ, 2 * HEAD_DIM)
        q3 = q_full[:, :HEAD_DIM]

        s_c = _dot_nt(q3, kc_ref[0, 0])
        n_idx = lax.broadcasted_iota(jnp.int32, (TQ, ncp), 1)
        t_idx = t0 + lax.broadcasted_iota(jnp.int32, (TQ, ncp), 0)
        c_valid = n_idx * CMP_STRIDE + (CMP_LEN - 1) <= t_idx
        n_row = lax.broadcasted_iota(jnp.int32, (1, ncp), 1)
        c_pos = (n_row * CMP_STRIDE - t0).astype(F32) + 0.5 * (CMP_LEN - 1)
        vca = vca_ref[0, 0]
        o_c = []
        imp = jnp.zeros((TQ, LANES), F32)
        for r in range(Q_PER_KV):
            sr = jnp.where(c_valid, s_c[rows[r]] + slopes[r] * c_pos, -jnp.inf)
            m = jnp.max(sr, axis=1, keepdims=True)
            m = jnp.where(m == -jnp.inf, 0.0, m)
            p = jnp.exp2(sr - m)
            inv = 1.0 / jnp.maximum(jnp.sum(p, axis=1, keepdims=True), 1e-30)
            pv = _dot(p.astype(BF16), vca) * inv
            o_c.append(pv[:, :HEAD_DIM])
            imp = imp + pv

        imp_t = imp.T[HEAD_DIM:, :]
        j_t = lax.broadcasted_iota(jnp.int32, (LANES - HEAD_DIM, TQ), 0)
        cur_t = (t0 + lax.broadcasted_iota(jnp.int32, (LANES - HEAD_DIM, TQ), 1)) // SEL_BLOCK
        forced = (j_t == 0) | (j_t == cur_t) | (j_t == cur_t - 1)
        cand = jnp.where(j_t <= cur_t, imp_t + jnp.where(forced, FORCED_BONUS, 0.0), -jnp.inf)
        sel = jnp.zeros(cand.shape, jnp.bool_)
        for _ in range(SEL_TOPK):
            mx = jnp.max(cand, axis=0, keepdims=True)
            first = jnp.min(jnp.where(cand == mx, j_t, LANES), axis=0, keepdims=True)
            hit = j_t == first
            sel = sel | (hit & (mx > -jnp.inf))
            cand = jnp.where(hit, -jnp.inf, cand)
        bias_t = jnp.concatenate([jnp.zeros(cand.shape, F32), jnp.where(sel, 0.0, MASK_NEG)], axis=0)
        sel_bias = bias_t.T
        sel_bias_bf = sel_bias.astype(BF16)
        for r in range(Q_PER_KV):
            qa_ref[u, rows[r], :] = q_full[rows[r]] + sel_bias_bf

        blk_any = jnp.broadcast_to(jnp.max(sel_bias, axis=0, keepdims=True), (8, LANES))
        per_chunk = TQ // SEL_BLOCK
        chunk_any = blk_any
        for d in range(1, per_chunk):
            chunk_any = jnp.maximum(chunk_any, pltpu.roll(blk_any, LANES - d, 1))
        blk8 = lax.broadcasted_iota(jnp.int32, (8, LANES), 1) - HEAD_DIM
        chunk_bit = jnp.where((blk8 >= 0) & (blk8 % per_chunk == 0),
                              jnp.left_shift(1, jnp.maximum(blk8, 0) // per_chunk), 0).astype(F32)
        flagged = jnp.where(chunk_any > 0.5 * MASK_NEG, chunk_bit, 0.0)
        bits = jnp.sum(flagged[0:1, :], axis=1, keepdims=True)[0, 0].astype(jnp.int32)
        cnt = jnp.int32(0)
        idle = jnp.int32(0)
        for c in range(n_chunks):
            below = c < i
            hit = (((bits >> c) & 1) == 1) & below
            list_ref[u, cnt] = jnp.int32(c)
            cnt = cnt + hit.astype(jnp.int32)
            idle = jnp.where(below & jnp.logical_not(hit), c, idle)
        filler = jnp.where(i + 1 < n_chunks, i + 1, idle)
        list_ref[u, cnt] = filler

        w0 = pl.multiple_of(jnp.maximum(t0 - WINDOW, 0), TQ)
        s_w = _dot_nt(q3, kw_ref[0, 0, pl.ds(w0, WIN_KEYS), :])
        wk_row = w0 - t0 + lax.broadcasted_iota(jnp.int32, (1, WIN_KEYS), 1)
        dist = (lax.broadcasted_iota(jnp.int32, (TQ, WIN_KEYS), 0)
                - (w0 - t0) - lax.broadcasted_iota(jnp.int32, (TQ, WIN_KEYS), 1))
        w_valid = (dist >= 0) & (dist < WINDOW)
        wk_pos = wk_row.astype(F32)
        for r in range(Q_PER_KV):
            sr = jnp.where(w_valid, s_w[rows[r]] + slopes[r] * wk_pos, MASK_NEG)
            m = jnp.max(sr, axis=1, keepdims=True)
            pw_ref[u, rows[r], :] = jnp.exp2(sr - m).astype(BF16)
        acc_w = _dot(pw_ref[u], vwa_ref[0, 0, pl.ds(w0, WIN_KEYS), :])

        gates = gate_ref[0, tile, :]
        lane = lax.broadcasted_iota(jnp.int32, (TQ, LANES), 1)
        partial, g_sel = [], []
        for r in range(Q_PER_KV):
            col = g * (Q_PER_KV * N_BRANCH) + r * N_BRANCH
            gc, gs, gw = (jnp.sum(jnp.where(lane == col + br, gates, 0.0), axis=1, keepdims=True)
                          for br in range(N_BRANCH))
            a = acc_w[rows[r]]
            o_w = a[:, :HEAD_DIM] / a[:, HEAD_DIM:HEAD_DIM + 1]
            partial.append(gc * o_c[r] + gw * o_w)
            g_sel.append(jnp.broadcast_to(gs, (TQ, HEAD_DIM)))

        head = [jnp.where(cnt > j, list_ref[u, j], filler) for j in range(SHORT_LIST - 1)] + [i]
        scores = []
        for j, c in enumerate(head):
            k0 = pl.multiple_of(c * TQ, TQ)
            vg_ref[u, j * TQ:(j + 1) * TQ, :] = vsa_ref[0, 0, pl.ds(k0, TQ), :]
            scores.append(_dot_nt(qa_ref[u], ksa_ref[0, 0, pl.ds(k0, TQ), :]))
        for r in range(Q_PER_KV):
            biased = []
            for j, c in enumerate(head):
                sr = scores[j][rows[r]] + slopes[r] * (c * TQ - t0 + lane_row).astype(F32)
                biased.append(jnp.where(causal, sr, MASK_NEG) if j == SHORT_LIST - 1 else sr)
            part = biased[0][:, halves[0]]
            for sr in biased:
                for hf in halves:
                    part = jnp.maximum(part, sr[:, hf])
            m = jnp.max(part, axis=1, keepdims=True)
            m1_ref[u, rows[r], :] = jnp.broadcast_to(m, (TQ, LANES))
            for j, sr in enumerate(biased):
                p4_ref[u, rows[r], j * TQ:(j + 1) * TQ] = jnp.exp2(sr - m).astype(BF16)
        acc1_ref[u] = _dot(p4_ref[u], vg_ref[u])
        return t0, cnt, partial, g_sel

    def rest_of_list(u, t0, cnt):
        k_last = (cnt - SHORT_LIST + 2) // 2 - 1

        def entry(pos):
            return list_ref[u, SHORT_LIST - 1 + pos]

        def raw_scores(pos, dst_ref, k):
            k0 = pl.multiple_of(entry(pos) * TQ, TQ)
            s_all = _dot_nt(qa_ref[u], ksa_ref[0, 0, pl.ds(k0, TQ), :])
            for r in range(Q_PER_KV):
                dst_ref[k, rows[r], :] = s_all[rows[r]]

        def alibi_row(pos, r):
            return slopes[r] * (entry(pos) * TQ - t0 + lane_row).astype(F32)

        def fold_max(pos, src_ref, k):
            for r in range(Q_PER_KV):
                sr = src_ref[k, rows[r], :] + alibi_row(pos, r)
                part = sr[:, halves[0]]
                for hf in halves[1:]:
                    part = jnp.maximum(part, sr[:, hf])
                mx_ref[rows[r], :] = jnp.maximum(mx_ref[rows[r], :], part)

        mx_ref[...] = jnp.full(mx_ref.shape, M_INIT, F32)
        raw_scores(0, se_ref, 0)

        def pass1(k, carry):
            raw_scores(2 * k + 1, so_ref, k)
            fold_max(2 * k, se_ref, k)
            raw_scores(2 * k + 2, se_ref, k + 1)
            fold_max(2 * k + 1, so_ref, k)
            return carry

        lax.fori_loop(0, k_last, pass1, 0)
        raw_scores(2 * k_last + 1, so_ref, k_last)
        fold_max(2 * k_last, se_ref, k_last)
        fold_max(2 * k_last + 1, so_ref, k_last)

        for r in range(Q_PER_KV):
            m = jnp.max(mx_ref[rows[r], :], axis=1, keepdims=True)
            mx_ref[rows[r], :] = jnp.broadcast_to(m, (TQ, LANES))
        acc_ref[...] = jnp.zeros(acc_ref.shape, F32)

        def probs(pos, src_ref, k, dst_ref):
            for r in range(Q_PER_KV):
                mb = mx_ref[rows[r], :]
                bias = alibi_row(pos, r)
                for hf in halves:
                    dst_ref[rows[r], hf] = jnp.exp2(src_ref[k, rows[r], hf] + bias[:, hf] - mb).astype(BF16)

        def add_pv(pos, src_ref):
            k0 = pl.multiple_of(entry(pos) * TQ, TQ)
            acc_ref[...] += _dot(src_ref[...], vsa_ref[0, 0, pl.ds(k0, TQ), :])

        probs(0, se_ref, 0, pa_ref)

        def pass2(k, carry):
            add_pv(2 * k, pa_ref)
            probs(2 * k + 1, so_ref, k, pb_ref)
            add_pv(2 * k + 1, pb_ref)
            probs(2 * k + 2, se_ref, k + 1, pa_ref)
            return carry

        lax.fori_loop(0, k_last, pass2, 0)
        add_pv(2 * k_last, pa_ref)
        probs(2 * k_last + 1, so_ref, k_last, pb_ref)
        add_pv(2 * k_last + 1, pb_ref)

        for r in range(Q_PER_KV):
            m_head = m1_ref[u, rows[r], :]
            m_rest = mx_ref[rows[r], :]
            m = jnp.maximum(m_head, m_rest)
            acc1_ref[u, rows[r], :] = (acc1_ref[u, rows[r], :] * jnp.exp2(m_head - m)
                                       + acc_ref[rows[r], :] * jnp.exp2(m_rest - m))

    states = [main(u) for u in range(TILES_PER_STEP)]
    for u, (t0, cnt, _, _) in enumerate(states):
        pl.when(cnt > SHORT_LIST - 1)(functools.partial(rest_of_list, u, t0, cnt))
    for u, (_, _, partial, g_sel) in enumerate(states):
        outs = []
        for r in range(Q_PER_KV):
            a = acc1_ref[u, rows[r], :]
            o_s = a[:, :HEAD_DIM] / a[:, HEAD_DIM:HEAD_DIM + 1]
            outs.append(partial[r] + g_sel[r] * o_s)
        o_ref[0, 0, u * TQ:(u + 1) * TQ, :] = jnp.concatenate(outs, axis=1).astype(o_ref.dtype)


def _nsa(q, kc, vca, ksa, vsa, kw, vwa, gates, slopes):
    b, _, t, _ = q.shape
    ncp = kc.shape[2]
    tstep = TILES_PER_STEP * TQ
    grp = lambda bi, g, i: (bi, g, 0, 0)
    per_tile = lambda shape, dtype: pltpu.VMEM((TILES_PER_STEP,) + shape, dtype)
    return pl.pallas_call(
        _nsa_kernel,
        grid=(b, N_KV_HEADS, t // tstep),
        in_specs=[
            pl.BlockSpec(memory_space=pltpu.SMEM),
            pl.BlockSpec((1, Q_PER_KV, tstep, 2 * HEAD_DIM), lambda bi, g, i: (bi, g, i, 0)),
            pl.BlockSpec((1, 1, ncp, HEAD_DIM), grp),
            pl.BlockSpec((1, 1, ncp, 2 * HEAD_DIM), grp),
            pl.BlockSpec((1, 1, t, 2 * HEAD_DIM), grp),
            pl.BlockSpec((1, 1, t, 2 * HEAD_DIM), grp),
            pl.BlockSpec((1, 1, t, HEAD_DIM), grp),
            pl.BlockSpec((1, 1, t, 2 * HEAD_DIM), grp),
            pl.BlockSpec((1, tstep, GATE_PAD), lambda bi, g, i: (bi, i, 0)),
        ],
        out_specs=pl.BlockSpec((1, 1, tstep, Q_PER_KV * HEAD_DIM), lambda bi, g, i: (bi, g, i, 0)),
        out_shape=jax.ShapeDtypeStruct((b, N_KV_HEADS, t, Q_PER_KV * HEAD_DIM), BF16),
        scratch_shapes=[
            pltpu.VMEM((t // (2 * TQ), Q_PER_KV * TQ, TQ), F32),
            pltpu.VMEM((t // (2 * TQ), Q_PER_KV * TQ, TQ), F32),
            pltpu.VMEM((Q_PER_KV * TQ, LANES), F32),
            pltpu.VMEM((Q_PER_KV * TQ, TQ), BF16),
            pltpu.VMEM((Q_PER_KV * TQ, TQ), BF16),
            pltpu.VMEM((Q_PER_KV * TQ, LANES), F32),
            per_tile((Q_PER_KV * TQ, 2 * HEAD_DIM), BF16),
            per_tile((Q_PER_KV * TQ, WIN_KEYS), BF16),
            per_tile((Q_PER_KV * TQ, SHORT_LIST * TQ), BF16),
            per_tile((SHORT_LIST * TQ, 2 * HEAD_DIM), BF16),
            per_tile((Q_PER_KV * TQ, LANES), F32),
            per_tile((Q_PER_KV * TQ, LANES), F32),
            pltpu.SMEM((TILES_PER_STEP, t // TQ + 2), jnp.int32),
        ],
        compiler_params=pltpu.CompilerParams(
            dimension_semantics=("arbitrary", "arbitrary", "arbitrary"),
            vmem_limit_bytes=VMEM_LIMIT),
        name="nsa_attention",
    )(slopes, q, kc, vca, ksa, vsa, kw, vwa, gates)


def _outffn_kernel(x_ref, c_ref, n_ref, woc_ref, won_ref, fg_ref, wgu_ref, wd_ref, fin_ref,
                   o_ref, acc_ref, h_ref, act_ref, *, final):
    x1 = x_ref[0] + _dot(c_ref[0], woc_ref[...])
    for g in range(N_KV_HEADS):
        x1 = x1 + _dot(n_ref[0, g], won_ref[g])
    ms = jnp.mean(x1 * x1, axis=-1, keepdims=True)
    h_ref[...] = (x1 * lax.rsqrt(ms + EPS) * fg_ref[...]).astype(BF16)
    acc_ref[...] = x1

    for c in range(N_FF_CHUNKS):
        h = h_ref[...]
        gate = _dot(h, wgu_ref[:, c * FF_CHUNK:(c + 1) * FF_CHUNK])
        up = _dot(h, wgu_ref[:, D_FF + c * FF_CHUNK:D_FF + (c + 1) * FF_CHUNK])
        act_ref[:, c * FF_CHUNK:(c + 1) * FF_CHUNK] = (gate * _sigmoid(gate) * up).astype(BF16)
    y = acc_ref[...] + _dot(act_ref[...], wd_ref[...])
    if final:
        ms = jnp.mean(y * y, axis=-1, keepdims=True)
        y = y * lax.rsqrt(ms + EPS) * fin_ref[...]
    o_ref[0] = y


def _outffn(x, conv_out, nsa_out, wo_c, wo_n, ffn_g, wgu, wd, final_g, final):
    b, t, _ = x.shape
    tm = min(TM_PROJ, t)
    tok = lambda bi, i: (bi, i, 0)
    return pl.pallas_call(
        functools.partial(_outffn_kernel, final=final),
        grid=(b, t // tm),
        in_specs=[
            pl.BlockSpec((1, tm, D_MODEL), tok),
            pl.BlockSpec((1, tm, CONV_CH), tok),
            pl.BlockSpec((1, N_KV_HEADS, tm, Q_PER_KV * HEAD_DIM), lambda bi, i: (bi, 0, i, 0)),
            _const_spec((CONV_CH, D_MODEL)),
            _const_spec((N_KV_HEADS, Q_PER_KV * HEAD_DIM, D_MODEL)),
            _const_spec((1, D_MODEL)),
            _const_spec((D_MODEL, 2 * D_FF)),
            _const_spec((D_FF, D_MODEL)),
            _const_spec((1, D_MODEL)),
        ],
        out_specs=pl.BlockSpec((1, tm, D_MODEL), tok),
        out_shape=jax.ShapeDtypeStruct((b, t, D_MODEL), F32),
        scratch_shapes=[pltpu.VMEM((tm, D_MODEL), F32), pltpu.VMEM((tm, D_MODEL), BF16),
                        pltpu.VMEM((tm, D_FF), BF16)],
        compiler_params=pltpu.CompilerParams(
            dimension_semantics=("arbitrary", "arbitrary"), vmem_limit_bytes=VMEM_LIMIT),
        name="outproj_ffn",
    )(x, conv_out, nsa_out, wo_c, wo_n, ffn_g, wgu, wd, final_g)


def _pe_rows(pe):
    lo = pe[:CMP_STRIDE].reshape(1, -1)
    hi = pe[CMP_STRIDE:].reshape(1, -1)
    z = jnp.zeros((7, lo.shape[1]), pe.dtype)
    return jnp.concatenate([lo, z, hi, z], axis=0).astype(BF16)


def _w1_cat(w1):
    lo = w1[:CMP_STRIDE].reshape(CMP_STRIDE * HEAD_DIM, CMP_HIDDEN)
    hi = w1[CMP_STRIDE:].reshape(CMP_STRIDE * HEAD_DIM, CMP_HIDDEN)
    return jnp.concatenate([lo, hi], axis=1).astype(BF16)


def kernel(x, attn_norm, w_in, conv_w, conv_b, conv_ln_g, conv_ln_b, cmp_k_pe, cmp_k_w1, cmp_k_w2,
           cmp_v_pe, cmp_v_w1, cmp_v_w2, w_out, ffn_norm, w_gate_up, w_down, final_norm):
    depth = w_in.shape[0]
    t = x.shape[1]
    assert t % (2 * TQ) == 0 and t % (TILES_PER_STEP * TQ) == 0
    assert t // SEL_BLOCK <= LANES - HEAD_DIM and t >= WIN_KEYS
    slopes = jnp.asarray(_alibi_slopes(N_Q_HEADS) * np.float32(LOG2E))
    final_g = final_norm.reshape(1, D_MODEL)
    for l in range(depth):
        w_pad = jnp.pad(w_in[l], ((0, 0), (0, IN_COLS_PAD - w_in.shape[2]))).astype(BF16)
        conv_out, q, kc_r, vc_r, ksa, vsa, kw, vwa, gates = _inproj(
            x, attn_norm[l].reshape(1, D_MODEL), w_pad,
            jnp.pad(conv_w[l], ((0, CONV_HALO - CONV_WIDTH), (0, 0))),
            conv_b[l].reshape(1, CONV_CH), conv_ln_g[l].reshape(1, CONV_CH),
            conv_ln_b[l].reshape(1, CONV_CH))
        kc, vca = _compress(kc_r, vc_r,
                            _pe_rows(cmp_k_pe[l]), _w1_cat(cmp_k_w1[l]), cmp_k_w2[l].astype(BF16),
                            _pe_rows(cmp_v_pe[l]), _w1_cat(cmp_v_w1[l]), cmp_v_w2[l].astype(BF16))
        nsa_out = _nsa(q, kc, vca, ksa, vsa, kw, vwa, gates, slopes)
        wo = w_out[l].astype(BF16)
        wo_n = wo[CONV_CH:].reshape(N_KV_HEADS, Q_PER_KV * HEAD_DIM, D_MODEL)
        x = _outffn(x, conv_out, nsa_out, wo[:CONV_CH], wo_n, ffn_norm[l].reshape(1, D_MODEL),
                    w_gate_up[l].astype(BF16), w_down[l].astype(BF16), final_g,
                    final=(l == depth - 1))
    return x
```

```python
import functools
import math

import jax
import jax.numpy as jnp
import numpy as np
from jax import lax
from jax.experimental import pallas as pl
from jax.experimental.pallas import tpu as pltpu

F32 = jnp.float32
BF16 = jnp.bfloat16

D_MODEL = 1024
HEAD_DIM = 64
CONV_CH = 256
CONV_WIDTH = 31
N_Q_HEADS = 12
N_KV_HEADS = 4
Q_PER_KV = N_Q_HEADS // N_KV_HEADS
NSA_WIDTH = N_Q_HEADS * HEAD_DIM
KV_WIDTH = N_KV_HEADS * HEAD_DIM
CMP_LEN = 32
CMP_STRIDE = 16
CMP_HIDDEN = 256
SEL_BLOCK = 64
SEL_TOPK = 8
WINDOW = 512
N_BRANCH = 3
FORCED_BONUS = 1000.0
D_FF = 2816
EPS = 1e-6
LOG2E = math.log2(math.e)

LANES = 128
SUBLANES = 8
GATE_PAD = LANES
FF_CHUNK = 256
N_FF_CHUNKS = D_FF // FF_CHUNK
TM_PROJ = 512
CONV_HALO = 32
CONV_ROWS = TM_PROJ
TQ = 256
WIN_KEYS = WINDOW + TQ
SHORT_LIST = 4
TILES_PER_STEP = 4
MASK_NEG = -(2.0 ** 100)
M_INIT = -1e30
VMEM_LIMIT = 56 * 1024 * 1024


def _alibi_slopes(n):
    def pow2_slopes(m):
        start = 2.0 ** (-8.0 / m)
        return [start ** (i + 1) for i in range(m)]
    if math.log2(n).is_integer():
        s = pow2_slopes(n)
    else:
        c = 2 ** math.floor(math.log2(n))
        s = pow2_slopes(c) + pow2_slopes(2 * c)[0::2][: n - c]
    return np.asarray(s, dtype=np.float32)


def _sigmoid(v):
    return 1.0 / (1.0 + jnp.exp(-v))


def _dot(a, b):
    return jnp.dot(a, b, preferred_element_type=F32)


def _dot_nt(a, b):
    return lax.dot_general(a, b, (((1,), (1,)), ((), ())), preferred_element_type=F32)


def _const_spec(shape):
    nd = len(shape)
    return pl.BlockSpec(shape, lambda *_: (0,) * nd, pipeline_mode=pl.Buffered(1))


C_A = 0
C_Q = C_A + 2 * CONV_CH
C_KC = C_Q + NSA_WIDTH
C_VC = C_KC + KV_WIDTH
C_KS = C_VC + KV_WIDTH
C_VS = C_KS + KV_WIDTH
C_KW = C_VS + KV_WIDTH
C_VW = C_KW + KV_WIDTH
C_G = C_VW + KV_WIDTH
IN_COLS_PAD = C_G + GATE_PAD


def _conv_mixer(a, w_ref, b_ref, lg_ref, lb_ref, y_ref, o_ref):
    tm = a.shape[0]
    y_ref[CONV_HALO:CONV_HALO + tm, :] = a[:, :CONV_CH] * _sigmoid(a[:, CONV_CH:])
    first = CONV_HALO - (CONV_WIDTH - 1)
    for r0 in range(0, tm, CONV_ROWS):
        acc = b_ref[...]
        for rho in range(SUBLANES):
            n_rows = CONV_ROWS if rho == 0 else CONV_ROWS + SUBLANES
            z = None
            for base in range(0, CONV_HALO + 1, SUBLANES):
                k = base + rho - first
                if 0 <= k < CONV_WIDTH:
                    term = w_ref[k:k + 1, :] * y_ref[pl.ds(r0 + base, n_rows), :]
                    z = term if z is None else z + term
            acc = acc + (z if rho == 0 else z[rho:rho + CONV_ROWS, :])
        mu = jnp.mean(acc, axis=-1, keepdims=True)
        d = acc - mu
        var = jnp.mean(d * d, axis=-1, keepdims=True)
        yn = d * lax.rsqrt(var + EPS) * lg_ref[...] + lb_ref[...]
        o_ref[0, r0:r0 + CONV_ROWS, :] = (yn * _sigmoid(yn)).astype(o_ref.dtype)


def _inproj_kernel(x_ref, g_ref, w_ref, cw_ref, cb_ref, clg_ref, clb_ref,
                   conv_ref, q_ref, kcr_ref, vcr_ref, ksa_ref, vsa_ref, kw_ref, vwa_ref, gate_ref,
                   y_ref):
    tm = x_ref.shape[1]

    @pl.when(pl.program_id(1) == 0)
    def _zero_halo():
        y_ref[0:CONV_HALO, :] = jnp.zeros((CONV_HALO, CONV_CH), F32)

    @pl.when(pl.program_id(1) > 0)
    def _carry_halo():
        y_ref[0:CONV_HALO, :] = y_ref[tm:tm + CONV_HALO, :]

    x = x_ref[0]
    ms = jnp.mean(x * x, axis=-1, keepdims=True)
    h = (x * lax.rsqrt(ms + EPS) * g_ref[...]).astype(BF16)

    zq = _dot(h, w_ref[:, C_Q:C_KC]) * (HEAD_DIM ** -0.5 * LOG2E)
    zero_hi = jnp.zeros((tm, HEAD_DIM), BF16)
    for hd in range(N_Q_HEADS):
        qh = zq[:, hd * HEAD_DIM:(hd + 1) * HEAD_DIM].astype(BF16)
        q_ref[0, hd] = jnp.concatenate([qh, zero_hi], axis=1)
    zkc = _dot(h, w_ref[:, C_KC:C_VC])
    zvc = _dot(h, w_ref[:, C_VC:C_KS])
    for half in range(KV_WIDTH // LANES):
        kcr_ref[0, half] = zkc[:, half * LANES:(half + 1) * LANES]
        vcr_ref[0, half] = zvc[:, half * LANES:(half + 1) * LANES]

    a = _dot(h, w_ref[:, C_A:C_Q])
    _conv_mixer(a, cw_ref, cb_ref, clg_ref, clb_ref, y_ref, conv_ref)

    t = pl.program_id(1) * tm + lax.broadcasted_iota(jnp.int32, (tm, HEAD_DIM), 0)
    blk = lax.broadcasted_iota(jnp.int32, (tm, HEAD_DIM), 1)
    onehot = jnp.where(t // SEL_BLOCK == blk, 1.0, 0.0).astype(BF16)
    ones_col = jnp.where(blk == 0, 1.0, 0.0).astype(BF16)
    zks = _dot(h, w_ref[:, C_KS:C_VS])
    zvs = _dot(h, w_ref[:, C_VS:C_KW])
    zkw = _dot(h, w_ref[:, C_KW:C_VW])
    zvw = _dot(h, w_ref[:, C_VW:C_G])
    for g in range(N_KV_HEADS):
        sl = slice(g * HEAD_DIM, (g + 1) * HEAD_DIM)
        ksa_ref[0, g] = jnp.concatenate([zks[:, sl].astype(BF16), onehot], axis=1)
        vsa_ref[0, g] = jnp.concatenate([zvs[:, sl].astype(BF16), ones_col], axis=1)
        kw_ref[0, g] = zkw[:, sl].astype(BF16)
        vwa_ref[0, g] = jnp.concatenate([zvw[:, sl].astype(BF16), ones_col], axis=1)
    gate_ref[0] = _sigmoid(_dot(h, w_ref[:, C_G:IN_COLS_PAD]))


def _inproj(x, norm_g, w_pad, conv_w, conv_b, conv_ln_g, conv_ln_b):
    b, t, _ = x.shape
    tm = min(TM_PROJ, t)
    grid = (b, t // tm)
    tok = lambda bi, i: (bi, i, 0)
    head = lambda bi, i: (bi, 0, i, 0)
    out_shape = (
        jax.ShapeDtypeStruct((b, t, CONV_CH), BF16),
        jax.ShapeDtypeStruct((b, N_Q_HEADS, t, 2 * HEAD_DIM), BF16),
        jax.ShapeDtypeStruct((b, KV_WIDTH // LANES, t, LANES), F32),
        jax.ShapeDtypeStruct((b, KV_WIDTH // LANES, t, LANES), F32),
        jax.ShapeDtypeStruct((b, N_KV_HEADS, t, 2 * HEAD_DIM), BF16),
        jax.ShapeDtypeStruct((b, N_KV_HEADS, t, 2 * HEAD_DIM), BF16),
        jax.ShapeDtypeStruct((b, N_KV_HEADS, t, HEAD_DIM), BF16),
        jax.ShapeDtypeStruct((b, N_KV_HEADS, t, 2 * HEAD_DIM), BF16),
        jax.ShapeDtypeStruct((b, t, GATE_PAD), F32),
    )
    out_specs = (
        pl.BlockSpec((1, tm, CONV_CH), tok),
        pl.BlockSpec((1, N_Q_HEADS, tm, 2 * HEAD_DIM), head),
        pl.BlockSpec((1, KV_WIDTH // LANES, tm, LANES), head),
        pl.BlockSpec((1, KV_WIDTH // LANES, tm, LANES), head),
        pl.BlockSpec((1, N_KV_HEADS, tm, 2 * HEAD_DIM), head),
        pl.BlockSpec((1, N_KV_HEADS, tm, 2 * HEAD_DIM), head),
        pl.BlockSpec((1, N_KV_HEADS, tm, HEAD_DIM), head),
        pl.BlockSpec((1, N_KV_HEADS, tm, 2 * HEAD_DIM), head),
        pl.BlockSpec((1, tm, GATE_PAD), tok),
    )
    return pl.pallas_call(
        _inproj_kernel,
        grid=grid,
        in_specs=[
            pl.BlockSpec((1, tm, D_MODEL), tok),
            _const_spec((1, D_MODEL)),
            _const_spec((D_MODEL, IN_COLS_PAD)),
            _const_spec((CONV_HALO, CONV_CH)),
            _const_spec((1, CONV_CH)),
            _const_spec((1, CONV_CH)),
            _const_spec((1, CONV_CH)),
        ],
        out_specs=out_specs,
        out_shape=out_shape,
        scratch_shapes=[pltpu.VMEM((CONV_HALO + tm, CONV_CH), F32)],
        compiler_params=pltpu.CompilerParams(
            dimension_semantics=("arbitrary", "arbitrary"), vmem_limit_bytes=VMEM_LIMIT),
        name="inproj_conv",
    )(x, norm_g, w_pad, conv_w, conv_b, conv_ln_g, conv_ln_b)


def _compress_kernel(kr_ref, vr_ref, kpe_ref, kw1_ref, kw2_ref, vpe_ref, vw1_ref, vw2_ref,
                     kc_ref, vc_ref):
    nch = kr_ref.shape[2] // CMP_STRIDE
    row = lax.broadcasted_iota(jnp.int32, (nch, HEAD_DIM), 0)
    c_start = row * CMP_STRIDE
    s_start = lax.broadcasted_iota(jnp.int32, (nch, HEAD_DIM), 1) * SEL_BLOCK
    overlap = jnp.where((c_start < s_start + SEL_BLOCK) & (c_start + CMP_LEN > s_start),
                        1.0, 0.0).astype(BF16)
    for r_ref, pe_ref, w1_ref, w2_ref, o_ref, tail in (
            (kr_ref, kpe_ref, kw1_ref, kw2_ref, kc_ref, None),
            (vr_ref, vpe_ref, vw1_ref, vw2_ref, vc_ref, overlap)):
        pb = _dot(pe_ref[...], w1_ref[...])
        bias = pb[0:1, :CMP_HIDDEN] + pb[8:9, CMP_HIDDEN:]
        rows = [[r_ref[0, half, pl.ds(l, nch, stride=CMP_STRIDE), :] for l in range(CMP_STRIDE)]
                for half in range(KV_WIDTH // LANES)]
        for g in range(N_KV_HEADS):
            half, odd = divmod(g, LANES // HEAD_DIM)
            sl = slice(odd * HEAD_DIM, (odd + 1) * HEAD_DIM)
            xg = jnp.concatenate([rw[:, sl] for rw in rows[half]], axis=1).astype(BF16)
            lohi = _dot(xg, w1_ref[...])
            lo = lohi[:, :CMP_HIDDEN]
            hi = lohi[:, CMP_HIDDEN:]
            hi_next = jnp.concatenate([hi[1:], jnp.zeros((1, CMP_HIDDEN), F32)], axis=0)
            hid = lo + hi_next + bias
            act = (hid * _sigmoid(hid)).astype(BF16)
            out = _dot(act, w2_ref[...])
            out = jnp.where(row < nch - 1, out, 0.0)
            out = out.astype(o_ref.dtype)
            o_ref[0, g] = out if tail is None else jnp.concatenate([out, tail], axis=1)


def _compress(kc_r, vc_r, kpe, kw1, kw2, vpe, vw1, vw2):
    b, _, t, _ = kc_r.shape
    nch = t // CMP_STRIDE
    blk_len = CMP_STRIDE * HEAD_DIM
    out_sds = [jax.ShapeDtypeStruct((b, N_KV_HEADS, nch, w), BF16) for w in (HEAD_DIM, 2 * HEAD_DIM)]
    out_spec = [pl.BlockSpec((1, N_KV_HEADS, nch, w), lambda bi: (bi, 0, 0, 0))
                for w in (HEAD_DIM, 2 * HEAD_DIM)]
    raw_spec = pl.BlockSpec((1, KV_WIDTH // LANES, t, LANES), lambda bi: (bi, 0, 0, 0))
    w_specs = [_const_spec((16, blk_len)), _const_spec((blk_len, 2 * CMP_HIDDEN)),
               _const_spec((CMP_HIDDEN, HEAD_DIM))]
    return pl.pallas_call(
        _compress_kernel,
        grid=(b,),
        in_specs=[raw_spec, raw_spec] + w_specs + w_specs,
        out_specs=tuple(out_spec),
        out_shape=tuple(out_sds),
        compiler_params=pltpu.CompilerParams(
            dimension_semantics=("arbitrary",), vmem_limit_bytes=VMEM_LIMIT),
        name="compress",
    )(kc_r, vc_r, kpe, kw1, kw2, vpe, vw1, vw2)


def _nsa_kernel(slope_ref, q_ref, kc_ref, vca_ref, ksa_ref, vsa_ref, kw_ref, vwa_ref, gate_ref,
                wmask_ref, o_ref, se_ref, so_ref, mx_ref, pa_ref, pb_ref, acc_ref,
                qa_ref, pw_ref, p4_ref, vg_ref, m1_ref, acc1_ref, list_ref):
    g = pl.program_id(1)
    ncp = kc_ref.shape[2]
    n_chunks = list_ref.shape[1] - 2
    slopes = [slope_ref[g * Q_PER_KV + r] for r in range(Q_PER_KV)]
    rows = [slice(r * TQ, (r + 1) * TQ) for r in range(Q_PER_KV)]
    halves = [slice(hf * LANES, (hf + 1) * LANES) for hf in range(TQ // LANES)]
    lane_row = lax.broadcasted_iota(jnp.int32, (1, TQ), 1)
    causal = (lax.broadcasted_iota(jnp.int32, (TQ, TQ), 1)
              <= lax.broadcasted_iota(jnp.int32, (TQ, TQ), 0))

    def main(u):
        i = pl.program_id(2) * TILES_PER_STEP + u
        t0 = i * TQ
        tile = slice(u * TQ, (u + 1) * TQ)
        q_full = q_ref[0, :, tile, :].reshape(Q_PER_KV * TQ, 2 * HEAD_DIM)
        q3 = q_full[:, :HEAD_DIM]

        s_c = _dot_nt(q3, kc_ref[0, 0])
        n_idx = lax.broadcasted_iota(jnp.int32, (TQ, ncp), 1)
        t_idx = t0 + lax.broadcasted_iota(jnp.int32, (TQ, ncp), 0)
        c_valid = n_idx * CMP_STRIDE + (CMP_LEN - 1) <= t_idx
        n_row = lax.broadcasted_iota(jnp.int32, (1, ncp), 1)
        c_pos = (n_row * CMP_STRIDE - t0).astype(F32) + 0.5 * (CMP_LEN - 1)
        vca = vca_ref[0, 0]
        o_c = []
        imp = jnp.zeros((TQ, LANES), F32)
        for r in range(Q_PER_KV):
            sr = jnp.where(c_valid, s_c[rows[r]] + slopes[r] * c_pos, -jnp.inf)
            m = jnp.max(sr, axis=1, keepdims=True)
            m = jnp.where(m == -jnp.inf, 0.0, m)
            p = jnp.exp2(sr - m)
            inv = 1.0 / jnp.maximum(jnp.sum(p, axis=1, keepdims=True), 1e-30)
            pv = _dot(p.astype(BF16), vca) * inv
            o_c.append(pv[:, :HEAD_DIM])
            imp = imp + pv

        imp_t = imp.T[HEAD_DIM:, :]
        j_t = lax.broadcasted_iota(jnp.int32, (LANES - HEAD_DIM, TQ), 0)
        cur_t = (t0 + lax.broadcasted_iota(jnp.int32, (LANES - HEAD_DIM, TQ), 1)) // SEL_BLOCK
        forced = (j_t == 0) | (j_t == cur_t) | (j_t == cur_t - 1)
        cand = jnp.where(j_t <= cur_t, imp_t + jnp.where(forced, FORCED_BONUS, 0.0), -jnp.inf)
        sel = jnp.zeros(cand.shape, jnp.bool_)
        for _ in range(SEL_TOPK):
            mx = jnp.max(cand, axis=0, keepdims=True)
            first = jnp.min(jnp.where(cand == mx, j_t, LANES), axis=0, keepdims=True)
            hit = j_t == first
            sel = sel | (hit & (mx > -jnp.inf))
            cand = jnp.where(hit, -jnp.inf, cand)
        bias_t = jnp.concatenate([jnp.zeros(cand.shape, F32), jnp.where(sel, 0.0, MASK_NEG)], axis=0)
        sel_bias = bias_t.T
        sel_bias_bf = sel_bias.astype(BF16)
        for r in range(Q_PER_KV):
            qa_ref[u, rows[r], :] = q_full[rows[r]] + sel_bias_bf

        blk_any = jnp.broadcast_to(jnp.max(sel_bias, axis=0, keepdims=True), (8, LANES))
        per_chunk = TQ // SEL_BLOCK
        chunk_any = blk_any
        for d in range(1, per_chunk):
            chunk_any = jnp.maximum(chunk_any, pltpu.roll(blk_any, LANES - d, 1))
        blk8 = lax.broadcasted_iota(jnp.int32, (8, LANES), 1) - HEAD_DIM
        chunk_bit = jnp.where((blk8 >= 0) & (blk8 % per_chunk == 0),
                              jnp.left_shift(1, jnp.maximum(blk8, 0) // per_chunk), 0).astype(F32)
        flagged = jnp.where(chunk_any > 0.5 * MASK_NEG, chunk_bit, 0.0)
        bits = jnp.sum(flagged[0:1, :], axis=1, keepdims=True)[0, 0].astype(jnp.int32)
        cnt = jnp.int32(0)
        idle = jnp.int32(0)
        for c in range(n_chunks):
            below = c < i
            hit = (((bits >> c) & 1) == 1) & below
            list_ref[u, cnt] = jnp.int32(c)
            cnt = cnt + hit.astype(jnp.int32)
            idle = jnp.where(below & jnp.logical_not(hit), c, idle)
        filler = jnp.where(i + 1 < n_chunks, i + 1, idle)
        list_ref[u, cnt] = filler

        w0 = pl.multiple_of(jnp.maximum(t0 - WINDOW, 0), TQ)
        s_w = _dot_nt(q3, kw_ref[0, 0, pl.ds(w0, WIN_KEYS), :])
        wk_row = w0 - t0 + lax.broadcasted_iota(jnp.int32, (1, WIN_KEYS), 1)
        w_mask = wmask_ref[jnp.minimum(i, WINDOW // TQ)]
        wk_pos = wk_row.astype(F32)
        for r in range(Q_PER_KV):
            sr = s_w[rows[r]] + slopes[r] * wk_pos + w_mask
            m = jnp.max(sr, axis=1, keepdims=True)
            pw_ref[u, rows[r], :] = jnp.exp2(sr - m).astype(BF16)
        acc_w = _dot(pw_ref[u], vwa_ref[0, 0, pl.ds(w0, WIN_KEYS), :])

        gates = gate_ref[0, tile, :]
        lane = lax.broadcasted_iota(jnp.int32, (TQ, LANES), 1)
        partial, g_sel = [], []
        for r in range(Q_PER_KV):
            col = g * (Q_PER_KV * N_BRANCH) + r * N_BRANCH
            gc, gs, gw = (jnp.sum(jnp.where(lane == col + br, gates, 0.0), axis=1, keepdims=True)
                          for br in range(N_BRANCH))
            a = acc_w[rows[r]]
            o_w = a[:, :HEAD_DIM] / a[:, HEAD_DIM:HEAD_DIM + 1]
            partial.append(gc * o_c[r] + gw * o_w)
            g_sel.append(jnp.broadcast_to(gs, (TQ, HEAD_DIM)))

        head = [jnp.where(cnt > j, list_ref[u, j], filler) for j in range(SHORT_LIST - 1)] + [i]
        scores = []
        for j, c in enumerate(head):
            k0 = pl.multiple_of(c * TQ, TQ)
            vg_ref[u, j * TQ:(j + 1) * TQ, :] = vsa_ref[0, 0, pl.ds(k0, TQ), :]
            scores.append(_dot_nt(qa_ref[u], ksa_ref[0, 0, pl.ds(k0, TQ), :]))
        for r in range(Q_PER_KV):
            biased = []
            for j, c in enumerate(head):
                sr = scores[j][rows[r]] + slopes[r] * (c * TQ - t0 + lane_row).astype(F32)
                biased.append(jnp.where(causal, sr, MASK_NEG) if j == SHORT_LIST - 1 else sr)
            part = biased[0][:, halves[0]]
            for sr in biased:
                for hf in halves:
                    part = jnp.maximum(part, sr[:, hf])
            m = jnp.max(part, axis=1, keepdims=True)
            m1_ref[u, rows[r], :] = jnp.broadcast_to(m, (TQ, LANES))
            for j, sr in enumerate(biased):
                p4_ref[u, rows[r], j * TQ:(j + 1) * TQ] = jnp.exp2(sr - m).astype(BF16)
        acc1_ref[u] = _dot(p4_ref[u], vg_ref[u])
        return t0, cnt, partial, g_sel

    def rest_of_list(u, t0, cnt):
        k_last = (cnt - SHORT_LIST + 2) // 2 - 1

        def entry(pos):
            return list_ref[u, SHORT_LIST - 1 + pos]

        def raw_scores(pos, dst_ref, k):
            k0 = pl.multiple_of(entry(pos) * TQ, TQ)
            s_all = _dot_nt(qa_ref[u], ksa_ref[0, 0, pl.ds(k0, TQ), :])
            for r in range(Q_PER_KV):
                dst_ref[k, rows[r], :] = s_all[rows[r]]

        def alibi_row(pos, r):
            return slopes[r] * (entry(pos) * TQ - t0 + lane_row).astype(F32)

        def fold_max(pos, src_ref, k):
            for r in range(Q_PER_KV):
                sr = src_ref[k, rows[r], :] + alibi_row(pos, r)
                part = sr[:, halves[0]]
                for hf in halves[1:]:
                    part = jnp.maximum(part, sr[:, hf])
                mx_ref[rows[r], :] = jnp.maximum(mx_ref[rows[r], :], part)

        mx_ref[...] = jnp.full(mx_ref.shape, M_INIT, F32)
        raw_scores(0, se_ref, 0)

        def pass1(k, carry):
            raw_scores(2 * k + 1, so_ref, k)
            fold_max(2 * k, se_ref, k)
            raw_scores(2 * k + 2, se_ref, k + 1)
            fold_max(2 * k + 1, so_ref, k)
            return carry

        lax.fori_loop(0, k_last, pass1, 0)
        raw_scores(2 * k_last + 1, so_ref, k_last)
        fold_max(2 * k_last, se_ref, k_last)
        fold_max(2 * k_last + 1, so_ref, k_last)

        for r in range(Q_PER_KV):
            m = jnp.max(mx_ref[rows[r], :], axis=1, keepdims=True)
            mx_ref[rows[r], :] = jnp.broadcast_to(m, (TQ, LANES))
        acc_ref[...] = jnp.zeros(acc_ref.shape, F32)

        def probs(pos, src_ref, k, dst_ref):
            for r in range(Q_PER_KV):
                mb = mx_ref[rows[r], :]
                bias = alibi_row(pos, r)
                for hf in halves:
                    dst_ref[rows[r], hf] = jnp.exp2(src_ref[k, rows[r], hf] + bias[:, hf] - mb).astype(BF16)

        def add_pv(pos, src_ref):
            k0 = pl.multiple_of(entry(pos) * TQ, TQ)
            acc_ref[...] += _dot(src_ref[...], vsa_ref[0, 0, pl.ds(k0, TQ), :])

        probs(0, se_ref, 0, pa_ref)

        def pass2(k, carry):
            add_pv(2 * k, pa_ref)
            probs(2 * k + 1, so_ref, k, pb_ref)
            add_pv(2 * k + 1, pb_ref)
            probs(2 * k + 2, se_ref, k + 1, pa_ref)
            return carry

        lax.fori_loop(0, k_last, pass2, 0)
        add_pv(2 * k_last, pa_ref)
        probs(2 * k_last + 1, so_ref, k_last, pb_ref)
        add_pv(2 * k_last + 1, pb_ref)

        for r in range(Q_PER_KV):
            m_head = m1_ref[u, rows[r], :]
            m_rest = mx_ref[rows[r], :]
            m = jnp.maximum(m_head, m_rest)
            acc1_ref[u, rows[r], :] = (acc1_ref[u, rows[r], :] * jnp.exp2(m_head - m)
                                       + acc_ref[rows[r], :] * jnp.exp2(m_rest - m))

    states = [main(u) for u in range(TILES_PER_STEP)]
    for u, (t0, cnt, _, _) in enumerate(states):
        pl.when(cnt > SHORT_LIST - 1)(functools.partial(rest_of_list, u, t0, cnt))
    for u, (_, _, partial, g_sel) in enumerate(states):
        outs = []
        for r in range(Q_PER_KV):
            a = acc1_ref[u, rows[r], :]
            o_s = a[:, :HEAD_DIM] / a[:, HEAD_DIM:HEAD_DIM + 1]
            outs.append(partial[r] + g_sel[r] * o_s)
        o_ref[0, 0, u * TQ:(u + 1) * TQ, :] = jnp.concatenate(outs, axis=1).astype(o_ref.dtype)


def _window_masks():
    out = []
    for v in range(WINDOW // TQ + 1):
        rel0 = -v * TQ
        dist = np.arange(TQ)[:, None] - (rel0 + np.arange(WIN_KEYS)[None, :])
        out.append(np.where((dist >= 0) & (dist < WINDOW), 0.0, MASK_NEG))
    return jnp.asarray(np.stack(out), F32)


def _nsa(q, kc, vca, ksa, vsa, kw, vwa, gates, slopes):
    b, _, t, _ = q.shape
    ncp = kc.shape[2]
    tstep = TILES_PER_STEP * TQ
    grp = lambda bi, g, i: (bi, g, 0, 0)
    per_tile = lambda shape, dtype: pltpu.VMEM((TILES_PER_STEP,) + shape, dtype)
    return pl.pallas_call(
        _nsa_kernel,
        grid=(b, N_KV_HEADS, t // tstep),
        in_specs=[
            pl.BlockSpec(memory_space=pltpu.SMEM),
            pl.BlockSpec((1, Q_PER_KV, tstep, 2 * HEAD_DIM), lambda bi, g, i: (bi, g, i, 0)),
            pl.BlockSpec((1, 1, ncp, HEAD_DIM), grp),
            pl.BlockSpec((1, 1, ncp, 2 * HEAD_DIM), grp),
            pl.BlockSpec((1, 1, t, 2 * HEAD_DIM), grp),
            pl.BlockSpec((1, 1, t, 2 * HEAD_DIM), grp),
            pl.BlockSpec((1, 1, t, HEAD_DIM), grp),
            pl.BlockSpec((1, 1, t, 2 * HEAD_DIM), grp),
            pl.BlockSpec((1, tstep, GATE_PAD), lambda bi, g, i: (bi, i, 0)),
            _const_spec((WINDOW // TQ + 1, TQ, WIN_KEYS)),
        ],
        out_specs=pl.BlockSpec((1, 1, tstep, Q_PER_KV * HEAD_DIM), lambda bi, g, i: (bi, g, i, 0)),
        out_shape=jax.ShapeDtypeStruct((b, N_KV_HEADS, t, Q_PER_KV * HEAD_DIM), BF16),
        scratch_shapes=[
            pltpu.VMEM((t // (2 * TQ), Q_PER_KV * TQ, TQ), F32),
            pltpu.VMEM((t // (2 * TQ), Q_PER_KV * TQ, TQ), F32),
            pltpu.VMEM((Q_PER_KV * TQ, LANES), F32),
            pltpu.VMEM((Q_PER_KV * TQ, TQ), BF16),
            pltpu.VMEM((Q_PER_KV * TQ, TQ), BF16),
            pltpu.VMEM((Q_PER_KV * TQ, LANES), F32),
            per_tile((Q_PER_KV * TQ, 2 * HEAD_DIM), BF16),
            per_tile((Q_PER_KV * TQ, WIN_KEYS), BF16),
            per_tile((Q_PER_KV * TQ, SHORT_LIST * TQ), BF16),
            per_tile((SHORT_LIST * TQ, 2 * HEAD_DIM), BF16),
            per_tile((Q_PER_KV * TQ, LANES), F32),
            per_tile((Q_PER_KV * TQ, LANES), F32),
            pltpu.SMEM((TILES_PER_STEP, t // TQ + 2), jnp.int32),
        ],
        compiler_params=pltpu.CompilerParams(
            dimension_semantics=("arbitrary", "arbitrary", "arbitrary"),
            vmem_limit_bytes=VMEM_LIMIT),
        name="nsa_attention",
    )(slopes, q, kc, vca, ksa, vsa, kw, vwa, gates, _window_masks())


def _outffn_kernel(x_ref, c_ref, n_ref, woc_ref, won_ref, fg_ref, wgu_ref, wd_ref, fin_ref,
                   o_ref, acc_ref, h_ref, act_ref, *, final):
    x1 = x_ref[0] + _dot(c_ref[0], woc_ref[...])
    for g in range(N_KV_HEADS):
        x1 = x1 + _dot(n_ref[0, g], won_ref[g])
    ms = jnp.mean(x1 * x1, axis=-1, keepdims=True)
    h_ref[...] = (x1 * lax.rsqrt(ms + EPS) * fg_ref[...]).astype(BF16)
    acc_ref[...] = x1

    for c in range(N_FF_CHUNKS):
        h = h_ref[...]
        gate = _dot(h, wgu_ref[:, c * FF_CHUNK:(c + 1) * FF_CHUNK])
        up = _dot(h, wgu_ref[:, D_FF + c * FF_CHUNK:D_FF + (c + 1) * FF_CHUNK])
        act_ref[:, c * FF_CHUNK:(c + 1) * FF_CHUNK] = (gate * _sigmoid(gate) * up).astype(BF16)
    y = acc_ref[...] + _dot(act_ref[...], wd_ref[...])
    if final:
        ms = jnp.mean(y * y, axis=-1, keepdims=True)
        y = y * lax.rsqrt(ms + EPS) * fin_ref[...]
    o_ref[0] = y


def _outffn(x, conv_out, nsa_out, wo_c, wo_n, ffn_g, wgu, wd, final_g, final):
    b, t, _ = x.shape
    tm = min(TM_PROJ, t)
    tok = lambda bi, i: (bi, i, 0)
    return pl.pallas_call(
        functools.partial(_outffn_kernel, final=final),
        grid=(b, t // tm),
        in_specs=[
            pl.BlockSpec((1, tm, D_MODEL), tok),
            pl.BlockSpec((1, tm, CONV_CH), tok),
            pl.BlockSpec((1, N_KV_HEADS, tm, Q_PER_KV * HEAD_DIM), lambda bi, i: (bi, 0, i, 0)),
            _const_spec((CONV_CH, D_MODEL)),
            _const_spec((N_KV_HEADS, Q_PER_KV * HEAD_DIM, D_MODEL)),
            _const_spec((1, D_MODEL)),
            _const_spec((D_MODEL, 2 * D_FF)),
            _const_spec((D_FF, D_MODEL)),
            _const_spec((1, D_MODEL)),
        ],
        out_specs=pl.BlockSpec((1, tm, D_MODEL), tok),
        out_shape=jax.ShapeDtypeStruct((b, t, D_MODEL), F32),
        scratch_shapes=[pltpu.VMEM((tm, D_MODEL), F32), pltpu.VMEM((tm, D_MODEL), BF16),
                        pltpu.VMEM((tm, D_FF), BF16)],
        compiler_params=pltpu.CompilerParams(
            dimension_semantics=("arbitrary", "arbitrary"), vmem_limit_bytes=VMEM_LIMIT),
        name="outproj_ffn",
    )(x, conv_out, nsa_out, wo_c, wo_n, ffn_g, wgu, wd, final_g)


def _pe_rows(pe):
    lo = pe[:CMP_STRIDE].reshape(1, -1)
    hi = pe[CMP_STRIDE:].reshape(1, -1)
    z = jnp.zeros((7, lo.shape[1]), pe.dtype)
    return jnp.concatenate([lo, z, hi, z], axis=0).astype(BF16)


def _w1_cat(w1):
    lo = w1[:CMP_STRIDE].reshape(CMP_STRIDE * HEAD_DIM, CMP_HIDDEN)
    hi = w1[CMP_STRIDE:].reshape(CMP_STRIDE * HEAD_DIM, CMP_HIDDEN)
    return jnp.concatenate([lo, hi], axis=1).astype(BF16)


def kernel(x, attn_norm, w_in, conv_w, conv_b, conv_ln_g, conv_ln_b, cmp_k_pe, cmp_k_w1, cmp_k_w2,
           cmp_v_pe, cmp_v_w1, cmp_v_w2, w_out, ffn_norm, w_gate_up, w_down, final_norm):
    depth = w_in.shape[0]
    t = x.shape[1]
    assert t % (2 * TQ) == 0 and t % (TILES_PER_STEP * TQ) == 0
    assert t // SEL_BLOCK <= LANES - HEAD_DIM and t >= WIN_KEYS
    slopes = jnp.asarray(_alibi_slopes(N_Q_HEADS) * np.float32(LOG2E))
    final_g = final_norm.reshape(1, D_MODEL)
    for l in range(depth):
        w_pad = jnp.pad(w_in[l], ((0, 0), (0, IN_COLS_PAD - w_in.shape[2]))).astype(BF16)
        conv_out, q, kc_r, vc_r, ksa, vsa, kw, vwa, gates = _inproj(
            x, attn_norm[l].reshape(1, D_MODEL), w_pad,
            jnp.pad(conv_w[l], ((0, CONV_HALO - CONV_WIDTH), (0, 0))),
            conv_b[l].reshape(1, CONV_CH), conv_ln_g[l].reshape(1, CONV_CH),
            conv_ln_b[l].reshape(1, CONV_CH))
        kc, vca = _compress(kc_r, vc_r,
                            _pe_rows(cmp_k_pe[l]), _w1_cat(cmp_k_w1[l]), cmp_k_w2[l].astype(BF16),
                            _pe_rows(cmp_v_pe[l]), _w1_cat(cmp_v_w1[l]), cmp_v_w2[l].astype(BF16))
        nsa_out = _nsa(q, kc, vca, ksa, vsa, kw, vwa, gates, slopes)
        wo = w_out[l].astype(BF16)
        wo_n = wo[CONV_CH:].reshape(N_KV_HEADS, Q_PER_KV * HEAD_DIM, D_MODEL)
        x = _outffn(x, conv_out, nsa_out, wo[:CONV_CH], wo_n, ffn_norm[l].reshape(1, D_MODEL),
                    w_gate_up[l].astype(BF16), w_down[l].astype(BF16), final_g,
                    final=(l == depth - 1))
    return x
```

```python
import functools
import math

import jax
import jax.numpy as jnp
import numpy as np
from jax import lax
from jax.experimental import pallas as pl
from jax.experimental.pallas import tpu as pltpu

F32 = jnp.float32
BF16 = jnp.bfloat16

D_MODEL = 1024
HEAD_DIM = 64
CONV_CH = 256
CONV_WIDTH = 31
N_Q_HEADS = 12
N_KV_HEADS = 4
Q_PER_KV = N_Q_HEADS // N_KV_HEADS
NSA_WIDTH = N_Q_HEADS * HEAD_DIM
KV_WIDTH = N_KV_HEADS * HEAD_DIM
CMP_LEN = 32
CMP_STRIDE = 16
CMP_HIDDEN = 256
SEL_BLOCK = 64
SEL_TOPK = 8
WINDOW = 512
N_BRANCH = 3
FORCED_BONUS = 1000.0
D_FF = 2816
EPS = 1e-6
LOG2E = math.log2(math.e)

LANES = 128
SUBLANES = 8
GATE_PAD = LANES
FF_CHUNK = 256
N_FF_CHUNKS = D_FF // FF_CHUNK
TM_PROJ = 512
CONV_HALO = 32
CONV_ROWS = TM_PROJ
TQ = 256
WIN_KEYS = WINDOW + TQ
SHORT_LIST = 4
TILES_PER_STEP = 4
MASK_NEG = -(2.0 ** 100)
M_INIT = -1e30
VMEM_LIMIT = 56 * 1024 * 1024


def _alibi_slopes(n):
    def pow2_slopes(m):
        start = 2.0 ** (-8.0 / m)
        return [start ** (i + 1) for i in range(m)]
    if math.log2(n).is_integer():
        s = pow2_slopes(n)
    else:
        c = 2 ** math.floor(math.log2(n))
        s = pow2_slopes(c) + pow2_slopes(2 * c)[0::2][: n - c]
    return np.asarray(s, dtype=np.float32)


def _sigmoid(v):
    return 1.0 / (1.0 + jnp.exp(-v))


def _dot(a, b):
    return jnp.dot(a, b, preferred_element_type=F32)


def _dot_nt(a, b):
    return lax.dot_general(a, b, (((1,), (1,)), ((), ())), preferred_element_type=F32)


def _const_spec(shape):
    nd = len(shape)
    return pl.BlockSpec(shape, lambda *_: (0,) * nd, pipeline_mode=pl.Buffered(1))


C_A = 0
C_Q = C_A + 2 * CONV_CH
C_KC = C_Q + NSA_WIDTH
C_VC = C_KC + KV_WIDTH
C_KS = C_VC + KV_WIDTH
C_VS = C_KS + KV_WIDTH
C_KW = C_VS + KV_WIDTH
C_VW = C_KW + KV_WIDTH
C_G = C_VW + KV_WIDTH
IN_COLS_PAD = C_G + GATE_PAD


def _conv_mixer(a, w_ref, b_ref, lg_ref, lb_ref, y_ref, o_ref):
    tm = a.shape[0]
    y_ref[CONV_HALO:CONV_HALO + tm, :] = a[:, :CONV_CH] * _sigmoid(a[:, CONV_CH:])
    first = CONV_HALO - (CONV_WIDTH - 1)
    for r0 in range(0, tm, CONV_ROWS):
        acc = b_ref[...]
        for rho in range(SUBLANES):
            n_rows = CONV_ROWS if rho == 0 else CONV_ROWS + SUBLANES
            z = None
            for base in range(0, CONV_HALO + 1, SUBLANES):
                k = base + rho - first
                if 0 <= k < CONV_WIDTH:
                    term = w_ref[k:k + 1, :] * y_ref[pl.ds(r0 + base, n_rows), :]
                    z = term if z is None else z + term
            acc = acc + (z if rho == 0 else z[rho:rho + CONV_ROWS, :])
        mu = jnp.mean(acc, axis=-1, keepdims=True)
        d = acc - mu
        var = jnp.mean(d * d, axis=-1, keepdims=True)
        yn = d * lax.rsqrt(var + EPS) * lg_ref[...] + lb_ref[...]
        o_ref[0, r0:r0 + CONV_ROWS, :] = (yn * _sigmoid(yn)).astype(o_ref.dtype)


def _inproj_kernel(x_ref, g_ref, w_ref, cw_ref, cb_ref, clg_ref, clb_ref,
                   conv_ref, q_ref, kcr_ref, vcr_ref, ksa_ref, vsa_ref, kw_ref, vwa_ref, gate_ref,
                   y_ref):
    tm = x_ref.shape[1]

    @pl.when(pl.program_id(1) == 0)
    def _zero_halo():
        y_ref[0:CONV_HALO, :] = jnp.zeros((CONV_HALO, CONV_CH), F32)

    @pl.when(pl.program_id(1) > 0)
    def _carry_halo():
        y_ref[0:CONV_HALO, :] = y_ref[tm:tm + CONV_HALO, :]

    x = x_ref[0]
    ms = jnp.mean(x * x, axis=-1, keepdims=True)
    h = (x * lax.rsqrt(ms + EPS) * g_ref[...]).astype(BF16)

    zq = _dot(h, w_ref[:, C_Q:C_KC]) * (HEAD_DIM ** -0.5 * LOG2E)
    zero_hi = jnp.zeros((tm, HEAD_DIM), BF16)
    for hd in range(N_Q_HEADS):
        qh = zq[:, hd * HEAD_DIM:(hd + 1) * HEAD_DIM].astype(BF16)
        q_ref[0, hd] = jnp.concatenate([qh, zero_hi], axis=1)
    zkc = _dot(h, w_ref[:, C_KC:C_VC])
    zvc = _dot(h, w_ref[:, C_VC:C_KS])
    for half in range(KV_WIDTH // LANES):
        kcr_ref[0, half] = zkc[:, half * LANES:(half + 1) * LANES]
        vcr_ref[0, half] = zvc[:, half * LANES:(half + 1) * LANES]

    a = _dot(h, w_ref[:, C_A:C_Q])
    _conv_mixer(a, cw_ref, cb_ref, clg_ref, clb_ref, y_ref, conv_ref)

    t = pl.program_id(1) * tm + lax.broadcasted_iota(jnp.int32, (tm, HEAD_DIM), 0)
    blk = lax.broadcasted_iota(jnp.int32, (tm, HEAD_DIM), 1)
    onehot = jnp.where(t // SEL_BLOCK == blk, 1.0, 0.0).astype(BF16)
    ones_col = jnp.where(blk == 0, 1.0, 0.0).astype(BF16)
    zks = _dot(h, w_ref[:, C_KS:C_VS])
    zvs = _dot(h, w_ref[:, C_VS:C_KW])
    zkw = _dot(h, w_ref[:, C_KW:C_VW])
    zvw = _dot(h, w_ref[:, C_VW:C_G])
    for g in range(N_KV_HEADS):
        sl = slice(g * HEAD_DIM, (g + 1) * HEAD_DIM)
        ksa_ref[0, g] = jnp.concatenate([zks[:, sl].astype(BF16), onehot], axis=1)
        vsa_ref[0, g] = jnp.concatenate([zvs[:, sl].astype(BF16), ones_col], axis=1)
        kw_ref[0, g] = zkw[:, sl].astype(BF16)
        vwa_ref[0, g] = jnp.concatenate([zvw[:, sl].astype(BF16), ones_col], axis=1)
    gate_ref[0] = _sigmoid(_dot(h, w_ref[:, C_G:IN_COLS_PAD]))


def _inproj(x, norm_g, w_pad, conv_w, conv_b, conv_ln_g, conv_ln_b):
    b, t, _ = x.shape
    tm = min(TM_PROJ, t)
    grid = (b, t // tm)
    tok = lambda bi, i: (bi, i, 0)
    head = lambda bi, i: (bi, 0, i, 0)
    out_shape = (
        jax.ShapeDtypeStruct((b, t, CONV_CH), BF16),
        jax.ShapeDtypeStruct((b, N_Q_HEADS, t, 2 * HEAD_DIM), BF16),
        jax.ShapeDtypeStruct((b, KV_WIDTH // LANES, t, LANES), F32),
        jax.ShapeDtypeStruct((b, KV_WIDTH // LANES, t, LANES), F32),
        jax.ShapeDtypeStruct((b, N_KV_HEADS, t, 2 * HEAD_DIM), BF16),
        jax.ShapeDtypeStruct((b, N_KV_HEADS, t, 2 * HEAD_DIM), BF16),
        jax.ShapeDtypeStruct((b, N_KV_HEADS, t, HEAD_DIM), BF16),
        jax.ShapeDtypeStruct((b, N_KV_HEADS, t, 2 * HEAD_DIM), BF16),
        jax.ShapeDtypeStruct((b, t, GATE_PAD), F32),
    )
    out_specs = (
        pl.BlockSpec((1, tm, CONV_CH), tok),
        pl.BlockSpec((1, N_Q_HEADS, tm, 2 * HEAD_DIM), head),
        pl.BlockSpec((1, KV_WIDTH // LANES, tm, LANES), head),
        pl.BlockSpec((1, KV_WIDTH // LANES, tm, LANES), head),
        pl.BlockSpec((1, N_KV_HEADS, tm, 2 * HEAD_DIM), head),
        pl.BlockSpec((1, N_KV_HEADS, tm, 2 * HEAD_DIM), head),
        pl.BlockSpec((1, N_KV_HEADS, tm, HEAD_DIM), head),
        pl.BlockSpec((1, N_KV_HEADS, tm, 2 * HEAD_DIM), head),
        pl.BlockSpec((1, tm, GATE_PAD), tok),
    )
    return pl.pallas_call(
        _inproj_kernel,
        grid=grid,
        in_specs=[
            pl.BlockSpec((1, tm, D_MODEL), tok),
            _const_spec((1, D_MODEL)),
            _const_spec((D_MODEL, IN_COLS_PAD)),
            _const_spec((CONV_HALO, CONV_CH)),
            _const_spec((1, CONV_CH)),
            _const_spec((1, CONV_CH)),
            _const_spec((1, CONV_CH)),
        ],
        out_specs=out_specs,
        out_shape=out_shape,
        scratch_shapes=[pltpu.VMEM((CONV_HALO + tm, CONV_CH), F32)],
        compiler_params=pltpu.CompilerParams(
            dimension_semantics=("arbitrary", "arbitrary"), vmem_limit_bytes=VMEM_LIMIT),
        name="inproj_conv",
    )(x, norm_g, w_pad, conv_w, conv_b, conv_ln_g, conv_ln_b)


def _compress_kernel(kr_ref, vr_ref, kpe_ref, kw1_ref, kw2_ref, vpe_ref, vw1_ref, vw2_ref,
                     kc_ref, vc_ref):
    nch = kr_ref.shape[2] // CMP_STRIDE
    row = lax.broadcasted_iota(jnp.int32, (nch, HEAD_DIM), 0)
    c_start = row * CMP_STRIDE
    s_start = lax.broadcasted_iota(jnp.int32, (nch, HEAD_DIM), 1) * SEL_BLOCK
    overlap = jnp.where((c_start < s_start + SEL_BLOCK) & (c_start + CMP_LEN > s_start),
                        1.0, 0.0).astype(BF16)
    for r_ref, pe_ref, w1_ref, w2_ref, o_ref, tail in (
            (kr_ref, kpe_ref, kw1_ref, kw2_ref, kc_ref, None),
            (vr_ref, vpe_ref, vw1_ref, vw2_ref, vc_ref, overlap)):
        pb = _dot(pe_ref[...], w1_ref[...])
        bias = pb[0:1, :CMP_HIDDEN] + pb[8:9, CMP_HIDDEN:]
        rows = [[r_ref[0, half, pl.ds(l, nch, stride=CMP_STRIDE), :] for l in range(CMP_STRIDE)]
                for half in range(KV_WIDTH // LANES)]
        for g in range(N_KV_HEADS):
            half, odd = divmod(g, LANES // HEAD_DIM)
            sl = slice(odd * HEAD_DIM, (odd + 1) * HEAD_DIM)
            xg = jnp.concatenate([rw[:, sl] for rw in rows[half]], axis=1).astype(BF16)
            lohi = _dot(xg, w1_ref[...])
            lo = lohi[:, :CMP_HIDDEN]
            hi = lohi[:, CMP_HIDDEN:]
            hi_next = jnp.concatenate([hi[1:], jnp.zeros((1, CMP_HIDDEN), F32)], axis=0)
            hid = lo + hi_next + bias
            act = (hid * _sigmoid(hid)).astype(BF16)
            out = _dot(act, w2_ref[...])
            out = jnp.where(row < nch - 1, out, 0.0)
            out = out.astype(o_ref.dtype)
            o_ref[0, g] = out if tail is None else jnp.concatenate([out, tail], axis=1)


def _compress(kc_r, vc_r, kpe, kw1, kw2, vpe, vw1, vw2):
    b, _, t, _ = kc_r.shape
    nch = t // CMP_STRIDE
    blk_len = CMP_STRIDE * HEAD_DIM
    out_sds = [jax.ShapeDtypeStruct((b, N_KV_HEADS, nch, w), BF16) for w in (HEAD_DIM, 2 * HEAD_DIM)]
    out_spec = [pl.BlockSpec((1, N_KV_HEADS, nch, w), lambda bi: (bi, 0, 0, 0))
                for w in (HEAD_DIM, 2 * HEAD_DIM)]
    raw_spec = pl.BlockSpec((1, KV_WIDTH // LANES, t, LANES), lambda bi: (bi, 0, 0, 0))
    w_specs = [_const_spec((16, blk_len)), _const_spec((blk_len, 2 * CMP_HIDDEN)),
               _const_spec((CMP_HIDDEN, HEAD_DIM))]
    return pl.pallas_call(
        _compress_kernel,
        grid=(b,),
        in_specs=[raw_spec, raw_spec] + w_specs + w_specs,
        out_specs=tuple(out_spec),
        out_shape=tuple(out_sds),
        compiler_params=pltpu.CompilerParams(
            dimension_semantics=("arbitrary",), vmem_limit_bytes=VMEM_LIMIT),
        name="compress",
    )(kc_r, vc_r, kpe, kw1, kw2, vpe, vw1, vw2)


def _nsa_kernel(slope_ref, q_ref, kc_ref, vca_ref, ksa_ref, vsa_ref, kw_ref, vwa_ref, gate_ref,
                wmask_ref, o_ref, se_ref, so_ref, mx_ref, pa_ref, pb_ref, acc_ref,
                qa_ref, pw_ref, p4_ref, vg_ref, m1_ref, acc1_ref, list_ref):
    g = pl.program_id(1)
    ncp = kc_ref.shape[2]
    n_chunks = list_ref.shape[1] - 2
    slopes = [slope_ref[g * Q_PER_KV + r] for r in range(Q_PER_KV)]
    rows = [slice(r * TQ, (r + 1) * TQ) for r in range(Q_PER_KV)]
    halves = [slice(hf * LANES, (hf + 1) * LANES) for hf in range(TQ // LANES)]
    lane_row = lax.broadcasted_iota(jnp.int32, (1, TQ), 1)
    causal = (lax.broadcasted_iota(jnp.int32, (TQ, TQ), 1)
              <= lax.broadcasted_iota(jnp.int32, (TQ, TQ), 0))

    def select(u):
        i = pl.program_id(2) * TILES_PER_STEP + u
        t0 = i * TQ
        tile = slice(u * TQ, (u + 1) * TQ)
        q_full = q_ref[0, :, tile, :].reshape(Q_PER_KV * TQ, 2 * HEAD_DIM)
        q3 = q_full[:, :HEAD_DIM]

        s_c = _dot_nt(q3, kc_ref[0, 0])
        n_idx = lax.broadcasted_iota(jnp.int32, (TQ, ncp), 1)
        t_idx = t0 + lax.broadcasted_iota(jnp.int32, (TQ, ncp), 0)
        c_valid = n_idx * CMP_STRIDE + (CMP_LEN - 1) <= t_idx
        n_row = lax.broadcasted_iota(jnp.int32, (1, ncp), 1)
        c_pos = (n_row * CMP_STRIDE - t0).astype(F32) + 0.5 * (CMP_LEN - 1)
        vca = vca_ref[0, 0]
        o_c = []
        imp = jnp.zeros((TQ, LANES), F32)
        for r in range(Q_PER_KV):
            sr = jnp.where(c_valid, s_c[rows[r]] + slopes[r] * c_pos, -jnp.inf)
            m = jnp.max(sr, axis=1, keepdims=True)
            m = jnp.where(m == -jnp.inf, 0.0, m)
            p = jnp.exp2(sr - m)
            inv = 1.0 / jnp.maximum(jnp.sum(p, axis=1, keepdims=True), 1e-30)
            pv = _dot(p.astype(BF16), vca) * inv
            o_c.append(pv[:, :HEAD_DIM])
            imp = imp + pv

        imp_t = imp.T[HEAD_DIM:, :]
        j_t = lax.broadcasted_iota(jnp.int32, (LANES - HEAD_DIM, TQ), 0)
        cur_t = (t0 + lax.broadcasted_iota(jnp.int32, (LANES - HEAD_DIM, TQ), 1)) // SEL_BLOCK
        forced = (j_t == 0) | (j_t == cur_t) | (j_t == cur_t - 1)
        cand = jnp.where(j_t <= cur_t, imp_t + jnp.where(forced, FORCED_BONUS, 0.0), -jnp.inf)
        sel = jnp.zeros(cand.shape, jnp.bool_)
        for _ in range(SEL_TOPK):
            mx = jnp.max(cand, axis=0, keepdims=True)
            first = jnp.min(jnp.where(cand == mx, j_t, LANES), axis=0, keepdims=True)
            hit = j_t == first
            sel = sel | (hit & (mx > -jnp.inf))
            cand = jnp.where(hit, -jnp.inf, cand)
        bias_t = jnp.concatenate([jnp.zeros(cand.shape, F32), jnp.where(sel, 0.0, MASK_NEG)], axis=0)
        sel_bias = bias_t.T
        sel_bias_bf = sel_bias.astype(BF16)
        for r in range(Q_PER_KV):
            qa_ref[u, rows[r], :] = q_full[rows[r]] + sel_bias_bf

        blk_any = jnp.broadcast_to(jnp.max(sel_bias, axis=0, keepdims=True), (8, LANES))
        per_chunk = TQ // SEL_BLOCK
        chunk_any = blk_any
        for d in range(1, per_chunk):
            chunk_any = jnp.maximum(chunk_any, pltpu.roll(blk_any, LANES - d, 1))
        blk8 = lax.broadcasted_iota(jnp.int32, (8, LANES), 1) - HEAD_DIM
        chunk_bit = jnp.where((blk8 >= 0) & (blk8 % per_chunk == 0),
                              jnp.left_shift(1, jnp.maximum(blk8, 0) // per_chunk), 0).astype(F32)
        flagged = jnp.where(chunk_any > 0.5 * MASK_NEG, chunk_bit, 0.0)
        bits = jnp.sum(flagged[0:1, :], axis=1, keepdims=True)[0, 0].astype(jnp.int32)
        cnt = jnp.int32(0)
        idle = jnp.int32(0)
        for c in range(n_chunks):
            below = c < i
            hit = (((bits >> c) & 1) == 1) & below
            list_ref[u, cnt] = jnp.int32(c)
            cnt = cnt + hit.astype(jnp.int32)
            idle = jnp.where(below & jnp.logical_not(hit), c, idle)
        filler = jnp.where(i + 1 < n_chunks, i + 1, idle)
        list_ref[u, cnt] = filler

        return i, t0, tile, q3, o_c, cnt, filler

    def window(u, i, t0, tile, q3, o_c):
        w0 = pl.multiple_of(jnp.maximum(t0 - WINDOW, 0), TQ)
        s_w = _dot_nt(q3, kw_ref[0, 0, pl.ds(w0, WIN_KEYS), :])
        wk_row = w0 - t0 + lax.broadcasted_iota(jnp.int32, (1, WIN_KEYS), 1)
        w_mask = wmask_ref[jnp.minimum(i, WINDOW // TQ)]
        wk_pos = wk_row.astype(F32)
        for r in range(Q_PER_KV):
            sr = s_w[rows[r]] + slopes[r] * wk_pos + w_mask
            m = jnp.max(sr, axis=1, keepdims=True)
            pw_ref[u, rows[r], :] = jnp.exp2(sr - m).astype(BF16)
        acc_w = _dot(pw_ref[u], vwa_ref[0, 0, pl.ds(w0, WIN_KEYS), :])

        gates = gate_ref[0, tile, :]
        lane = lax.broadcasted_iota(jnp.int32, (TQ, LANES), 1)
        partial, g_sel = [], []
        for r in range(Q_PER_KV):
            col = g * (Q_PER_KV * N_BRANCH) + r * N_BRANCH
            gc, gs, gw = (jnp.sum(jnp.where(lane == col + br, gates, 0.0), axis=1, keepdims=True)
                          for br in range(N_BRANCH))
            a = acc_w[rows[r]]
            o_w = a[:, :HEAD_DIM] / a[:, HEAD_DIM:HEAD_DIM + 1]
            partial.append(gc * o_c[r] + gw * o_w)
            g_sel.append(jnp.broadcast_to(gs, (TQ, HEAD_DIM)))

        return partial, g_sel

    def head_of_list(u, i, t0, cnt, filler):
        head = [jnp.where(cnt > j, list_ref[u, j], filler) for j in range(SHORT_LIST - 1)] + [i]
        scores = []
        for j, c in enumerate(head):
            k0 = pl.multiple_of(c * TQ, TQ)
            vg_ref[u, j * TQ:(j + 1) * TQ, :] = vsa_ref[0, 0, pl.ds(k0, TQ), :]
            scores.append(_dot_nt(qa_ref[u], ksa_ref[0, 0, pl.ds(k0, TQ), :]))
        for r in range(Q_PER_KV):
            biased = []
            for j, c in enumerate(head):
                sr = scores[j][rows[r]] + slopes[r] * (c * TQ - t0 + lane_row).astype(F32)
                biased.append(jnp.where(causal, sr, MASK_NEG) if j == SHORT_LIST - 1 else sr)
            part = biased[0][:, halves[0]]
            for sr in biased:
                for hf in halves:
                    part = jnp.maximum(part, sr[:, hf])
            m = jnp.max(part, axis=1, keepdims=True)
            m1_ref[u, rows[r], :] = jnp.broadcast_to(m, (TQ, LANES))
            for j, sr in enumerate(biased):
                p4_ref[u, rows[r], j * TQ:(j + 1) * TQ] = jnp.exp2(sr - m).astype(BF16)
        acc1_ref[u] = _dot(p4_ref[u], vg_ref[u])

    def rest_of_list(u, t0, cnt):
        k_last = (cnt - SHORT_LIST + 2) // 2 - 1

        def entry(pos):
            return list_ref[u, SHORT_LIST - 1 + pos]

        def raw_scores(pos, dst_ref, k):
            k0 = pl.multiple_of(entry(pos) * TQ, TQ)
            s_all = _dot_nt(qa_ref[u], ksa_ref[0, 0, pl.ds(k0, TQ), :])
            for r in range(Q_PER_KV):
                dst_ref[k, rows[r], :] = s_all[rows[r]]

        def alibi_row(pos, r):
            return slopes[r] * (entry(pos) * TQ - t0 + lane_row).astype(F32)

        def fold_max(pos, src_ref, k):
            for r in range(Q_PER_KV):
                sr = src_ref[k, rows[r], :] + alibi_row(pos, r)
                part = sr[:, halves[0]]
                for hf in halves[1:]:
                    part = jnp.maximum(part, sr[:, hf])
                mx_ref[rows[r], :] = jnp.maximum(mx_ref[rows[r], :], part)

        mx_ref[...] = jnp.full(mx_ref.shape, M_INIT, F32)
        raw_scores(0, se_ref, 0)

        def pass1(k, carry):
            raw_scores(2 * k + 1, so_ref, k)
            fold_max(2 * k, se_ref, k)
            raw_scores(2 * k + 2, se_ref, k + 1)
            fold_max(2 * k + 1, so_ref, k)
            return carry

        lax.fori_loop(0, k_last, pass1, 0)
        raw_scores(2 * k_last + 1, so_ref, k_last)
        fold_max(2 * k_last, se_ref, k_last)
        fold_max(2 * k_last + 1, so_ref, k_last)

        for r in range(Q_PER_KV):
            m = jnp.max(mx_ref[rows[r], :], axis=1, keepdims=True)
            mx_ref[rows[r], :] = jnp.broadcast_to(m, (TQ, LANES))
        acc_ref[...] = jnp.zeros(acc_ref.shape, F32)

        def probs(pos, src_ref, k, dst_ref):
            for r in range(Q_PER_KV):
                mb = mx_ref[rows[r], :]
                bias = alibi_row(pos, r)
                for hf in halves:
                    dst_ref[rows[r], hf] = jnp.exp2(src_ref[k, rows[r], hf] + bias[:, hf] - mb).astype(BF16)

        def add_pv(pos, src_ref):
            k0 = pl.multiple_of(entry(pos) * TQ, TQ)
            acc_ref[...] += _dot(src_ref[...], vsa_ref[0, 0, pl.ds(k0, TQ), :])

        probs(0, se_ref, 0, pa_ref)

        def pass2(k, carry):
            add_pv(2 * k, pa_ref)
            probs(2 * k + 1, so_ref, k, pb_ref)
            add_pv(2 * k + 1, pb_ref)
            probs(2 * k + 2, se_ref, k + 1, pa_ref)
            return carry

        lax.fori_loop(0, k_last, pass2, 0)
        add_pv(2 * k_last, pa_ref)
        probs(2 * k_last + 1, so_ref, k_last, pb_ref)
        add_pv(2 * k_last + 1, pb_ref)

        for r in range(Q_PER_KV):
            m_head = m1_ref[u, rows[r], :]
            m_rest = mx_ref[rows[r], :]
            m = jnp.maximum(m_head, m_rest)
            acc1_ref[u, rows[r], :] = (acc1_ref[u, rows[r], :] * jnp.exp2(m_head - m)
                                       + acc_ref[rows[r], :] * jnp.exp2(m_rest - m))

    states = []
    picked = []
    for u in range(TILES_PER_STEP):
        i, t0, tile, q3, o_c, cnt, filler = select(u)
        partial, g_sel = window(u, i, t0, tile, q3, o_c)
        picked.append((i, t0, cnt, filler))
        states.append((t0, cnt, partial, g_sel))
    for u, (i, t0, cnt, filler) in enumerate(picked):
        head_of_list(u, i, t0, cnt, filler)
    for u, (t0, cnt, _, _) in enumerate(states):
        pl.when(cnt > SHORT_LIST - 1)(functools.partial(rest_of_list, u, t0, cnt))
    for u, (_, _, partial, g_sel) in enumerate(states):
        outs = []
        for r in range(Q_PER_KV):
            a = acc1_ref[u, rows[r], :]
            o_s = a[:, :HEAD_DIM] / a[:, HEAD_DIM:HEAD_DIM + 1]
            outs.append(partial[r] + g_sel[r] * o_s)
        o_ref[0, 0, u * TQ:(u + 1) * TQ, :] = jnp.concatenate(outs, axis=1).astype(o_ref.dtype)


def _window_masks():
    out = []
    for v in range(WINDOW // TQ + 1):
        rel0 = -v * TQ
        dist = np.arange(TQ)[:, None] - (rel0 + np.arange(WIN_KEYS)[None, :])
        out.append(np.where((dist >= 0) & (dist < WINDOW), 0.0, MASK_NEG))
    return jnp.asarray(np.stack(out), F32)


def _nsa(q, kc, vca, ksa, vsa, kw, vwa, gates, slopes):
    b, _, t, _ = q.shape
    ncp = kc.shape[2]
    tstep = TILES_PER_STEP * TQ
    grp = lambda bi, g, i: (bi, g, 0, 0)
    per_tile = lambda shape, dtype: pltpu.VMEM((TILES_PER_STEP,) + shape, dtype)
    return pl.pallas_call(
        _nsa_kernel,
        grid=(b, N_KV_HEADS, t // tstep),
        in_specs=[
            pl.BlockSpec(memory_space=pltpu.SMEM),
            pl.BlockSpec((1, Q_PER_KV, tstep, 2 * HEAD_DIM), lambda bi, g, i: (bi, g, i, 0)),
            pl.BlockSpec((1, 1, ncp, HEAD_DIM), grp),
            pl.BlockSpec((1, 1, ncp, 2 * HEAD_DIM), grp),
            pl.BlockSpec((1, 1, t, 2 * HEAD_DIM), grp),
            pl.BlockSpec((1, 1, t, 2 * HEAD_DIM), grp),
            pl.BlockSpec((1, 1, t, HEAD_DIM), grp),
            pl.BlockSpec((1, 1, t, 2 * HEAD_DIM), grp),
            pl.BlockSpec((1, tstep, GATE_PAD), lambda bi, g, i: (bi, i, 0)),
            _const_spec((WINDOW // TQ + 1, TQ, WIN_KEYS)),
        ],
        out_specs=pl.BlockSpec((1, 1, tstep, Q_PER_KV * HEAD_DIM), lambda bi, g, i: (bi, g, i, 0)),
        out_shape=jax.ShapeDtypeStruct((b, N_KV_HEADS, t, Q_PER_KV * HEAD_DIM), BF16),
        scratch_shapes=[
            pltpu.VMEM((t // (2 * TQ), Q_PER_KV * TQ, TQ), F32),
            pltpu.VMEM((t // (2 * TQ), Q_PER_KV * TQ, TQ), F32),
            pltpu.VMEM((Q_PER_KV * TQ, LANES), F32),
            pltpu.VMEM((Q_PER_KV * TQ, TQ), BF16),
            pltpu.VMEM((Q_PER_KV * TQ, TQ), BF16),
            pltpu.VMEM((Q_PER_KV * TQ, LANES), F32),
            per_tile((Q_PER_KV * TQ, 2 * HEAD_DIM), BF16),
            per_tile((Q_PER_KV * TQ, WIN_KEYS), BF16),
            per_tile((Q_PER_KV * TQ, SHORT_LIST * TQ), BF16),
            per_tile((SHORT_LIST * TQ, 2 * HEAD_DIM), BF16),
            per_tile((Q_PER_KV * TQ, LANES), F32),
            per_tile((Q_PER_KV * TQ, LANES), F32),
            pltpu.SMEM((TILES_PER_STEP, t // TQ + 2), jnp.int32),
        ],
        compiler_params=pltpu.CompilerParams(
            dimension_semantics=("arbitrary", "arbitrary", "arbitrary"),
            vmem_limit_bytes=VMEM_LIMIT),
        name="nsa_attention",
    )(slopes, q, kc, vca, ksa, vsa, kw, vwa, gates, _window_masks())


def _outffn_kernel(x_ref, c_ref, n_ref, woc_ref, won_ref, fg_ref, wgu_ref, wd_ref, fin_ref,
                   o_ref, acc_ref, h_ref, act_ref, *, final):
    x1 = x_ref[0] + _dot(c_ref[0], woc_ref[...])
    for g in range(N_KV_HEADS):
        x1 = x1 + _dot(n_ref[0, g], won_ref[g])
    ms = jnp.mean(x1 * x1, axis=-1, keepdims=True)
    h_ref[...] = (x1 * lax.rsqrt(ms + EPS) * fg_ref[...]).astype(BF16)
    acc_ref[...] = x1

    for c in range(N_FF_CHUNKS):
        h = h_ref[...]
        gate = _dot(h, wgu_ref[:, c * FF_CHUNK:(c + 1) * FF_CHUNK])
        up = _dot(h, wgu_ref[:, D_FF + c * FF_CHUNK:D_FF + (c + 1) * FF_CHUNK])
        act_ref[:, c * FF_CHUNK:(c + 1) * FF_CHUNK] = (gate * _sigmoid(gate) * up).astype(BF16)
    y = acc_ref[...] + _dot(act_ref[...], wd_ref[...])
    if final:
        ms = jnp.mean(y * y, axis=-1, keepdims=True)
        y = y * lax.rsqrt(ms + EPS) * fin_ref[...]
    o_ref[0] = y


def _outffn(x, conv_out, nsa_out, wo_c, wo_n, ffn_g, wgu, wd, final_g, final):
    b, t, _ = x.shape
    tm = min(TM_PROJ, t)
    tok = lambda bi, i: (bi, i, 0)
    return pl.pallas_call(
        functools.partial(_outffn_kernel, final=final),
        grid=(b, t // tm),
        in_specs=[
            pl.BlockSpec((1, tm, D_MODEL), tok),
            pl.BlockSpec((1, tm, CONV_CH), tok),
            pl.BlockSpec((1, N_KV_HEADS, tm, Q_PER_KV * HEAD_DIM), lambda bi, i: (bi, 0, i, 0)),
            _const_spec((CONV_CH, D_MODEL)),
            _const_spec((N_KV_HEADS, Q_PER_KV * HEAD_DIM, D_MODEL)),
            _const_spec((1, D_MODEL)),
            _const_spec((D_MODEL, 2 * D_FF)),
            _const_spec((D_FF, D_MODEL)),
            _const_spec((1, D_MODEL)),
        ],
        out_specs=pl.BlockSpec((1, tm, D_MODEL), tok),
        out_shape=jax.ShapeDtypeStruct((b, t, D_MODEL), F32),
        scratch_shapes=[pltpu.VMEM((tm, D_MODEL), F32), pltpu.VMEM((tm, D_MODEL), BF16),
                        pltpu.VMEM((tm, D_FF), BF16)],
        compiler_params=pltpu.CompilerParams(
            dimension_semantics=("arbitrary", "arbitrary"), vmem_limit_bytes=VMEM_LIMIT),
        name="outproj_ffn",
    )(x, conv_out, nsa_out, wo_c, wo_n, ffn_g, wgu, wd, final_g)


def _pe_rows(pe):
    lo = pe[:CMP_STRIDE].reshape(1, -1)
    hi = pe[CMP_STRIDE:].reshape(1, -1)
    z = jnp.zeros((7, lo.shape[1]), pe.dtype)
    return jnp.concatenate([lo, z, hi, z], axis=0).astype(BF16)


def _w1_cat(w1):
    lo = w1[:CMP_STRIDE].reshape(CMP_STRIDE * HEAD_DIM, CMP_HIDDEN)
    hi = w1[CMP_STRIDE:].reshape(CMP_STRIDE * HEAD_DIM, CMP_HIDDEN)
    return jnp.concatenate([lo, hi], axis=1).astype(BF16)


def kernel(x, attn_norm, w_in, conv_w, conv_b, conv_ln_g, conv_ln_b, cmp_k_pe, cmp_k_w1, cmp_k_w2,
           cmp_v_pe, cmp_v_w1, cmp_v_w2, w_out, ffn_norm, w_gate_up, w_down, final_norm):
    depth = w_in.shape[0]
    t = x.shape[1]
    assert t % (2 * TQ) == 0 and t % (TILES_PER_STEP * TQ) == 0
    assert t // SEL_BLOCK <= LANES - HEAD_DIM and t >= WIN_KEYS
    slopes = jnp.asarray(_alibi_slopes(N_Q_HEADS) * np.float32(LOG2E))
    final_g = final_norm.reshape(1, D_MODEL)
    for l in range(depth):
        w_pad = jnp.pad(w_in[l], ((0, 0), (0, IN_COLS_PAD - w_in.shape[2]))).astype(BF16)
        conv_out, q, kc_r, vc_r, ksa, vsa, kw, vwa, gates = _inproj(
            x, attn_norm[l].reshape(1, D_MODEL), w_pad,
            jnp.pad(conv_w[l], ((0, CONV_HALO - CONV_WIDTH), (0, 0))),
            conv_b[l].reshape(1, CONV_CH), conv_ln_g[l].reshape(1, CONV_CH),
            conv_ln_b[l].reshape(1, CONV_CH))
        kc, vca = _compress(kc_r, vc_r,
                            _pe_rows(cmp_k_pe[l]), _w1_cat(cmp_k_w1[l]), cmp_k_w2[l].astype(BF16),
                            _pe_rows(cmp_v_pe[l]), _w1_cat(cmp_v_w1[l]), cmp_v_w2[l].astype(BF16))
        nsa_out = _nsa(q, kc, vca, ksa, vsa, kw, vwa, gates, slopes)
        wo = w_out[l].astype(BF16)
        wo_n = wo[CONV_CH:].reshape(N_KV_HEADS, Q_PER_KV * HEAD_DIM, D_MODEL)
        x = _outffn(x, conv_out, nsa_out, wo[:CONV_CH], wo_n, ffn_norm[l].reshape(1, D_MODEL),
                    w_gate_up[l].astype(BF16), w_down[l].astype(BF16), final_g,
                    final=(l == depth - 1))
    return x
```

```python
import functools
import math

import jax
import jax.numpy as jnp
import numpy as np
from jax import lax
from jax.experimental import pallas as pl
from jax.experimental.pallas import tpu as pltpu

F32 = jnp.float32
BF16 = jnp.bfloat16

D_MODEL = 1024
HEAD_DIM = 64
CONV_CH = 256
CONV_WIDTH = 31
N_Q_HEADS = 12
N_KV_HEADS = 4
Q_PER_KV = N_Q_HEADS // N_KV_HEADS
NSA_WIDTH = N_Q_HEADS * HEAD_DIM
KV_WIDTH = N_KV_HEADS * HEAD_DIM
CMP_LEN = 32
CMP_STRIDE = 16
CMP_HIDDEN = 256
SEL_BLOCK = 64
SEL_TOPK = 8
WINDOW = 512
N_BRANCH = 3
FORCED_BONUS = 1000.0
D_FF = 2816
EPS = 1e-6
LOG2E = math.log2(math.e)

LANES = 128
SUBLANES = 8
GATE_PAD = LANES
FF_CHUNK = 256
N_FF_CHUNKS = D_FF // FF_CHUNK
TM_PROJ = 512
CONV_HALO = 32
CONV_ROWS = TM_PROJ
TQ = 256
WIN_KEYS = WINDOW + TQ
SHORT_LIST = 4
TILES_PER_STEP = 4
MASK_NEG = -(2.0 ** 100)
M_INIT = -1e30
VMEM_LIMIT = 56 * 1024 * 1024


def _alibi_slopes(n):
    def pow2_slopes(m):
        start = 2.0 ** (-8.0 / m)
        return [start ** (i + 1) for i in range(m)]
    if math.log2(n).is_integer():
        s = pow2_slopes(n)
    else:
        c = 2 ** math.floor(math.log2(n))
        s = pow2_slopes(c) + pow2_slopes(2 * c)[0::2][: n - c]
    return np.asarray(s, dtype=np.float32)


def _sigmoid(v):
    return 1.0 / (1.0 + jnp.exp(-v))


def _dot(a, b):
    return jnp.dot(a, b, preferred_element_type=F32)


def _dot_nt(a, b):
    return lax.dot_general(a, b, (((1,), (1,)), ((), ())), preferred_element_type=F32)


def _const_spec(shape):
    nd = len(shape)
    return pl.BlockSpec(shape, lambda *_: (0,) * nd, pipeline_mode=pl.Buffered(1))


C_A = 0
C_Q = C_A + 2 * CONV_CH
C_KC = C_Q + NSA_WIDTH
C_VC = C_KC + KV_WIDTH
C_KS = C_VC + KV_WIDTH
C_VS = C_KS + KV_WIDTH
C_KW = C_VS + KV_WIDTH
C_VW = C_KW + KV_WIDTH
C_G = C_VW + KV_WIDTH
IN_COLS_PAD = C_G + GATE_PAD


def _conv_mixer(a, w_ref, b_ref, lg_ref, lb_ref, y_ref, o_ref):
    tm = a.shape[0]
    y_ref[CONV_HALO:CONV_HALO + tm, :] = a[:, :CONV_CH] * _sigmoid(a[:, CONV_CH:])
    first = CONV_HALO - (CONV_WIDTH - 1)
    for r0 in range(0, tm, CONV_ROWS):
        acc = b_ref[...]
        for rho in range(SUBLANES):
            n_rows = CONV_ROWS if rho == 0 else CONV_ROWS + SUBLANES
            z = None
            for base in range(0, CONV_HALO + 1, SUBLANES):
                k = base + rho - first
                if 0 <= k < CONV_WIDTH:
                    term = w_ref[k:k + 1, :] * y_ref[pl.ds(r0 + base, n_rows), :]
                    z = term if z is None else z + term
            acc = acc + (z if rho == 0 else z[rho:rho + CONV_ROWS, :])
        mu = jnp.mean(acc, axis=-1, keepdims=True)
        d = acc - mu
        var = jnp.mean(d * d, axis=-1, keepdims=True)
        yn = d * lax.rsqrt(var + EPS) * lg_ref[...] + lb_ref[...]
        o_ref[0, r0:r0 + CONV_ROWS, :] = (yn * _sigmoid(yn)).astype(o_ref.dtype)


def _inproj_kernel(x_ref, g_ref, w_ref, cw_ref, cb_ref, clg_ref, clb_ref,
                   conv_ref, q_ref, kcr_ref, vcr_ref, ksa_ref, vsa_ref, kw_ref, vwa_ref, gate_ref,
                   y_ref):
    tm = x_ref.shape[1]

    @pl.when(pl.program_id(1) == 0)
    def _zero_halo():
        y_ref[0:CONV_HALO, :] = jnp.zeros((CONV_HALO, CONV_CH), F32)

    @pl.when(pl.program_id(1) > 0)
    def _carry_halo():
        y_ref[0:CONV_HALO, :] = y_ref[tm:tm + CONV_HALO, :]

    x = x_ref[0]
    ms = jnp.mean(x * x, axis=-1, keepdims=True)
    h = (x * lax.rsqrt(ms + EPS) * g_ref[...]).astype(BF16)

    zq = _dot(h, w_ref[:, C_Q:C_KC]) * (HEAD_DIM ** -0.5 * LOG2E)
    zero_hi = jnp.zeros((tm, HEAD_DIM), BF16)
    for hd in range(N_Q_HEADS):
        qh = zq[:, hd * HEAD_DIM:(hd + 1) * HEAD_DIM].astype(BF16)
        q_ref[0, hd] = jnp.concatenate([qh, zero_hi], axis=1)
    zkc = _dot(h, w_ref[:, C_KC:C_VC])
    zvc = _dot(h, w_ref[:, C_VC:C_KS])
    for half in range(KV_WIDTH // LANES):
        kcr_ref[0, half] = zkc[:, half * LANES:(half + 1) * LANES]
        vcr_ref[0, half] = zvc[:, half * LANES:(half + 1) * LANES]

    a = _dot(h, w_ref[:, C_A:C_Q])
    _conv_mixer(a, cw_ref, cb_ref, clg_ref, clb_ref, y_ref, conv_ref)

    t = pl.program_id(1) * tm + lax.broadcasted_iota(jnp.int32, (tm, HEAD_DIM), 0)
    blk = lax.broadcasted_iota(jnp.int32, (tm, HEAD_DIM), 1)
    onehot = jnp.where(t // SEL_BLOCK == blk, 1.0, 0.0).astype(BF16)
    ones_col = jnp.where(blk == 0, 1.0, 0.0).astype(BF16)
    zks = _dot(h, w_ref[:, C_KS:C_VS])
    zvs = _dot(h, w_ref[:, C_VS:C_KW])
    zkw = _dot(h, w_ref[:, C_KW:C_VW])
    zvw = _dot(h, w_ref[:, C_VW:C_G])
    for g in range(N_KV_HEADS):
        sl = slice(g * HEAD_DIM, (g + 1) * HEAD_DIM)
        ksa_ref[0, g] = jnp.concatenate([zks[:, sl].astype(BF16), onehot], axis=1)
        vsa_ref[0, g] = jnp.concatenate([zvs[:, sl].astype(BF16), ones_col], axis=1)
        kw_ref[0, g] = zkw[:, sl].astype(BF16)
        vwa_ref[0, g] = jnp.concatenate([zvw[:, sl].astype(BF16), ones_col], axis=1)
    gate_ref[0] = _sigmoid(_dot(h, w_ref[:, C_G:IN_COLS_PAD]))


def _inproj(x, norm_g, w_pad, conv_w, conv_b, conv_ln_g, conv_ln_b):
    b, t, _ = x.shape
    tm = min(TM_PROJ, t)
    grid = (b, t // tm)
    tok = lambda bi, i: (bi, i, 0)
    head = lambda bi, i: (bi, 0, i, 0)
    out_shape = (
        jax.ShapeDtypeStruct((b, t, CONV_CH), BF16),
        jax.ShapeDtypeStruct((b, N_Q_HEADS, t, 2 * HEAD_DIM), BF16),
        jax.ShapeDtypeStruct((b, KV_WIDTH // LANES, t, LANES), F32),
        jax.ShapeDtypeStruct((b, KV_WIDTH // LANES, t, LANES), F32),
        jax.ShapeDtypeStruct((b, N_KV_HEADS, t, 2 * HEAD_DIM), BF16),
        jax.ShapeDtypeStruct((b, N_KV_HEADS, t, 2 * HEAD_DIM), BF16),
        jax.ShapeDtypeStruct((b, N_KV_HEADS, t, HEAD_DIM), BF16),
        jax.ShapeDtypeStruct((b, N_KV_HEADS, t, 2 * HEAD_DIM), BF16),
        jax.ShapeDtypeStruct((b, t, GATE_PAD), F32),
    )
    out_specs = (
        pl.BlockSpec((1, tm, CONV_CH), tok),
        pl.BlockSpec((1, N_Q_HEADS, tm, 2 * HEAD_DIM), head),
        pl.BlockSpec((1, KV_WIDTH // LANES, tm, LANES), head),
        pl.BlockSpec((1, KV_WIDTH // LANES, tm, LANES), head),
        pl.BlockSpec((1, N_KV_HEADS, tm, 2 * HEAD_DIM), head),
        pl.BlockSpec((1, N_KV_HEADS, tm, 2 * HEAD_DIM), head),
        pl.BlockSpec((1, N_KV_HEADS, tm, HEAD_DIM), head),
        pl.BlockSpec((1, N_KV_HEADS, tm, 2 * HEAD_DIM), head),
        pl.BlockSpec((1, tm, GATE_PAD), tok),
    )
    return pl.pallas_call(
        _inproj_kernel,
        grid=grid,
        in_specs=[
            pl.BlockSpec((1, tm, D_MODEL), tok),
            _const_spec((1, D_MODEL)),
            _const_spec((D_MODEL, IN_COLS_PAD)),
            _const_spec((CONV_HALO, CONV_CH)),
            _const_spec((1, CONV_CH)),
            _const_spec((1, CONV_CH)),
            _const_spec((1, CONV_CH)),
        ],
        out_specs=out_specs,
        out_shape=out_shape,
        scratch_shapes=[pltpu.VMEM((CONV_HALO + tm, CONV_CH), F32)],
        compiler_params=pltpu.CompilerParams(
            dimension_semantics=("arbitrary", "arbitrary"), vmem_limit_bytes=VMEM_LIMIT),
        name="inproj_conv",
    )(x, norm_g, w_pad, conv_w, conv_b, conv_ln_g, conv_ln_b)


def _compress_kernel(kr_ref, vr_ref, kpe_ref, kw1_ref, kw2_ref, vpe_ref, vw1_ref, vw2_ref,
                     kc_ref, vc_ref):
    nch = kr_ref.shape[2] // CMP_STRIDE
    row = lax.broadcasted_iota(jnp.int32, (nch, HEAD_DIM), 0)
    c_start = row * CMP_STRIDE
    s_start = lax.broadcasted_iota(jnp.int32, (nch, HEAD_DIM), 1) * SEL_BLOCK
    overlap = jnp.where((c_start < s_start + SEL_BLOCK) & (c_start + CMP_LEN > s_start),
                        1.0, 0.0).astype(BF16)
    for r_ref, pe_ref, w1_ref, w2_ref, o_ref, tail in (
            (kr_ref, kpe_ref, kw1_ref, kw2_ref, kc_ref, None),
            (vr_ref, vpe_ref, vw1_ref, vw2_ref, vc_ref, overlap)):
        pb = _dot(pe_ref[...], w1_ref[...])
        bias = pb[0:1, :CMP_HIDDEN] + pb[8:9, CMP_HIDDEN:]
        rows = [[r_ref[0, half, pl.ds(l, nch, stride=CMP_STRIDE), :] for l in range(CMP_STRIDE)]
                for half in range(KV_WIDTH // LANES)]
        for g in range(N_KV_HEADS):
            half, odd = divmod(g, LANES // HEAD_DIM)
            sl = slice(odd * HEAD_DIM, (odd + 1) * HEAD_DIM)
            xg = jnp.concatenate([rw[:, sl] for rw in rows[half]], axis=1).astype(BF16)
            lohi = _dot(xg, w1_ref[...])
            lo = lohi[:, :CMP_HIDDEN]
            hi = lohi[:, CMP_HIDDEN:]
            hi_next = jnp.concatenate([hi[1:], jnp.zeros((1, CMP_HIDDEN), F32)], axis=0)
            hid = lo + hi_next + bias
            act = (hid * _sigmoid(hid)).astype(BF16)
            out = _dot(act, w2_ref[...])
            out = jnp.where(row < nch - 1, out, 0.0)
            out = out.astype(o_ref.dtype)
            o_ref[0, g] = out if tail is None else jnp.concatenate([out, tail], axis=1)


def _compress(kc_r, vc_r, kpe, kw1, kw2, vpe, vw1, vw2):
    b, _, t, _ = kc_r.shape
    nch = t // CMP_STRIDE
    blk_len = CMP_STRIDE * HEAD_DIM
    out_sds = [jax.ShapeDtypeStruct((b, N_KV_HEADS, nch, w), BF16) for w in (HEAD_DIM, 2 * HEAD_DIM)]
    out_spec = [pl.BlockSpec((1, N_KV_HEADS, nch, w), lambda bi: (bi, 0, 0, 0))
                for w in (HEAD_DIM, 2 * HEAD_DIM)]
    raw_spec = pl.BlockSpec((1, KV_WIDTH // LANES, t, LANES), lambda bi: (bi, 0, 0, 0))
    w_specs = [_const_spec((16, blk_len)), _const_spec((blk_len, 2 * CMP_HIDDEN)),
               _const_spec((CMP_HIDDEN, HEAD_DIM))]
    return pl.pallas_call(
        _compress_kernel,
        grid=(b,),
        in_specs=[raw_spec, raw_spec] + w_specs + w_specs,
        out_specs=tuple(out_spec),
        out_shape=tuple(out_sds),
        compiler_params=pltpu.CompilerParams(
            dimension_semantics=("arbitrary",), vmem_limit_bytes=VMEM_LIMIT),
        name="compress",
    )(kc_r, vc_r, kpe, kw1, kw2, vpe, vw1, vw2)


def _nsa_kernel(slope_ref, q_ref, kc_ref, vca_ref, ksa_ref, vsa_ref, kw_ref, vwa_ref, gate_ref,
                wmask_ref, o_ref, se_ref, so_ref, mx_ref, pa_ref, pb_ref, acc_ref,
                qa_ref, pw_ref, p4_ref, vg_ref, m1_ref, acc1_ref, list_ref):
    g = pl.program_id(1)
    ncp = kc_ref.shape[2]
    n_chunks = list_ref.shape[1] - 2
    slopes = [slope_ref[g * Q_PER_KV + r] for r in range(Q_PER_KV)]
    rows = [slice(r * TQ, (r + 1) * TQ) for r in range(Q_PER_KV)]
    halves = [slice(hf * LANES, (hf + 1) * LANES) for hf in range(TQ // LANES)]
    lane_row = lax.broadcasted_iota(jnp.int32, (1, TQ), 1)
    causal = (lax.broadcasted_iota(jnp.int32, (TQ, TQ), 1)
              <= lax.broadcasted_iota(jnp.int32, (TQ, TQ), 0))

    def select(u):
        i = pl.program_id(2) * TILES_PER_STEP + u
        t0 = i * TQ
        tile = slice(u * TQ, (u + 1) * TQ)
        q_full = q_ref[0, :, tile, :].reshape(Q_PER_KV * TQ, 2 * HEAD_DIM)
        q3 = q_full[:, :HEAD_DIM]

        s_c = _dot_nt(q3, kc_ref[0, 0])
        n_idx = lax.broadcasted_iota(jnp.int32, (TQ, ncp), 1)
        t_idx = t0 + lax.broadcasted_iota(jnp.int32, (TQ, ncp), 0)
        c_valid = n_idx * CMP_STRIDE + (CMP_LEN - 1) <= t_idx
        n_row = lax.broadcasted_iota(jnp.int32, (1, ncp), 1)
        c_pos = (n_row * CMP_STRIDE - t0).astype(F32) + 0.5 * (CMP_LEN - 1)
        vca = vca_ref[0, 0]
        o_c = []
        imp = jnp.zeros((TQ, LANES), F32)
        for r in range(Q_PER_KV):
            sr = jnp.where(c_valid, s_c[rows[r]] + slopes[r] * c_pos, -jnp.inf)
            m = jnp.max(sr, axis=1, keepdims=True)
            m = jnp.where(m == -jnp.inf, 0.0, m)
            p = jnp.exp2(sr - m)
            inv = 1.0 / jnp.maximum(jnp.sum(p, axis=1, keepdims=True), 1e-30)
            pv = _dot(p.astype(BF16), vca) * inv
            o_c.append(pv[:, :HEAD_DIM])
            imp = imp + pv

        imp_t = imp.T[HEAD_DIM:, :]
        j_t = lax.broadcasted_iota(jnp.int32, (LANES - HEAD_DIM, TQ), 0)
        cur_t = (t0 + lax.broadcasted_iota(jnp.int32, (LANES - HEAD_DIM, TQ), 1)) // SEL_BLOCK
        forced = (j_t == 0) | (j_t == cur_t) | (j_t == cur_t - 1)
        cand = jnp.where(j_t <= cur_t, imp_t + jnp.where(forced, FORCED_BONUS, 0.0), -jnp.inf)
        sel = jnp.zeros(cand.shape, jnp.bool_)
        for _ in range(SEL_TOPK):
            mx = jnp.max(cand, axis=0, keepdims=True)
            first = jnp.min(jnp.where(cand == mx, j_t, LANES), axis=0, keepdims=True)
            hit = j_t == first
            sel = sel | (hit & (mx > -jnp.inf))
            cand = jnp.where(hit, -jnp.inf, cand)
        bias_t = jnp.concatenate([jnp.zeros(cand.shape, F32), jnp.where(sel, 0.0, MASK_NEG)], axis=0)
        sel_bias = bias_t.T
        sel_bias_bf = sel_bias.astype(BF16)
        for r in range(Q_PER_KV):
            qa_ref[u, rows[r], :] = q_full[rows[r]] + sel_bias_bf

        blk_any = jnp.broadcast_to(jnp.max(sel_bias, axis=0, keepdims=True), (8, LANES))
        per_chunk = TQ // SEL_BLOCK
        chunk_any = blk_any
        for d in range(1, per_chunk):
            chunk_any = jnp.maximum(chunk_any, pltpu.roll(blk_any, LANES - d, 1))
        blk8 = lax.broadcasted_iota(jnp.int32, (8, LANES), 1) - HEAD_DIM
        chunk_bit = jnp.where((blk8 >= 0) & (blk8 % per_chunk == 0),
                              jnp.left_shift(1, jnp.maximum(blk8, 0) // per_chunk), 0).astype(F32)
        flagged = jnp.where(chunk_any > 0.5 * MASK_NEG, chunk_bit, 0.0)
        bits = jnp.sum(flagged[0:1, :], axis=1, keepdims=True)[0, 0].astype(jnp.int32)
        cnt = jnp.int32(0)
        idle = jnp.int32(0)
        for c in range(n_chunks):
            below = c < i
            hit = (((bits >> c) & 1) == 1) & below
            list_ref[u, cnt] = jnp.int32(c)
            cnt = cnt + hit.astype(jnp.int32)
            idle = jnp.where(below & jnp.logical_not(hit), c, idle)
        filler = jnp.where(i + 1 < n_chunks, i + 1, idle)
        list_ref[u, cnt] = filler

        return i, t0, tile, q3, o_c, cnt, filler

    def window(u, i, t0, tile, q3, o_c):
        w0 = pl.multiple_of(jnp.maximum(t0 - WINDOW, 0), TQ)
        s_w = _dot_nt(q3, kw_ref[0, 0, pl.ds(w0, WIN_KEYS), :])
        wk_row = w0 - t0 + lax.broadcasted_iota(jnp.int32, (1, WIN_KEYS), 1)
        w_mask = wmask_ref[jnp.minimum(i, WINDOW // TQ)]
        wk_pos = wk_row.astype(F32)
        for r in range(Q_PER_KV):
            sr = s_w[rows[r]] + slopes[r] * wk_pos + w_mask
            m = jnp.max(sr, axis=1, keepdims=True)
            pw_ref[u, rows[r], :] = jnp.exp2(sr - m).astype(BF16)
        acc_w = _dot(pw_ref[u], vwa_ref[0, 0, pl.ds(w0, WIN_KEYS), :])

        gates = gate_ref[0, tile, :]
        lane = lax.broadcasted_iota(jnp.int32, (TQ, LANES), 1)
        partial, g_sel = [], []
        for r in range(Q_PER_KV):
            col = g * (Q_PER_KV * N_BRANCH) + r * N_BRANCH
            gc, gs, gw = (jnp.sum(jnp.where(lane == col + br, gates, 0.0), axis=1, keepdims=True)
                          for br in range(N_BRANCH))
            a = acc_w[rows[r]]
            o_w = a[:, :HEAD_DIM] / a[:, HEAD_DIM:HEAD_DIM + 1]
            partial.append(gc * o_c[r] + gw * o_w)
            g_sel.append(jnp.broadcast_to(gs, (TQ, HEAD_DIM)))

        return partial, g_sel

    def head_of_list(u, i, t0, cnt, filler):
        head = [jnp.where(cnt > j, list_ref[u, j], filler) for j in range(SHORT_LIST - 1)] + [i]
        scores = []
        for j, c in enumerate(head):
            k0 = pl.multiple_of(c * TQ, TQ)
            vg_ref[u, j * TQ:(j + 1) * TQ, :] = vsa_ref[0, 0, pl.ds(k0, TQ), :]
            scores.append(_dot_nt(qa_ref[u], ksa_ref[0, 0, pl.ds(k0, TQ), :]))
        for r in range(Q_PER_KV):
            biased = []
            for j, c in enumerate(head):
                sr = scores[j][rows[r]] + slopes[r] * (c * TQ - t0 + lane_row).astype(F32)
                biased.append(jnp.where(causal, sr, MASK_NEG) if j == SHORT_LIST - 1 else sr)
            part = biased[0][:, halves[0]]
            for sr in biased:
                for hf in halves:
                    part = jnp.maximum(part, sr[:, hf])
            m = jnp.max(part, axis=1, keepdims=True)
            m1_ref[u, rows[r], :] = jnp.broadcast_to(m, (TQ, LANES))
            for j, sr in enumerate(biased):
                p4_ref[u, rows[r], j * TQ:(j + 1) * TQ] = jnp.exp2(sr - m).astype(BF16)
        acc1_ref[u] = _dot(p4_ref[u], vg_ref[u])

    def rest_of_list(u, t0, cnt):
        k_last = (cnt - SHORT_LIST + 2) // 2 - 1

        def entry(pos):
            return list_ref[u, SHORT_LIST - 1 + pos]

        def raw_scores(pos, dst_ref, k):
            k0 = pl.multiple_of(entry(pos) * TQ, TQ)
            s_all = _dot_nt(qa_ref[u], ksa_ref[0, 0, pl.ds(k0, TQ), :])
            for r in range(Q_PER_KV):
                dst_ref[k, rows[r], :] = s_all[rows[r]]

        def alibi_row(pos, r):
            return slopes[r] * (entry(pos) * TQ - t0 + lane_row).astype(F32)

        def fold_max(pos, src_ref, k):
            for r in range(Q_PER_KV):
                sr = src_ref[k, rows[r], :] + alibi_row(pos, r)
                part = sr[:, halves[0]]
                for hf in halves[1:]:
                    part = jnp.maximum(part, sr[:, hf])
                mx_ref[rows[r], :] = jnp.maximum(mx_ref[rows[r], :], part)

        mx_ref[...] = jnp.full(mx_ref.shape, M_INIT, F32)
        raw_scores(0, se_ref, 0)

        def pass1(k, carry):
            raw_scores(2 * k + 1, so_ref, k)
            fold_max(2 * k, se_ref, k)
            raw_scores(2 * k + 2, se_ref, k + 1)
            fold_max(2 * k + 1, so_ref, k)
            return carry

        lax.fori_loop(0, k_last, pass1, 0)
        raw_scores(2 * k_last + 1, so_ref, k_last)
        fold_max(2 * k_last, se_ref, k_last)
        fold_max(2 * k_last + 1, so_ref, k_last)

        for r in range(Q_PER_KV):
            m = jnp.max(mx_ref[rows[r], :], axis=1, keepdims=True)
            mx_ref[rows[r], :] = jnp.broadcast_to(m, (TQ, LANES))
        acc_ref[...] = jnp.zeros(acc_ref.shape, F32)

        def probs(pos, src_ref, k, dst_ref):
            for r in range(Q_PER_KV):
                mb = mx_ref[rows[r], :]
                bias = alibi_row(pos, r)
                for hf in halves:
                    dst_ref[rows[r], hf] = jnp.exp2(src_ref[k, rows[r], hf] + bias[:, hf] - mb).astype(BF16)

        def add_pv(pos, src_ref):
            k0 = pl.multiple_of(entry(pos) * TQ, TQ)
            acc_ref[...] += _dot(src_ref[...], vsa_ref[0, 0, pl.ds(k0, TQ), :])

        probs(0, se_ref, 0, pa_ref)

        def pass2(k, carry):
            add_pv(2 * k, pa_ref)
            probs(2 * k + 1, so_ref, k, pb_ref)
            add_pv(2 * k + 1, pb_ref)
            probs(2 * k + 2, se_ref, k + 1, pa_ref)
            return carry

        lax.fori_loop(0, k_last, pass2, 0)
        add_pv(2 * k_last, pa_ref)
        probs(2 * k_last + 1, so_ref, k_last, pb_ref)
        add_pv(2 * k_last + 1, pb_ref)

        for r in range(Q_PER_KV):
            m_head = m1_ref[u, rows[r], :]
            m_rest = mx_ref[rows[r], :]
            m = jnp.maximum(m_head, m_rest)
            acc1_ref[u, rows[r], :] = (acc1_ref[u, rows[r], :] * jnp.exp2(m_head - m)
                                       + acc_ref[rows[r], :] * jnp.exp2(m_rest - m))

    states = []
    picked = []
    sel_out = [select(u) for u in range(TILES_PER_STEP)]
    for u in range(TILES_PER_STEP):
        i, t0, tile, q3, o_c, cnt, filler = sel_out[u]
        partial, g_sel = window(u, i, t0, tile, q3, o_c)
        picked.append((i, t0, cnt, filler))
        states.append((t0, cnt, partial, g_sel))
    for u, (i, t0, cnt, filler) in enumerate(picked):
        head_of_list(u, i, t0, cnt, filler)
    for u, (t0, cnt, _, _) in enumerate(states):
        pl.when(cnt > SHORT_LIST - 1)(functools.partial(rest_of_list, u, t0, cnt))
    for u, (_, _, partial, g_sel) in enumerate(states):
        outs = []
        for r in range(Q_PER_KV):
            a = acc1_ref[u, rows[r], :]
            o_s = a[:, :HEAD_DIM] / a[:, HEAD_DIM:HEAD_DIM + 1]
            outs.append(partial[r] + g_sel[r] * o_s)
        o_ref[0, 0, u * TQ:(u + 1) * TQ, :] = jnp.concatenate(outs, axis=1).astype(o_ref.dtype)


def _window_masks():
    out = []
    for v in range(WINDOW // TQ + 1):
        rel0 = -v * TQ
        dist = np.arange(TQ)[:, None] - (rel0 + np.arange(WIN_KEYS)[None, :])
        out.append(np.where((dist >= 0) & (dist < WINDOW), 0.0, MASK_NEG))
    return jnp.asarray(np.stack(out), F32)


def _nsa(q, kc, vca, ksa, vsa, kw, vwa, gates, slopes):
    b, _, t, _ = q.shape
    ncp = kc.shape[2]
    tstep = TILES_PER_STEP * TQ
    grp = lambda bi, g, i: (bi, g, 0, 0)
    per_tile = lambda shape, dtype: pltpu.VMEM((TILES_PER_STEP,) + shape, dtype)
    return pl.pallas_call(
        _nsa_kernel,
        grid=(b, N_KV_HEADS, t // tstep),
        in_specs=[
            pl.BlockSpec(memory_space=pltpu.SMEM),
            pl.BlockSpec((1, Q_PER_KV, tstep, 2 * HEAD_DIM), lambda bi, g, i: (bi, g, i, 0)),
            pl.BlockSpec((1, 1, ncp, HEAD_DIM), grp),
            pl.BlockSpec((1, 1, ncp, 2 * HEAD_DIM), grp),
            pl.BlockSpec((1, 1, t, 2 * HEAD_DIM), grp),
            pl.BlockSpec((1, 1, t, 2 * HEAD_DIM), grp),
            pl.BlockSpec((1, 1, t, HEAD_DIM), grp),
            pl.BlockSpec((1, 1, t, 2 * HEAD_DIM), grp),
            pl.BlockSpec((1, tstep, GATE_PAD), lambda bi, g, i: (bi, i, 0)),
            _const_spec((WINDOW // TQ + 1, TQ, WIN_KEYS)),
        ],
        out_specs=pl.BlockSpec((1, 1, tstep, Q_PER_KV * HEAD_DIM), lambda bi, g, i: (bi, g, i, 0)),
        out_shape=jax.ShapeDtypeStruct((b, N_KV_HEADS, t, Q_PER_KV * HEAD_DIM), BF16),
        scratch_shapes=[
            pltpu.VMEM((t // (2 * TQ), Q_PER_KV * TQ, TQ), F32),
            pltpu.VMEM((t // (2 * TQ), Q_PER_KV * TQ, TQ), F32),
            pltpu.VMEM((Q_PER_KV * TQ, LANES), F32),
            pltpu.VMEM((Q_PER_KV * TQ, TQ), BF16),
            pltpu.VMEM((Q_PER_KV * TQ, TQ), BF16),
            pltpu.VMEM((Q_PER_KV * TQ, LANES), F32),
            per_tile((Q_PER_KV * TQ, 2 * HEAD_DIM), BF16),
            per_tile((Q_PER_KV * TQ, WIN_KEYS), BF16),
            per_tile((Q_PER_KV * TQ, SHORT_LIST * TQ), BF16),
            per_tile((SHORT_LIST * TQ, 2 * HEAD_DIM), BF16),
            per_tile((Q_PER_KV * TQ, LANES), F32),
            per_tile((Q_PER_KV * TQ, LANES), F32),
            pltpu.SMEM((TILES_PER_STEP, t // TQ + 2), jnp.int32),
        ],
        compiler_params=pltpu.CompilerParams(
            dimension_semantics=("arbitrary", "arbitrary", "arbitrary"),
            vmem_limit_bytes=VMEM_LIMIT),
        name="nsa_attention",
    )(slopes, q, kc, vca, ksa, vsa, kw, vwa, gates, _window_masks())


def _outffn_kernel(x_ref, c_ref, n_ref, woc_ref, won_ref, fg_ref, wgu_ref, wd_ref, fin_ref,
                   o_ref, acc_ref, h_ref, act_ref, *, final):
    x1 = x_ref[0] + _dot(c_ref[0], woc_ref[...])
    for g in range(N_KV_HEADS):
        x1 = x1 + _dot(n_ref[0, g], won_ref[g])
    ms = jnp.mean(x1 * x1, axis=-1, keepdims=True)
    h_ref[...] = (x1 * lax.rsqrt(ms + EPS) * fg_ref[...]).astype(BF16)
    acc_ref[...] = x1

    for c in range(N_FF_CHUNKS):
        h = h_ref[...]
        gate = _dot(h, wgu_ref[:, c * FF_CHUNK:(c + 1) * FF_CHUNK])
        up = _dot(h, wgu_ref[:, D_FF + c * FF_CHUNK:D_FF + (c + 1) * FF_CHUNK])
        act_ref[:, c * FF_CHUNK:(c + 1) * FF_CHUNK] = (gate * _sigmoid(gate) * up).astype(BF16)
    y = acc_ref[...] + _dot(act_ref[...], wd_ref[...])
    if final:
        ms = jnp.mean(y * y, axis=-1, keepdims=True)
        y = y * lax.rsqrt(ms + EPS) * fin_ref[...]
    o_ref[0] = y


def _outffn(x, conv_out, nsa_out, wo_c, wo_n, ffn_g, wgu, wd, final_g, final):
    b, t, _ = x.shape
    tm = min(TM_PROJ, t)
    tok = lambda bi, i: (bi, i, 0)
    return pl.pallas_call(
        functools.partial(_outffn_kernel, final=final),
        grid=(b, t // tm),
        in_specs=[
            pl.BlockSpec((1, tm, D_MODEL), tok),
            pl.BlockSpec((1, tm, CONV_CH), tok),
            pl.BlockSpec((1, N_KV_HEADS, tm, Q_PER_KV * HEAD_DIM), lambda bi, i: (bi, 0, i, 0)),
            _const_spec((CONV_CH, D_MODEL)),
            _const_spec((N_KV_HEADS, Q_PER_KV * HEAD_DIM, D_MODEL)),
            _const_spec((1, D_MODEL)),
            _const_spec((D_MODEL, 2 * D_FF)),
            _const_spec((D_FF, D_MODEL)),
            _const_spec((1, D_MODEL)),
        ],
        out_specs=pl.BlockSpec((1, tm, D_MODEL), tok),
        out_shape=jax.ShapeDtypeStruct((b, t, D_MODEL), F32),
        scratch_shapes=[pltpu.VMEM((tm, D_MODEL), F32), pltpu.VMEM((tm, D_MODEL), BF16),
                        pltpu.VMEM((tm, D_FF), BF16)],
        compiler_params=pltpu.CompilerParams(
            dimension_semantics=("arbitrary", "arbitrary"), vmem_limit_bytes=VMEM_LIMIT),
        name="outproj_ffn",
    )(x, conv_out, nsa_out, wo_c, wo_n, ffn_g, wgu, wd, final_g)


def _pe_rows(pe):
    lo = pe[:CMP_STRIDE].reshape(1, -1)
    hi = pe[CMP_STRIDE:].reshape(1, -1)
    z = jnp.zeros((7, lo.shape[1]), pe.dtype)
    return jnp.concatenate([lo, z, hi, z], axis=0).astype(BF16)


def _w1_cat(w1):
    lo = w1[:CMP_STRIDE].reshape(CMP_STRIDE * HEAD_DIM, CMP_HIDDEN)
    hi = w1[CMP_STRIDE:].reshape(CMP_STRIDE * HEAD_DIM, CMP_HIDDEN)
    return jnp.concatenate([lo, hi], axis=1).astype(BF16)


def kernel(x, attn_norm, w_in, conv_w, conv_b, conv_ln_g, conv_ln_b, cmp_k_pe, cmp_k_w1, cmp_k_w2,
           cmp_v_pe, cmp_v_w1, cmp_v_w2, w_out, ffn_norm, w_gate_up, w_down, final_norm):
    depth = w_in.shape[0]
    t = x.shape[1]
    assert t % (2 * TQ) == 0 and t % (TILES_PER_STEP * TQ) == 0
    assert t // SEL_BLOCK <= LANES - HEAD_DIM and t >= WIN_KEYS
    slopes = jnp.asarray(_alibi_slopes(N_Q_HEADS) * np.float32(LOG2E))
    final_g = final_norm.reshape(1, D_MODEL)
    for l in range(depth):
        w_pad = jnp.pad(w_in[l], ((0, 0), (0, IN_COLS_PAD - w_in.shape[2]))).astype(BF16)
        conv_out, q, kc_r, vc_r, ksa, vsa, kw, vwa, gates = _inproj(
            x, attn_norm[l].reshape(1, D_MODEL), w_pad,
            jnp.pad(conv_w[l], ((0, CONV_HALO - CONV_WIDTH), (0, 0))),
            conv_b[l].reshape(1, CONV_CH), conv_ln_g[l].reshape(1, CONV_CH),
            conv_ln_b[l].reshape(1, CONV_CH))
        kc, vca = _compress(kc_r, vc_r,
                            _pe_rows(cmp_k_pe[l]), _w1_cat(cmp_k_w1[l]), cmp_k_w2[l].astype(BF16),
                            _pe_rows(cmp_v_pe[l]), _w1_cat(cmp_v_w1[l]), cmp_v_w2[l].astype(BF16))
        nsa_out = _nsa(q, kc, vca, ksa, vsa, kw, vwa, gates, slopes)
        wo = w_out[l].astype(BF16)
        wo_n = wo[CONV_CH:].reshape(N_KV_HEADS, Q_PER_KV * HEAD_DIM, D_MODEL)
        x = _outffn(x, conv_out, nsa_out, wo[:CONV_CH], wo_n, ffn_norm[l].reshape(1, D_MODEL),
                    w_gate_up[l].astype(BF16), w_down[l].astype(BF16), final_g,
                    final=(l == depth - 1))
    return x
```

```python
import functools
import math

import jax
import jax.numpy as jnp
import numpy as np
from jax import lax
from jax.experimental import pallas as pl
from jax.experimental.pallas import tpu as pltpu

F32 = jnp.float32
BF16 = jnp.bfloat16

D_MODEL = 1024
HEAD_DIM = 64
CONV_CH = 256
CONV_WIDTH = 31
N_Q_HEADS = 12
N_KV_HEADS = 4
Q_PER_KV = N_Q_HEADS // N_KV_HEADS
NSA_WIDTH = N_Q_HEADS * HEAD_DIM
KV_WIDTH = N_KV_HEADS * HEAD_DIM
CMP_LEN = 32
CMP_STRIDE = 16
CMP_HIDDEN = 256
SEL_BLOCK = 64
SEL_TOPK = 8
WINDOW = 512
N_BRANCH = 3
FORCED_BONUS = 1000.0
D_FF = 2816
EPS = 1e-6
LOG2E = math.log2(math.e)

LANES = 128
SUBLANES = 8
GATE_PAD = LANES
FF_CHUNK = 256
N_FF_CHUNKS = D_FF // FF_CHUNK
TM_PROJ = 512
CONV_HALO = 32
CONV_ROWS = TM_PROJ
TQ = 256
WIN_KEYS = WINDOW + TQ
SHORT_LIST = 4
TILES_PER_STEP = 4
MASK_NEG = -(2.0 ** 100)
M_INIT = -1e30
VMEM_LIMIT = 56 * 1024 * 1024


def _alibi_slopes(n):
    def pow2_slopes(m):
        start = 2.0 ** (-8.0 / m)
        return [start ** (i + 1) for i in range(m)]
    if math.log2(n).is_integer():
        s = pow2_slopes(n)
    else:
        c = 2 ** math.floor(math.log2(n))
        s = pow2_slopes(c) + pow2_slopes(2 * c)[0::2][: n - c]
    return np.asarray(s, dtype=np.float32)


def _sigmoid(v):
    return 1.0 / (1.0 + jnp.exp(-v))


def _dot(a, b):
    return jnp.dot(a, b, preferred_element_type=F32)


def _dot_nt(a, b):
    return lax.dot_general(a, b, (((1,), (1,)), ((), ())), preferred_element_type=F32)


def _const_spec(shape):
    nd = len(shape)
    return pl.BlockSpec(shape, lambda *_: (0,) * nd, pipeline_mode=pl.Buffered(1))


C_A = 0
C_Q = C_A + 2 * CONV_CH
C_KC = C_Q + NSA_WIDTH
C_VC = C_KC + KV_WIDTH
C_KS = C_VC + KV_WIDTH
C_VS = C_KS + KV_WIDTH
C_KW = C_VS + KV_WIDTH
C_VW = C_KW + KV_WIDTH
C_G = C_VW + KV_WIDTH
IN_COLS_PAD = C_G + GATE_PAD


def _conv_mixer(a, w_ref, b_ref, lg_ref, lb_ref, y_ref, o_ref):
    tm = a.shape[0]
    y_ref[CONV_HALO:CONV_HALO + tm, :] = a[:, :CONV_CH] * _sigmoid(a[:, CONV_CH:])
    first = CONV_HALO - (CONV_WIDTH - 1)
    for r0 in range(0, tm, CONV_ROWS):
        acc = b_ref[...]
        for rho in range(SUBLANES):
            n_rows = CONV_ROWS if rho == 0 else CONV_ROWS + SUBLANES
            z = None
            for base in range(0, CONV_HALO + 1, SUBLANES):
                k = base + rho - first
                if 0 <= k < CONV_WIDTH:
                    term = w_ref[k:k + 1, :] * y_ref[pl.ds(r0 + base, n_rows), :]
                    z = term if z is None else z + term
            acc = acc + (z if rho == 0 else z[rho:rho + CONV_ROWS, :])
        mu = jnp.mean(acc, axis=-1, keepdims=True)
        d = acc - mu
        var = jnp.mean(d * d, axis=-1, keepdims=True)
        yn = d * lax.rsqrt(var + EPS) * lg_ref[...] + lb_ref[...]
        o_ref[0, r0:r0 + CONV_ROWS, :] = (yn * _sigmoid(yn)).astype(o_ref.dtype)


def _inproj_kernel(x_ref, g_ref, w_ref, cw_ref, cb_ref, clg_ref, clb_ref,
                   conv_ref, q_ref, kcr_ref, vcr_ref, ksa_ref, vsa_ref, kw_ref, vwa_ref, gate_ref,
                   y_ref):
    tm = x_ref.shape[1]

    @pl.when(pl.program_id(1) == 0)
    def _zero_halo():
        y_ref[0:CONV_HALO, :] = jnp.zeros((CONV_HALO, CONV_CH), F32)

    @pl.when(pl.program_id(1) > 0)
    def _carry_halo():
        y_ref[0:CONV_HALO, :] = y_ref[tm:tm + CONV_HALO, :]

    x = x_ref[0]
    ms = jnp.mean(x * x, axis=-1, keepdims=True)
    h = (x * lax.rsqrt(ms + EPS) * g_ref[...]).astype(BF16)

    zq = _dot(h, w_ref[:, C_Q:C_KC]) * (HEAD_DIM ** -0.5 * LOG2E)
    zero_hi = jnp.zeros((tm, HEAD_DIM), BF16)
    for hd in range(N_Q_HEADS):
        qh = zq[:, hd * HEAD_DIM:(hd + 1) * HEAD_DIM].astype(BF16)
        q_ref[0, hd] = jnp.concatenate([qh, zero_hi], axis=1)
    zkc = _dot(h, w_ref[:, C_KC:C_VC])
    zvc = _dot(h, w_ref[:, C_VC:C_KS])
    for half in range(KV_WIDTH // LANES):
        kcr_ref[0, half] = zkc[:, half * LANES:(half + 1) * LANES]
        vcr_ref[0, half] = zvc[:, half * LANES:(half + 1) * LANES]

    a = _dot(h, w_ref[:, C_A:C_Q])
    _conv_mixer(a, cw_ref, cb_ref, clg_ref, clb_ref, y_ref, conv_ref)

    t = pl.program_id(1) * tm + lax.broadcasted_iota(jnp.int32, (tm, HEAD_DIM), 0)
    blk = lax.broadcasted_iota(jnp.int32, (tm, HEAD_DIM), 1)
    onehot = jnp.where(t // SEL_BLOCK == blk, 1.0, 0.0).astype(BF16)
    ones_col = jnp.where(blk == 0, 1.0, 0.0).astype(BF16)
    zks = _dot(h, w_ref[:, C_KS:C_VS])
    zvs = _dot(h, w_ref[:, C_VS:C_KW])
    zkw = _dot(h, w_ref[:, C_KW:C_VW])
    zvw = _dot(h, w_ref[:, C_VW:C_G])
    for g in range(N_KV_HEADS):
        sl = slice(g * HEAD_DIM, (g + 1) * HEAD_DIM)
        ksa_ref[0, g] = jnp.concatenate([zks[:, sl].astype(BF16), onehot], axis=1)
        vsa_ref[0, g] = jnp.concatenate([zvs[:, sl].astype(BF16), ones_col], axis=1)
        kw_ref[0, g] = zkw[:, sl].astype(BF16)
        vwa_ref[0, g] = jnp.concatenate([zvw[:, sl].astype(BF16), ones_col], axis=1)
    gate_ref[0] = _sigmoid(_dot(h, w_ref[:, C_G:IN_COLS_PAD]))


def _inproj(x, norm_g, w_pad, conv_w, conv_b, conv_ln_g, conv_ln_b):
    b, t, _ = x.shape
    tm = min(TM_PROJ, t)
    grid = (b, t // tm)
    tok = lambda bi, i: (bi, i, 0)
    head = lambda bi, i: (bi, 0, i, 0)
    out_shape = (
        jax.ShapeDtypeStruct((b, t, CONV_CH), BF16),
        jax.ShapeDtypeStruct((b, N_Q_HEADS, t, 2 * HEAD_DIM), BF16),
        jax.ShapeDtypeStruct((b, KV_WIDTH // LANES, t, LANES), F32),
        jax.ShapeDtypeStruct((b, KV_WIDTH // LANES, t, LANES), F32),
        jax.ShapeDtypeStruct((b, N_KV_HEADS, t, 2 * HEAD_DIM), BF16),
        jax.ShapeDtypeStruct((b, N_KV_HEADS, t, 2 * HEAD_DIM), BF16),
        jax.ShapeDtypeStruct((b, N_KV_HEADS, t, HEAD_DIM), BF16),
        jax.ShapeDtypeStruct((b, N_KV_HEADS, t, 2 * HEAD_DIM), BF16),
        jax.ShapeDtypeStruct((b, t, GATE_PAD), F32),
    )
    out_specs = (
        pl.BlockSpec((1, tm, CONV_CH), tok),
        pl.BlockSpec((1, N_Q_HEADS, tm, 2 * HEAD_DIM), head),
        pl.BlockSpec((1, KV_WIDTH // LANES, tm, LANES), head),
        pl.BlockSpec((1, KV_WIDTH // LANES, tm, LANES), head),
        pl.BlockSpec((1, N_KV_HEADS, tm, 2 * HEAD_DIM), head),
        pl.BlockSpec((1, N_KV_HEADS, tm, 2 * HEAD_DIM), head),
        pl.BlockSpec((1, N_KV_HEADS, tm, HEAD_DIM), head),
        pl.BlockSpec((1, N_KV_HEADS, tm, 2 * HEAD_DIM), head),
        pl.BlockSpec((1, tm, GATE_PAD), tok),
    )
    return pl.pallas_call(
        _inproj_kernel,
        grid=grid,
        in_specs=[
            pl.BlockSpec((1, tm, D_MODEL), tok),
            _const_spec((1, D_MODEL)),
            _const_spec((D_MODEL, IN_COLS_PAD)),
            _const_spec((CONV_HALO, CONV_CH)),
            _const_spec((1, CONV_CH)),
            _const_spec((1, CONV_CH)),
            _const_spec((1, CONV_CH)),
        ],
        out_specs=out_specs,
        out_shape=out_shape,
        scratch_shapes=[pltpu.VMEM((CONV_HALO + tm, CONV_CH), F32)],
        compiler_params=pltpu.CompilerParams(
            dimension_semantics=("arbitrary", "arbitrary"), vmem_limit_bytes=VMEM_LIMIT),
        name="inproj_conv",
    )(x, norm_g, w_pad, conv_w, conv_b, conv_ln_g, conv_ln_b)


def _compress_kernel(kr_ref, vr_ref, kpe_ref, kw1_ref, kw2_ref, vpe_ref, vw1_ref, vw2_ref,
                     kc_ref, vc_ref):
    nch = kr_ref.shape[2] // CMP_STRIDE
    row = lax.broadcasted_iota(jnp.int32, (nch, HEAD_DIM), 0)
    c_start = row * CMP_STRIDE
    s_start = lax.broadcasted_iota(jnp.int32, (nch, HEAD_DIM), 1) * SEL_BLOCK
    overlap = jnp.where((c_start < s_start + SEL_BLOCK) & (c_start + CMP_LEN > s_start),
                        1.0, 0.0).astype(BF16)
    for r_ref, pe_ref, w1_ref, w2_ref, o_ref, tail in (
            (kr_ref, kpe_ref, kw1_ref, kw2_ref, kc_ref, None),
            (vr_ref, vpe_ref, vw1_ref, vw2_ref, vc_ref, overlap)):
        pb = _dot(pe_ref[...], w1_ref[...])
        bias = pb[0:1, :CMP_HIDDEN] + pb[8:9, CMP_HIDDEN:]
        rows = [[r_ref[0, half, pl.ds(l, nch, stride=CMP_STRIDE), :] for l in range(CMP_STRIDE)]
                for half in range(KV_WIDTH // LANES)]
        for g in range(N_KV_HEADS):
            half, odd = divmod(g, LANES // HEAD_DIM)
            sl = slice(odd * HEAD_DIM, (odd + 1) * HEAD_DIM)
            xg = jnp.concatenate([rw[:, sl] for rw in rows[half]], axis=1).astype(BF16)
            lohi = _dot(xg, w1_ref[...])
            lo = lohi[:, :CMP_HIDDEN]
            hi = lohi[:, CMP_HIDDEN:]
            hi_next = jnp.concatenate([hi[1:], jnp.zeros((1, CMP_HIDDEN), F32)], axis=0)
            hid = lo + hi_next + bias
            act = (hid * _sigmoid(hid)).astype(BF16)
            out = _dot(act, w2_ref[...])
            out = jnp.where(row < nch - 1, out, 0.0)
            out = out.astype(o_ref.dtype)
            o_ref[0, g] = out if tail is None else jnp.concatenate([out, tail], axis=1)


def _compress(kc_r, vc_r, kpe, kw1, kw2, vpe, vw1, vw2):
    b, _, t, _ = kc_r.shape
    nch = t // CMP_STRIDE
    blk_len = CMP_STRIDE * HEAD_DIM
    out_sds = [jax.ShapeDtypeStruct((b, N_KV_HEADS, nch, w), BF16) for w in (HEAD_DIM, 2 * HEAD_DIM)]
    out_spec = [pl.BlockSpec((1, N_KV_HEADS, nch, w), lambda bi: (bi, 0, 0, 0))
                for w in (HEAD_DIM, 2 * HEAD_DIM)]
    raw_spec = pl.BlockSpec((1, KV_WIDTH // LANES, t, LANES), lambda bi: (bi, 0, 0, 0))
    w_specs = [_const_spec((16, blk_len)), _const_spec((blk_len, 2 * CMP_HIDDEN)),
               _const_spec((CMP_HIDDEN, HEAD_DIM))]
    return pl.pallas_call(
        _compress_kernel,
        grid=(b,),
        in_specs=[raw_spec, raw_spec] + w_specs + w_specs,
        out_specs=tuple(out_spec),
        out_shape=tuple(out_sds),
        compiler_params=pltpu.CompilerParams(
            dimension_semantics=("arbitrary",), vmem_limit_bytes=VMEM_LIMIT),
        name="compress",
    )(kc_r, vc_r, kpe, kw1, kw2, vpe, vw1, vw2)


def _nsa_kernel(slope_ref, q_ref, kc_ref, vca_ref, ksa_ref, vsa_ref, kw_ref, vwa_ref, gate_ref,
                wmask_ref, o_ref, se_ref, so_ref, mx_ref, pa_ref, pb_ref, acc_ref,
                qa_ref, pw_ref, p4_ref, vg_ref, m1_ref, acc1_ref, list_ref):
    g = pl.program_id(1)
    ncp = kc_ref.shape[2]
    n_chunks = list_ref.shape[1] - 2
    slopes = [slope_ref[g * Q_PER_KV + r] for r in range(Q_PER_KV)]
    rows = [slice(r * TQ, (r + 1) * TQ) for r in range(Q_PER_KV)]
    halves = [slice(hf * LANES, (hf + 1) * LANES) for hf in range(TQ // LANES)]
    lane_row = lax.broadcasted_iota(jnp.int32, (1, TQ), 1)
    causal = (lax.broadcasted_iota(jnp.int32, (TQ, TQ), 1)
              <= lax.broadcasted_iota(jnp.int32, (TQ, TQ), 0))

    def select(u):
        i = pl.program_id(2) * TILES_PER_STEP + u
        t0 = i * TQ
        tile = slice(u * TQ, (u + 1) * TQ)
        q_full = q_ref[0, :, tile, :].reshape(Q_PER_KV * TQ, 2 * HEAD_DIM)
        q3 = q_full[:, :HEAD_DIM]

        s_c = _dot_nt(q3, kc_ref[0, 0])
        n_idx = lax.broadcasted_iota(jnp.int32, (TQ, ncp), 1)
        t_idx = t0 + lax.broadcasted_iota(jnp.int32, (TQ, ncp), 0)
        c_valid = n_idx * CMP_STRIDE + (CMP_LEN - 1) <= t_idx
        n_row = lax.broadcasted_iota(jnp.int32, (1, ncp), 1)
        c_pos = (n_row * CMP_STRIDE - t0).astype(F32) + 0.5 * (CMP_LEN - 1)
        vca = vca_ref[0, 0]
        o_c = []
        imp = jnp.zeros((TQ, LANES), F32)
        for r in range(Q_PER_KV):
            sr = jnp.where(c_valid, s_c[rows[r]] + slopes[r] * c_pos, -jnp.inf)
            m = jnp.max(sr, axis=1, keepdims=True)
            m = jnp.where(m == -jnp.inf, 0.0, m)
            p = jnp.exp2(sr - m)
            inv = 1.0 / jnp.maximum(jnp.sum(p, axis=1, keepdims=True), 1e-30)
            pv = _dot(p.astype(BF16), vca) * inv
            o_c.append(pv[:, :HEAD_DIM])
            imp = imp + pv

        imp_t = imp.T[HEAD_DIM:, :]
        j_t = lax.broadcasted_iota(jnp.int32, (LANES - HEAD_DIM, TQ), 0)
        cur_t = (t0 + lax.broadcasted_iota(jnp.int32, (LANES - HEAD_DIM, TQ), 1)) // SEL_BLOCK
        forced = (j_t == 0) | (j_t == cur_t) | (j_t == cur_t - 1)
        cand = jnp.where(j_t <= cur_t, imp_t + jnp.where(forced, FORCED_BONUS, 0.0), -jnp.inf)
        sel = jnp.zeros(cand.shape, jnp.bool_)
        for _ in range(SEL_TOPK):
            mx = jnp.max(cand, axis=0, keepdims=True)
            first = jnp.min(jnp.where(cand == mx, j_t, LANES), axis=0, keepdims=True)
            hit = j_t == first
            sel = sel | (hit & (mx > -jnp.inf))
            cand = jnp.where(hit, -jnp.inf, cand)
        bias_t = jnp.concatenate([jnp.zeros(cand.shape, F32), jnp.where(sel, 0.0, MASK_NEG)], axis=0)
        sel_bias = bias_t.T
        sel_bias_bf = sel_bias.astype(BF16)
        for r in range(Q_PER_KV):
            qa_ref[u, rows[r], :] = q_full[rows[r]] + sel_bias_bf

        blk_any = jnp.broadcast_to(jnp.max(sel_bias, axis=0, keepdims=True), (8, LANES))
        per_chunk = TQ // SEL_BLOCK
        chunk_any = blk_any
        for d in range(1, per_chunk):
            chunk_any = jnp.maximum(chunk_any, pltpu.roll(blk_any, LANES - d, 1))
        blk8 = lax.broadcasted_iota(jnp.int32, (8, LANES), 1) - HEAD_DIM
        chunk_bit = jnp.where((blk8 >= 0) & (blk8 % per_chunk == 0),
                              jnp.left_shift(1, jnp.maximum(blk8, 0) // per_chunk), 0).astype(F32)
        flagged = jnp.where(chunk_any > 0.5 * MASK_NEG, chunk_bit, 0.0)
        bits = jnp.sum(flagged[0:1, :], axis=1, keepdims=True)[0, 0].astype(jnp.int32)
        cnt = jnp.int32(0)
        idle = jnp.int32(0)
        for c in range(n_chunks):
            below = c < i
            hit = (((bits >> c) & 1) == 1) & below
            list_ref[u, cnt] = jnp.int32(c)
            cnt = cnt + hit.astype(jnp.int32)
            idle = jnp.where(below & jnp.logical_not(hit), c, idle)
        filler = jnp.where(i + 1 < n_chunks, i + 1, idle)
        list_ref[u, cnt] = filler

        return i, t0, tile, q3, o_c, cnt, filler

    def window(u, i, t0, tile, q3, o_c):
        w0 = pl.multiple_of(jnp.maximum(t0 - WINDOW, 0), TQ)
        s_w = _dot_nt(q3, kw_ref[0, 0, pl.ds(w0, WIN_KEYS), :])
        wk_row = w0 - t0 + lax.broadcasted_iota(jnp.int32, (1, WIN_KEYS), 1)
        w_mask = wmask_ref[jnp.minimum(i, WINDOW // TQ)]
        wk_pos = wk_row.astype(F32)
        for r in range(Q_PER_KV):
            sr = s_w[rows[r]] + slopes[r] * wk_pos + w_mask
            m = jnp.max(sr, axis=1, keepdims=True)
            pw_ref[u, rows[r], :] = jnp.exp2((sr - m).astype(BF16))
        acc_w = _dot(pw_ref[u], vwa_ref[0, 0, pl.ds(w0, WIN_KEYS), :])

        gates = gate_ref[0, tile, :]
        lane = lax.broadcasted_iota(jnp.int32, (TQ, LANES), 1)
        partial, g_sel = [], []
        for r in range(Q_PER_KV):
            col = g * (Q_PER_KV * N_BRANCH) + r * N_BRANCH
            gc, gs, gw = (jnp.sum(jnp.where(lane == col + br, gates, 0.0), axis=1, keepdims=True)
                          for br in range(N_BRANCH))
            a = acc_w[rows[r]]
            o_w = a[:, :HEAD_DIM] / a[:, HEAD_DIM:HEAD_DIM + 1]
            partial.append(gc * o_c[r] + gw * o_w)
            g_sel.append(jnp.broadcast_to(gs, (TQ, HEAD_DIM)))

        return partial, g_sel

    def head_of_list(u, i, t0, cnt, filler):
        head = [jnp.where(cnt > j, list_ref[u, j], filler) for j in range(SHORT_LIST - 1)] + [i]
        scores = []
        for j, c in enumerate(head):
            k0 = pl.multiple_of(c * TQ, TQ)
            vg_ref[u, j * TQ:(j + 1) * TQ, :] = vsa_ref[0, 0, pl.ds(k0, TQ), :]
            scores.append(_dot_nt(qa_ref[u], ksa_ref[0, 0, pl.ds(k0, TQ), :]))
        for r in range(Q_PER_KV):
            biased = []
            for j, c in enumerate(head):
                sr = scores[j][rows[r]] + slopes[r] * (c * TQ - t0 + lane_row).astype(F32)
                biased.append(jnp.where(causal, sr, MASK_NEG) if j == SHORT_LIST - 1 else sr)
            part = biased[0][:, halves[0]]
            for sr in biased:
                for hf in halves:
                    part = jnp.maximum(part, sr[:, hf])
            m = jnp.max(part, axis=1, keepdims=True)
            m1_ref[u, rows[r], :] = jnp.broadcast_to(m, (TQ, LANES))
            for j, sr in enumerate(biased):
                p4_ref[u, rows[r], j * TQ:(j + 1) * TQ] = jnp.exp2((sr - m).astype(BF16))
        acc1_ref[u] = _dot(p4_ref[u], vg_ref[u])

    def rest_of_list(u, t0, cnt):
        k_last = (cnt - SHORT_LIST + 2) // 2 - 1

        def entry(pos):
            return list_ref[u, SHORT_LIST - 1 + pos]

        def raw_scores(pos, dst_ref, k):
            k0 = pl.multiple_of(entry(pos) * TQ, TQ)
            s_all = _dot_nt(qa_ref[u], ksa_ref[0, 0, pl.ds(k0, TQ), :])
            for r in range(Q_PER_KV):
                dst_ref[k, rows[r], :] = s_all[rows[r]]

        def alibi_row(pos, r):
            return slopes[r] * (entry(pos) * TQ - t0 + lane_row).astype(F32)

        def fold_max(pos, src_ref, k):
            for r in range(Q_PER_KV):
                sr = src_ref[k, rows[r], :] + alibi_row(pos, r)
                part = sr[:, halves[0]]
                for hf in halves[1:]:
                    part = jnp.maximum(part, sr[:, hf])
                mx_ref[rows[r], :] = jnp.maximum(mx_ref[rows[r], :], part)

        mx_ref[...] = jnp.full(mx_ref.shape, M_INIT, F32)
        raw_scores(0, se_ref, 0)

        def pass1(k, carry):
            raw_scores(2 * k + 1, so_ref, k)
            fold_max(2 * k, se_ref, k)
            raw_scores(2 * k + 2, se_ref, k + 1)
            fold_max(2 * k + 1, so_ref, k)
            return carry

        lax.fori_loop(0, k_last, pass1, 0)
        raw_scores(2 * k_last + 1, so_ref, k_last)
        fold_max(2 * k_last, se_ref, k_last)
        fold_max(2 * k_last + 1, so_ref, k_last)

        for r in range(Q_PER_KV):
            m = jnp.max(mx_ref[rows[r], :], axis=1, keepdims=True)
            mx_ref[rows[r], :] = jnp.broadcast_to(m, (TQ, LANES))
        acc_ref[...] = jnp.zeros(acc_ref.shape, F32)

        def probs(pos, src_ref, k, dst_ref):
            for r in range(Q_PER_KV):
                mb = mx_ref[rows[r], :]
                bias = alibi_row(pos, r)
                for hf in halves:
                    dst_ref[rows[r], hf] = jnp.exp2(src_ref[k, rows[r], hf] + bias[:, hf] - mb).astype(BF16)

        def add_pv(pos, src_ref):
            k0 = pl.multiple_of(entry(pos) * TQ, TQ)
            acc_ref[...] += _dot(src_ref[...], vsa_ref[0, 0, pl.ds(k0, TQ), :])

        probs(0, se_ref, 0, pa_ref)

        def pass2(k, carry):
            add_pv(2 * k, pa_ref)
            probs(2 * k + 1, so_ref, k, pb_ref)
            add_pv(2 * k + 1, pb_ref)
            probs(2 * k + 2, se_ref, k + 1, pa_ref)
            return carry

        lax.fori_loop(0, k_last, pass2, 0)
        add_pv(2 * k_last, pa_ref)
        probs(2 * k_last + 1, so_ref, k_last, pb_ref)
        add_pv(2 * k_last + 1, pb_ref)

        for r in range(Q_PER_KV):
            m_head = m1_ref[u, rows[r], :]
            m_rest = mx_ref[rows[r], :]
            m = jnp.maximum(m_head, m_rest)
            acc1_ref[u, rows[r], :] = (acc1_ref[u, rows[r], :] * jnp.exp2(m_head - m)
                                       + acc_ref[rows[r], :] * jnp.exp2(m_rest - m))

    states = []
    picked = []
    sel_out = [select(u) for u in range(TILES_PER_STEP)]
    for u in range(TILES_PER_STEP):
        i, t0, tile, q3, o_c, cnt, filler = sel_out[u]
        partial, g_sel = window(u, i, t0, tile, q3, o_c)
        picked.append((i, t0, cnt, filler))
        states.append((t0, cnt, partial, g_sel))
    for u, (i, t0, cnt, filler) in enumerate(picked):
        head_of_list(u, i, t0, cnt, filler)
    for u, (t0, cnt, _, _) in enumerate(states):
        pl.when(cnt > SHORT_LIST - 1)(functools.partial(rest_of_list, u, t0, cnt))
    for u, (_, _, partial, g_sel) in enumerate(states):
        outs = []
        for r in range(Q_PER_KV):
            a = acc1_ref[u, rows[r], :]
            o_s = a[:, :HEAD_DIM] / a[:, HEAD_DIM:HEAD_DIM + 1]
            outs.append(partial[r] + g_sel[r] * o_s)
        o_ref[0, 0, u * TQ:(u + 1) * TQ, :] = jnp.concatenate(outs, axis=1).astype(o_ref.dtype)


def _window_masks():
    out = []
    for v in range(WINDOW // TQ + 1):
        rel0 = -v * TQ
        dist = np.arange(TQ)[:, None] - (rel0 + np.arange(WIN_KEYS)[None, :])
        out.append(np.where((dist >= 0) & (dist < WINDOW), 0.0, MASK_NEG))
    return jnp.asarray(np.stack(out), F32)


def _nsa(q, kc, vca, ksa, vsa, kw, vwa, gates, slopes):
    b, _, t, _ = q.shape
    ncp = kc.shape[2]
    tstep = TILES_PER_STEP * TQ
    grp = lambda bi, g, i: (bi, g, 0, 0)
    per_tile = lambda shape, dtype: pltpu.VMEM((TILES_PER_STEP,) + shape, dtype)
    return pl.pallas_call(
        _nsa_kernel,
        grid=(b, N_KV_HEADS, t // tstep),
        in_specs=[
            pl.BlockSpec(memory_space=pltpu.SMEM),
            pl.BlockSpec((1, Q_PER_KV, tstep, 2 * HEAD_DIM), lambda bi, g, i: (bi, g, i, 0)),
            pl.BlockSpec((1, 1, ncp, HEAD_DIM), grp),
            pl.BlockSpec((1, 1, ncp, 2 * HEAD_DIM), grp),
            pl.BlockSpec((1, 1, t, 2 * HEAD_DIM), grp),
            pl.BlockSpec((1, 1, t, 2 * HEAD_DIM), grp),
            pl.BlockSpec((1, 1, t, HEAD_DIM), grp),
            pl.BlockSpec((1, 1, t, 2 * HEAD_DIM), grp),
            pl.BlockSpec((1, tstep, GATE_PAD), lambda bi, g, i: (bi, i, 0)),
            _const_spec((WINDOW // TQ + 1, TQ, WIN_KEYS)),
        ],
        out_specs=pl.BlockSpec((1, 1, tstep, Q_PER_KV * HEAD_DIM), lambda bi, g, i: (bi, g, i, 0)),
        out_shape=jax.ShapeDtypeStruct((b, N_KV_HEADS, t, Q_PER_KV * HEAD_DIM), BF16),
        scratch_shapes=[
            pltpu.VMEM((t // (2 * TQ), Q_PER_KV * TQ, TQ), F32),
            pltpu.VMEM((t // (2 * TQ), Q_PER_KV * TQ, TQ), F32),
            pltpu.VMEM((Q_PER_KV * TQ, LANES), F32),
            pltpu.VMEM((Q_PER_KV * TQ, TQ), BF16),
            pltpu.VMEM((Q_PER_KV * TQ, TQ), BF16),
            pltpu.VMEM((Q_PER_KV * TQ, LANES), F32),
            per_tile((Q_PER_KV * TQ, 2 * HEAD_DIM), BF16),
            per_tile((Q_PER_KV * TQ, WIN_KEYS), BF16),
            per_tile((Q_PER_KV * TQ, SHORT_LIST * TQ), BF16),
            per_tile((SHORT_LIST * TQ, 2 * HEAD_DIM), BF16),
            per_tile((Q_PER_KV * TQ, LANES), F32),
            per_tile((Q_PER_KV * TQ, LANES), F32),
            pltpu.SMEM((TILES_PER_STEP, t // TQ + 2), jnp.int32),
        ],
        compiler_params=pltpu.CompilerParams(
            dimension_semantics=("arbitrary", "arbitrary", "arbitrary"),
            vmem_limit_bytes=VMEM_LIMIT),
        name="nsa_attention",
    )(slopes, q, kc, vca, ksa, vsa, kw, vwa, gates, _window_masks())


def _outffn_kernel(x_ref, c_ref, n_ref, woc_ref, won_ref, fg_ref, wgu_ref, wd_ref, fin_ref,
                   o_ref, acc_ref, h_ref, act_ref, *, final):
    x1 = x_ref[0] + _dot(c_ref[0], woc_ref[...])
    for g in range(N_KV_HEADS):
        x1 = x1 + _dot(n_ref[0, g], won_ref[g])
    ms = jnp.mean(x1 * x1, axis=-1, keepdims=True)
    h_ref[...] = (x1 * lax.rsqrt(ms + EPS) * fg_ref[...]).astype(BF16)
    acc_ref[...] = x1

    for c in range(N_FF_CHUNKS):
        h = h_ref[...]
        gate = _dot(h, wgu_ref[:, c * FF_CHUNK:(c + 1) * FF_CHUNK])
        up = _dot(h, wgu_ref[:, D_FF + c * FF_CHUNK:D_FF + (c + 1) * FF_CHUNK])
        act_ref[:, c * FF_CHUNK:(c + 1) * FF_CHUNK] = (gate * _sigmoid(gate) * up).astype(BF16)
    y = acc_ref[...] + _dot(act_ref[...], wd_ref[...])
    if final:
        ms = jnp.mean(y * y, axis=-1, keepdims=True)
        y = y * lax.rsqrt(ms + EPS) * fin_ref[...]
    o_ref[0] = y


def _outffn(x, conv_out, nsa_out, wo_c, wo_n, ffn_g, wgu, wd, final_g, final):
    b, t, _ = x.shape
    tm = min(TM_PROJ, t)
    tok = lambda bi, i: (bi, i, 0)
    return pl.pallas_call(
        functools.partial(_outffn_kernel, final=final),
        grid=(b, t // tm),
        in_specs=[
            pl.BlockSpec((1, tm, D_MODEL), tok),
            pl.BlockSpec((1, tm, CONV_CH), tok),
            pl.BlockSpec((1, N_KV_HEADS, tm, Q_PER_KV * HEAD_DIM), lambda bi, i: (bi, 0, i, 0)),
            _const_spec((CONV_CH, D_MODEL)),
            _const_spec((N_KV_HEADS, Q_PER_KV * HEAD_DIM, D_MODEL)),
            _const_spec((1, D_MODEL)),
            _const_spec((D_MODEL, 2 * D_FF)),
            _const_spec((D_FF, D_MODEL)),
            _const_spec((1, D_MODEL)),
        ],
        out_specs=pl.BlockSpec((1, tm, D_MODEL), tok),
        out_shape=jax.ShapeDtypeStruct((b, t, D_MODEL), F32),
        scratch_shapes=[pltpu.VMEM((tm, D_MODEL), F32), pltpu.VMEM((tm, D_MODEL), BF16),
                        pltpu.VMEM((tm, D_FF), BF16)],
        compiler_params=pltpu.CompilerParams(
            dimension_semantics=("arbitrary", "arbitrary"), vmem_limit_bytes=VMEM_LIMIT),
        name="outproj_ffn",
    )(x, conv_out, nsa_out, wo_c, wo_n, ffn_g, wgu, wd, final_g)


def _pe_rows(pe):
    lo = pe[:CMP_STRIDE].reshape(1, -1)
    hi = pe[CMP_STRIDE:].reshape(1, -1)
    z = jnp.zeros((7, lo.shape[1]), pe.dtype)
    return jnp.concatenate([lo, z, hi, z], axis=0).astype(BF16)


def _w1_cat(w1):
    lo = w1[:CMP_STRIDE].reshape(CMP_STRIDE * HEAD_DIM, CMP_HIDDEN)
    hi = w1[CMP_STRIDE:].reshape(CMP_STRIDE * HEAD_DIM, CMP_HIDDEN)
    return jnp.concatenate([lo, hi], axis=1).astype(BF16)


def kernel(x, attn_norm, w_in, conv_w, conv_b, conv_ln_g, conv_ln_b, cmp_k_pe, cmp_k_w1, cmp_k_w2,
           cmp_v_pe, cmp_v_w1, cmp_v_w2, w_out, ffn_norm, w_gate_up, w_down, final_norm):
    depth = w_in.shape[0]
    t = x.shape[1]
    assert t % (2 * TQ) == 0 and t % (TILES_PER_STEP * TQ) == 0
    assert t // SEL_BLOCK <= LANES - HEAD_DIM and t >= WIN_KEYS
    slopes = jnp.asarray(_alibi_slopes(N_Q_HEADS) * np.float32(LOG2E))
    final_g = final_norm.reshape(1, D_MODEL)
    for l in range(depth):
        w_pad = jnp.pad(w_in[l], ((0, 0), (0, IN_COLS_PAD - w_in.shape[2]))).astype(BF16)
        conv_out, q, kc_r, vc_r, ksa, vsa, kw, vwa, gates = _inproj(
            x, attn_norm[l].reshape(1, D_MODEL), w_pad,
            jnp.pad(conv_w[l], ((0, CONV_HALO - CONV_WIDTH), (0, 0))),
            conv_b[l].reshape(1, CONV_CH), conv_ln_g[l].reshape(1, CONV_CH),
            conv_ln_b[l].reshape(1, CONV_CH))
        kc, vca = _compress(kc_r, vc_r,
                            _pe_rows(cmp_k_pe[l]), _w1_cat(cmp_k_w1[l]), cmp_k_w2[l].astype(BF16),
                            _pe_rows(cmp_v_pe[l]), _w1_cat(cmp_v_w1[l]), cmp_v_w2[l].astype(BF16))
        nsa_out = _nsa(q, kc, vca, ksa, vsa, kw, vwa, gates, slopes)
        wo = w_out[l].astype(BF16)
        wo_n = wo[CONV_CH:].reshape(N_KV_HEADS, Q_PER_KV * HEAD_DIM, D_MODEL)
        x = _outffn(x, conv_out, nsa_out, wo[:CONV_CH], wo_n, ffn_norm[l].reshape(1, D_MODEL),
                    w_gate_up[l].astype(BF16), w_down[l].astype(BF16), final_g,
                    final=(l == depth - 1))
    return x
```
